```python
import jax, jax.numpy as jnp
from jax import lax
import numpy as np

D_MODEL = 1024
BATCH = 8
SEQ = 4096
DEPTH = 4

POOL_WINDOWS = (2, 4, 8, 16)
POOL_GROUPS = len(POOL_WINDOWS)
POOL_WIDTH = D_MODEL // 2
POOL_GW = POOL_WIDTH // POOL_GROUPS
N_HEADS = 16
HEAD_DIM = 64
N_KV_GROUPS = 2
HPG = N_HEADS // N_KV_GROUPS
NSA_WIDTH = N_HEADS * HEAD_DIM
KV_WIDTH = N_KV_GROUPS * HEAD_DIM
CMP_LEN = 32
CMP_STRIDE = 16
CMP_HIDDEN = 256
SEL_BLOCK = 64
N_SEL = 16
WINDOW = 512
Q_BLOCK = 64
SEL_BONUS = 1e4
NEG_INF = -1e30
ROPE_THETA = 10000.0
D_FF = 4 * D_MODEL
RMS_EPS = 1e-6
IN_SIZES = (POOL_WIDTH, NSA_WIDTH, KV_WIDTH, KV_WIDTH, KV_WIDTH, KV_WIDTH, KV_WIDTH, KV_WIDTH, 3 * N_HEADS, 2 * D_MODEL)
N_IN = sum(IN_SIZES)
IN_SPLITS = tuple(int(v) for v in np.cumsum(IN_SIZES)[:-1])

kernel_name = 'hybrid_pool_nsa_gated_trunk'


def rms_norm(x, g):
    xf = x.astype(jnp.float32)
    y = xf * lax.rsqrt(jnp.mean(xf * xf, axis=-1, keepdims=True) + RMS_EPS)
    return (y * g.astype(jnp.float32)).astype(x.dtype)


def rope_tables(pos):
    inv = ROPE_THETA ** (-jnp.arange(0, HEAD_DIM, 2, dtype=jnp.float32) / HEAD_DIM)
    ang = pos.astype(jnp.float32)[:, None] * inv[None, :]
    ang = jnp.concatenate([ang, ang], axis=-1)
    return jnp.cos(ang), jnp.sin(ang)


def apply_rope(x, cos, sin):
    x1, x2 = jnp.split(x, 2, axis=-1)
    rot = jnp.concatenate([-x2, x1], axis=-1)
    y = x.astype(jnp.float32) * cos[:, None, :] + rot.astype(jnp.float32) * sin[:, None, :]
    return y.astype(x.dtype)


def masked_softmax(s, mask):
    s = jnp.where(mask, s.astype(jnp.float32), NEG_INF)
    return jnp.where(mask, jax.nn.softmax(s, axis=-1), 0.0)


def pool_mixer(u, w_pool, pool_scale):
    B, S, _ = u.shape
    uf = u.astype(jnp.float32)
    csum = jnp.pad(jnp.cumsum(uf, axis=1), ((0, 0), (1, 0), (0, 0)))
    t = jnp.arange(S)
    diffs = []
    for g, w in enumerate(POOL_WINDOWS):
        c = csum[:, :, g * POOL_GW:(g + 1) * POOL_GW]
        upper = c[:, 1:]
        lower = jnp.pad(c, ((0, 0), (w - 1, 0), (0, 0)))[:, :S]
        count = jnp.minimum(t + 1, w).astype(jnp.float32)[None, :, None]
        diffs.append((upper - lower) / count - uf[:, :, g * POOL_GW:(g + 1) * POOL_GW])
    d = jnp.stack(diffs, axis=2)
    y = jnp.einsum('bsgc,gcd->bsgd', d, w_pool.astype(jnp.float32)).reshape(B, S, POOL_WIDTH)
    return (y * pool_scale.astype(jnp.float32)).astype(u.dtype)


def compress(k, pe, w1, w2):
    B, S, G, dh = k.shape
    r = CMP_LEN // CMP_STRIDE
    n_chunks = S // CMP_STRIDE
    n_cmp = n_chunks - r + 1
    ch = k.reshape(B, n_chunks, CMP_STRIDE, G, dh)
    blocks = jnp.concatenate([ch[:, j:j + n_cmp] for j in range(r)], axis=2)
    blocks = blocks + pe[None, None, :, None, :]
    flat = blocks.transpose(0, 1, 3, 2, 4).reshape(B, n_cmp, G, CMP_LEN * dh)
    return jax.nn.gelu(flat @ w1) @ w2


def nsa_mixer(q, kc, vc, ks, vs, kw, vw, g_nsa, pe_k, pe_v, w_ck1, w_ck2, w_cv1, w_cv2):
    B, S = q.shape[:2]
    G, dh = N_KV_GROUPS, HEAD_DIM
    cos, sin = rope_tables(jnp.arange(S))
    q = apply_rope(q.reshape(B, S, N_HEADS, dh), cos, sin).reshape(B, S, G, HPG, dh)
    ks = apply_rope(ks.reshape(B, S, G, dh), cos, sin)
    kw = apply_rope(kw.reshape(B, S, G, dh), cos, sin)
    vs = vs.reshape(B, S, G, dh)
    vw = vw.reshape(B, S, G, dh)
    k_cmp = compress(kc.reshape(B, S, G, dh), pe_k, w_ck1, w_ck2)
    v_cmp = compress(vc.reshape(B, S, G, dh), pe_v, w_cv1, w_cv2)
    n_cmp = k_cmp.shape[1]
    cmp_start = jnp.arange(n_cmp) * CMP_STRIDE
    cmp_end = cmp_start + CMP_LEN - 1
    ccos, csin = rope_tables(cmp_end)
    k_cmp = apply_rope(k_cmp, ccos, csin)
    n_slc = S // SEL_BLOCK
    k_blk = ks.reshape(B, n_slc, SEL_BLOCK, G, dh).transpose(0, 3, 1, 2, 4)
    v_blk = vs.reshape(B, n_slc, SEL_BLOCK, G, dh).transpose(0, 3, 1, 2, 4)
    slc_start = jnp.arange(n_slc) * SEL_BLOCK
    slc_idx = jnp.arange(n_slc)
    overlap = ((cmp_start[:, None] <= slc_start[None, :] + SEL_BLOCK - 1)
               & (cmp_end[:, None] >= slc_start[None, :])).astype(jnp.float32)
    n_pick = min(N_SEL, n_slc)
    gather_blocks = jax.vmap(jax.vmap(lambda kb, i: kb[i]))
    win_len = Q_BLOCK + WINDOW - 1
    kw_pad = jnp.pad(kw, ((0, 0), (WINDOW - 1, 0), (0, 0), (0, 0)))
    vw_pad = jnp.pad(vw, ((0, 0), (WINDOW - 1, 0), (0, 0), (0, 0)))
    scale = HEAD_DIM ** -0.5
    gates = jax.nn.sigmoid(g_nsa.astype(jnp.float32)).reshape(B, S, G, HPG, 3)

    def block(s0):
        t = s0 + jnp.arange(Q_BLOCK)
        qb = lax.dynamic_slice_in_dim(q, s0, Q_BLOCK, axis=1)
        gb = lax.dynamic_slice_in_dim(gates, s0, Q_BLOCK, axis=1)
        s_c = jnp.einsum('bqghd,bngd->bghqn', qb, k_cmp) * scale
        p_c = masked_softmax(s_c, cmp_end[None, :] <= t[:, None])
        o_c = jnp.einsum('bghqn,bngd->bqghd', p_c, v_cmp)
        imp = jnp.einsum('bghqn,nj->bgqj', p_c, overlap)
        causal = slc_start[None, :] <= t[:, None]
        cur = t // SEL_BLOCK
        forced = causal & ((slc_idx[None, :] == 0) | (slc_idx[None, :] >= cur[:, None] - 1))
        score = jnp.where(forced, SEL_BONUS, jnp.where(causal, imp, NEG_INF))
        top_v, idx = lax.top_k(score, n_pick)
        sel_ok = top_v > 0.5 * NEG_INF
        idx_flat = idx.reshape(B, G, Q_BLOCK * n_pick)
        k_sel = gather_blocks(k_blk, idx_flat).reshape(B, G, Q_BLOCK, n_pick * SEL_BLOCK, dh)
        v_sel = gather_blocks(v_blk, idx_flat).reshape(B, G, Q_BLOCK, n_pick * SEL_BLOCK, dh)
        key_pos = idx[..., None] * SEL_BLOCK + jnp.arange(SEL_BLOCK)
        ok = (sel_ok[..., None] & (key_pos <= t[:, None, None])).reshape(B, G, Q_BLOCK, n_pick * SEL_BLOCK)
        s_s = jnp.einsum('bqghd,bgqkd->bghqk', qb, k_sel) * scale
        p_s = masked_softmax(s_s, ok[:, :, None])
        o_s = jnp.einsum('bghqk,bgqkd->bqghd', p_s, v_sel)
        kwb = lax.dynamic_slice_in_dim(kw_pad, s0, win_len, axis=1)
        vwb = lax.dynamic_slice_in_dim(vw_pad, s0, win_len, axis=1)
        kpos = s0 - (WINDOW - 1) + jnp.arange(win_len)
        wmask = (kpos[None, :] <= t[:, None]) & (kpos[None, :] > t[:, None] - WINDOW) & (kpos[None, :] >= 0)
        s_w = jnp.einsum('bqghd,bkgd->bghqk', qb, kwb) * scale
        p_w = masked_softmax(s_w, wmask)
        o_w = jnp.einsum('bghqk,bkgd->bqghd', p_w, vwb)
        o = gb[..., 0:1] * o_c + gb[..., 1:2] * o_s + gb[..., 2:3] * o_w
        return o.reshape(B, Q_BLOCK, NSA_WIDTH)

    out = lax.map(block, jnp.arange(S // Q_BLOCK) * Q_BLOCK)
    return out.transpose(1, 0, 2, 3).reshape(B, S, NSA_WIDTH)


def setup_inputs(seed: int = 0) -> dict:
    key = jax.random.key(seed)
    ks = jax.random.split(key, 20)
    f32 = jnp.float32

    def nrm(k, shape, fan_in):
        return jax.random.normal(k, shape, f32) * (fan_in ** -0.5)

    return {
        'x': jax.random.normal(ks[0], (BATCH, SEQ, D_MODEL), f32),
        'norm_mix': 1.0 + 0.05 * jax.random.normal(ks[1], (DEPTH, D_MODEL), f32),
        'w_in': nrm(ks[2], (DEPTH, D_MODEL, N_IN), D_MODEL),
        'w_pool': nrm(ks[3], (DEPTH, POOL_GROUPS, POOL_GW, POOL_GW), POOL_GW),
        'pool_scale': 1.0 + 0.1 * jax.random.normal(ks[4], (DEPTH, POOL_WIDTH), f32),
        'pe_k': 0.1 * jax.random.normal(ks[5], (DEPTH, CMP_LEN, HEAD_DIM), f32),
        'pe_v': 0.1 * jax.random.normal(ks[6], (DEPTH, CMP_LEN, HEAD_DIM), f32),
        'w_ck1': nrm(ks[7], (DEPTH, CMP_LEN * HEAD_DIM, CMP_HIDDEN), CMP_LEN * HEAD_DIM),
        'w_ck2': nrm(ks[8], (DEPTH, CMP_HIDDEN, HEAD_DIM), CMP_HIDDEN),
        'w_cv1': nrm(ks[9], (DEPTH, CMP_LEN * HEAD_DIM, CMP_HIDDEN), CMP_LEN * HEAD_DIM),
        'w_cv2': nrm(ks[10], (DEPTH, CMP_HIDDEN, HEAD_DIM), CMP_HIDDEN),
        'w_proj_pool': nrm(ks[11], (DEPTH, POOL_WIDTH, D_MODEL), POOL_WIDTH),
        'w_proj_nsa': nrm(ks[12], (DEPTH, NSA_WIDTH, D_MODEL), NSA_WIDTH),
        'w_out': nrm(ks[13], (DEPTH, D_MODEL, D_MODEL), D_MODEL),
        'norm_mlp': 1.0 + 0.05 * jax.random.normal(ks[14], (DEPTH, D_MODEL), f32),
        'w_ff1': nrm(ks[15], (DEPTH, D_MODEL, D_FF), D_MODEL),
        'w_ff2': nrm(ks[16], (DEPTH, D_FF, D_MODEL), D_FF),
        'norm_final': 1.0 + 0.05 * jax.random.normal(ks[17], (D_MODEL,), f32),
    }


def reference(x, norm_mix, w_in, w_pool, pool_scale, pe_k, pe_v, w_ck1, w_ck2, w_cv1, w_cv2,
              w_proj_pool, w_proj_nsa, w_out, norm_mlp, w_ff1, w_ff2, norm_final):
    for l in range(DEPTH):
        h = rms_norm(x, norm_mix[l])
        proj = h @ w_in[l]
        u, q, kc, vc, ksl, vsl, kwn, vwn, g_nsa, g_merge = jnp.split(proj, IN_SPLITS, axis=-1)
        y_pool = pool_mixer(u, w_pool[l], pool_scale[l])
        y_nsa = nsa_mixer(q, kc, vc, ksl, vsl, kwn, vwn, g_nsa, pe_k[l], pe_v[l],
                          w_ck1[l], w_ck2[l], w_cv1[l], w_cv2[l]).astype(x.dtype)
        g_a, g_b = jnp.split(jax.nn.sigmoid(g_merge.astype(jnp.float32)).astype(x.dtype), 2, axis=-1)
        merged = g_a * (y_pool @ w_proj_pool[l]) + g_b * (y_nsa @ w_proj_nsa[l])
        x = x + merged @ w_out[l]
        h = rms_norm(x, norm_mlp[l])
        x = x + jnp.square(jax.nn.relu(h @ w_ff1[l])) @ w_ff2[l]
    return rms_norm(x, norm_final)
```

```python
import functools

import jax
import jax.numpy as jnp
import numpy as np
from jax import lax
from jax.experimental import pallas as pl
from jax.experimental.pallas import tpu as pltpu

F32 = jnp.float32
BF16 = jnp.bfloat16

D_MODEL = 1024
POOL_WINDOWS = (2, 4, 8, 16)
POOL_WIDTH = 512
POOL_GW = 128
N_HEADS = 16
HEAD_DIM = 64
N_KV_GROUPS = 2
HPG = 8
NSA_WIDTH = 1024
KV_WIDTH = 128
CMP_LEN = 32
CMP_STRIDE = 16
CMP_HIDDEN = 256
SEL_BLOCK = 64
N_SEL = 16
WINDOW = 512
SEL_BONUS = 1e4
NEG_INF = -1e30
ROPE_THETA = 10000.0
D_FF = 4096
RMS_EPS = 1e-6
N_GATE = 3 * N_HEADS

LANES = 128
VMEM_LIMIT = 56 * 1024 * 1024

C_U = 0
C_Q = C_U + POOL_WIDTH
C_KV = C_Q + NSA_WIDTH
C_GM = C_KV + 6 * KV_WIDTH
C_GN = C_GM + 2 * D_MODEL
N_INP = C_GN + LANES

TM_IN = 512
TS_POOL = 512
POOL_HALO = 16
TM_MLP = 512
FF_CHUNK = 1024
TQ = 128
M_ATT = HPG * TQ
NK_SEL = 256
NK_WIN = 128
VT_CHUNK = 128

_NT = (((1,), (1,)), ((), ()))


def _dot(a, b):
    return jnp.dot(a, b, preferred_element_type=F32)


def _dot_nt(a, b):
    return lax.dot_general(a, b, _NT, preferred_element_type=F32)


def _rms(x, g):
    return x * lax.rsqrt(jnp.mean(x * x, axis=-1, keepdims=True) + RMS_EPS) * g


def _rope(t, cos, sin_lo, sin_hi):
    return t * cos + pltpu.roll(t, LANES - 32, 1) * sin_lo + pltpu.roll(t, 32, 1) * sin_hi


def _resident(shape, index_map):
    return pl.BlockSpec(shape, index_map, pipeline_mode=pl.Buffered(1))


def _inproj_kernel(x_ref, g_ref, w_ref, cos_ref, slo_ref, shi_ref,
                   u_ref, q_ref, kcvc_ref, ks_ref, vst_ref, kw_ref, vwt_ref, gm_ref, gn_ref):
    h = _rms(x_ref[...], g_ref[...]).astype(BF16)
    cos, slo, shi = cos_ref[...], slo_ref[...], shi_ref[...]
    n_chunks = TM_IN // VT_CHUNK

    u_ref[...] = _dot(h, w_ref[:, C_U:C_Q])
    q = _dot(h, w_ref[:, C_Q:C_KV])
    for k in range(NSA_WIDTH // LANES):
        sl = slice(k * LANES, (k + 1) * LANES)
        q_ref[:, sl] = (_rope(q[:, sl], cos, slo, shi) * (HEAD_DIM ** -0.5)).astype(BF16)
    kv = _dot(h, w_ref[:, C_KV:C_GM])
    kcvc_ref[...] = kv[:, 0:2 * KV_WIDTH]
    ks_ref[...] = _rope(kv[:, 2 * KV_WIDTH:3 * KV_WIDTH], cos, slo, shi).astype(BF16)
    vs_t = kv[:, 3 * KV_WIDTH:4 * KV_WIDTH].T
    kw_ref[...] = _rope(kv[:, 4 * KV_WIDTH:5 * KV_WIDTH], cos, slo, shi).astype(BF16)
    vw_t = kv[:, 5 * KV_WIDTH:6 * KV_WIDTH].T
    for c in range(n_chunks):
        sl = slice(c * VT_CHUNK, (c + 1) * VT_CHUNK)
        vst_ref[c] = vs_t[:, sl].astype(BF16)
        vwt_ref[c] = vw_t[:, sl].astype(BF16)
    gm_ref[...] = _dot(h, w_ref[:, C_GM:C_GN])
    gn_ref[...] = jax.nn.sigmoid(_dot(h, w_ref[:, C_GN:N_INP]))


def _inproj(x2, g, w, cos, slo, shi, seq):
    n = x2.shape[0]
    tiles_per_seq = seq // TM_IN
    n_chunks = TM_IN // VT_CHUNK
    row = lambda w_: pl.BlockSpec((TM_IN, w_), lambda i: (i, 0))
    tab = pl.BlockSpec((TM_IN, LANES), lambda i: (i % tiles_per_seq, 0))
    vt = pl.BlockSpec((n_chunks, KV_WIDTH, VT_CHUNK), lambda i: (i, 0, 0))
    return pl.pallas_call(
        _inproj_kernel,
        grid=(n // TM_IN,),
        in_specs=[row(D_MODEL), _resident((1, D_MODEL), lambda i: (0, 0)),
                  _resident((D_MODEL, N_INP), lambda i: (0, 0)), tab, tab, tab],
        out_specs=[row(POOL_WIDTH), row(NSA_WIDTH), row(2 * KV_WIDTH), row(KV_WIDTH), vt,
                   row(KV_WIDTH), vt, row(2 * D_MODEL), row(LANES)],
        out_shape=[
            jax.ShapeDtypeStruct((n, POOL_WIDTH), F32),
            jax.ShapeDtypeStruct((n, NSA_WIDTH), BF16),
            jax.ShapeDtypeStruct((n, 2 * KV_WIDTH), F32),
            jax.ShapeDtypeStruct((n, KV_WIDTH), BF16),
            jax.ShapeDtypeStruct((n // VT_CHUNK, KV_WIDTH, VT_CHUNK), BF16),
            jax.ShapeDtypeStruct((n, KV_WIDTH), BF16),
            jax.ShapeDtypeStruct((n // VT_CHUNK, KV_WIDTH, VT_CHUNK), BF16),
            jax.ShapeDtypeStruct((n, 2 * D_MODEL), F32),
            jax.ShapeDtypeStruct((n, LANES), F32),
        ],
        compiler_params=pltpu.CompilerParams(
            dimension_semantics=("arbitrary",), vmem_limit_bytes=VMEM_LIMIT),
        name="in_proj",
    )(x2, g, w, cos, slo, shi)


def _pool_kernel(u_ref, up_ref, wp_ref, sc_ref, o_ref):
    i = pl.program_id(1)
    cur = u_ref[0]
    prev = jnp.where(i > 0, up_ref[0], 0.0)
    t = i * TS_POOL + lax.broadcasted_iota(jnp.int32, (TS_POOL, POOL_GW), 0)
    for g, w in enumerate(POOL_WINDOWS):
        sl = slice(g * POOL_GW, (g + 1) * POOL_GW)
        cg = cur[:, sl]
        s = jnp.concatenate([prev[:, sl], cg], axis=0)
        sh = 1
        while sh < w:
            s = s + pltpu.roll(s, sh, 0)
            sh *= 2
        cnt = jnp.minimum(t + 1, w).astype(F32)
        d = s[POOL_HALO:] / cnt - cg
        y = _dot(d.astype(BF16), wp_ref[g]) * sc_ref[:, sl]
        o_ref[0, :, sl] = y.astype(BF16)


def _pool(u3, wp, sc):
    b, seq, _ = u3.shape
    halo_per_tile = TS_POOL // POOL_HALO
    return pl.pallas_call(
        _pool_kernel,
        grid=(b, seq // TS_POOL),
        in_specs=[
            pl.BlockSpec((1, TS_POOL, POOL_WIDTH), lambda bi, i: (bi, i, 0)),
            pl.BlockSpec((1, POOL_HALO, POOL_WIDTH),
                         lambda bi, i: (bi, jnp.maximum(i * halo_per_tile - 1, 0), 0)),
            pl.BlockSpec((len(POOL_WINDOWS), POOL_GW, POOL_GW), lambda bi, i: (0, 0, 0)),
            pl.BlockSpec((1, POOL_WIDTH), lambda bi, i: (0, 0)),
        ],
        out_specs=pl.BlockSpec((1, TS_POOL, POOL_WIDTH), lambda bi, i: (bi, i, 0)),
        out_shape=jax.ShapeDtypeStruct((b, seq, POOL_WIDTH), BF16),
        compiler_params=pltpu.CompilerParams(dimension_semantics=("arbitrary", "arbitrary")),
        name="pool_mixer",
    )(u3, u3, wp, sc)


def _compress_kernel(c_ref, pe_ref, w1_ref, w2_ref, cos_ref, slo_ref, shi_ref, o_ref, ot_ref):
    half = CMP_STRIDE * HEAD_DIM
    c = c_ref[...]
    a = _dot((c + pe_ref[:, 0:half]).astype(BF16), w1_ref[0:half, :])
    b = _dot((c + pe_ref[:, half:2 * half]).astype(BF16), w1_ref[half:2 * half, :])
    n_rows = c.shape[0]
    hid = a + pltpu.roll(b, n_rows - 1, 0)
    act = jax.nn.gelu(hid, approximate=True)
    out = _dot(act.astype(BF16), w2_ref[...])
    out = _rope(out, cos_ref[...], slo_ref[...], shi_ref[...])
    row = lax.broadcasted_iota(jnp.int32, out.shape, 0)
    out = jnp.where(row < n_rows - 1, out, 0.0)
    o_ref[...] = out.astype(BF16)
    ot_ref[...] = out.T.astype(BF16)


def _compress(c4, pe, w1, w2p, cos, slo, shi):
    b, n_kv, n_chunks, half = c4.shape
    sq = lambda *s: (None, None) + s
    return pl.pallas_call(
        _compress_kernel,
        grid=(b, n_kv),
        in_specs=[
            pl.BlockSpec((None, None, n_chunks, half), lambda bi, j: (bi, j, 0, 0)),
            pl.BlockSpec((None, 1, 2 * half), lambda bi, j: (j // N_KV_GROUPS, 0, 0)),
            pl.BlockSpec((None, 2 * half, CMP_HIDDEN), lambda bi, j: (j // N_KV_GROUPS, 0, 0)),
            pl.BlockSpec((None, CMP_HIDDEN, LANES), lambda bi, j: (j, 0, 0)),
            pl.BlockSpec((None, n_chunks, LANES), lambda bi, j: (j // N_KV_GROUPS, 0, 0)),
            pl.BlockSpec((None, n_chunks, LANES), lambda bi, j: (j // N_KV_GROUPS, 0, 0)),
            pl.BlockSpec((None, n_chunks, LANES), lambda bi, j: (j // N_KV_GROUPS, 0, 0)),
        ],
        out_specs=[
            pl.BlockSpec((None, None, n_chunks, LANES), lambda bi, j: (bi, j, 0, 0)),
            pl.BlockSpec((None, None, LANES, n_chunks), lambda bi, j: (bi, j, 0, 0)),
        ],
        out_shape=[
            jax.ShapeDtypeStruct((b, n_kv, n_chunks, LANES), BF16),
            jax.ShapeDtypeStruct((b, n_kv, LANES, n_chunks), BF16),
        ],
        compiler_params=pltpu.CompilerParams(dimension_semantics=("arbitrary", "arbitrary")),
        name="compress",
    )(c4, pe, w1, w2p, cos, slo, shi)


def _flash_step(kt, vt, bias_t, qs_ref, m_ref, l_ref, acc_ref):
    s = _dot_nt(kt, qs_ref[...]) + jnp.concatenate([bias_t] * HPG, axis=1)
    m_prev = m_ref[...]
    m_new = jnp.maximum(m_prev, jnp.max(s, axis=0, keepdims=True))
    alpha = jnp.exp(m_prev - m_new)
    p = jnp.exp(s - m_new)
    l_ref[...] = alpha * l_ref[...] + jnp.sum(p, axis=0, keepdims=True)
    acc_ref[...] = alpha * acc_ref[...] + _dot(vt, p.astype(BF16))
    m_ref[...] = m_new


def _attn_kernel(q_ref, gn_ref, kc_ref, vct_ref, ks_ref, vst_ref, kw_ref, vwt_ref, ovt_ref, et_ref,
                 o_ref, qs_ref, m_ref, l_ref, acc_ref, tot_ref, selt_ref):
    g = pl.program_id(1)
    i = pl.program_id(2)
    s0 = i * TQ
    n_cmp = kc_ref.shape[0]
    n_slc = ovt_ref.shape[0]

    lane = lax.broadcasted_iota(jnp.int32, (TQ, LANES), 1)
    in_group_half = (lane // HEAD_DIM) == g
    for hh in range(HPG):
        qp = q_ref[0, :, (hh // 2) * LANES:(hh // 2 + 1) * LANES].astype(F32)
        src = jnp.where(g == hh % 2, qp, pltpu.roll(qp, HEAD_DIM, 1))
        qs_ref[hh * TQ:(hh + 1) * TQ, :] = jnp.where(in_group_half, src, 0.0).astype(BF16)

    gn_t = gn_ref[0].T

    def gate_row(c):
        rows = []
        for hh in range(HPG):
            r0 = c * HPG + hh
            rows.append(jnp.where(g == 0, gn_t[r0:r0 + 1, :],
                                  gn_t[HEAD_DIM + r0:HEAD_DIM + r0 + 1, :]))
        return jnp.concatenate(rows, axis=1)

    s = _dot_nt(kc_ref[...], qs_ref[...])
    n_idx = lax.broadcasted_iota(jnp.int32, (n_cmp, M_ATT), 0)
    t_col = s0 + (lax.broadcasted_iota(jnp.int32, (n_cmp, M_ATT), 1) & (TQ - 1))
    valid = (n_idx * CMP_STRIDE + CMP_LEN - 1) <= t_col
    sm = jnp.where(valid, s, NEG_INF)
    mx = jnp.max(sm, axis=0, keepdims=True)
    p = jnp.where(valid, jnp.exp(sm - mx), 0.0)
    lsum = jnp.sum(p, axis=0, keepdims=True)
    pn = (p / jnp.where(lsum > 0.0, lsum, 1.0)).astype(BF16)
    tot_ref[...] = _dot(vct_ref[...], pn) * gate_row(0)

    imp = jnp.zeros((n_slc, TQ), F32)
    for hh in range(HPG):
        imp = imp + _dot(ovt_ref[...], pn[:, hh * TQ:(hh + 1) * TQ])

    jb = lax.broadcasted_iota(jnp.int32, (n_slc, TQ), 0)
    tq = s0 + lax.broadcasted_iota(jnp.int32, (n_slc, TQ), 1)
    causal = jb * SEL_BLOCK <= tq
    cur = tq // SEL_BLOCK
    forced = jnp.logical_and(causal, jnp.logical_or(jb == 0, jb >= cur - 1))
    score = jnp.where(forced, SEL_BONUS, jnp.where(causal, imp, NEG_INF))
    rank = jnp.zeros((n_slc, TQ), F32)
    for jp in range(n_slc):
        row = score[jp:jp + 1, :]
        gt = jnp.where(row > score, 1.0, 0.0)
        ge = jnp.where(row >= score, 1.0, 0.0)
        rank = rank + jnp.where(jb > jp, ge, gt)
    sel_t = jnp.where(causal, jnp.where(rank < float(N_SEL), 1.0, 0.0), 0.0)
    selt_ref[...] = sel_t.astype(BF16)

    def reset():
        m_ref[...] = jnp.full(m_ref.shape, NEG_INF, F32)
        l_ref[...] = jnp.zeros(l_ref.shape, F32)
        acc_ref[...] = jnp.zeros(acc_ref.shape, F32)

    def fold(c):
        tot_ref[...] = tot_ref[...] + acc_ref[...] * (gate_row(c) / l_ref[...])

    reset()
    sel_chunks = NK_SEL // VT_CHUNK

    def sel_body(j, carry):
        k0 = pl.multiple_of(j * NK_SEL, NK_SEL)
        kt = ks_ref[0, pl.ds(k0, NK_SEL), :]
        vt = jnp.concatenate([vst_ref[j * sel_chunks + c] for c in range(sel_chunks)], axis=1)
        picked = _dot(et_ref[j], selt_ref[...])
        kpos = k0 + lax.broadcasted_iota(jnp.int32, (NK_SEL, TQ), 0)
        tqk = s0 + lax.broadcasted_iota(jnp.int32, (NK_SEL, TQ), 1)
        bias = jnp.where(kpos <= tqk, jnp.where(picked > 0.5, 0.0, NEG_INF), NEG_INF)
        _flash_step(kt, vt, bias, qs_ref, m_ref, l_ref, acc_ref)
        return carry

    lax.fori_loop(0, (s0 + TQ - 1) // NK_SEL + 1, sel_body, 0)
    fold(1)

    reset()
    win_chunks = NK_WIN // VT_CHUNK

    def win_body(j, carry):
        k0 = pl.multiple_of(j * NK_WIN, NK_WIN)
        kt = kw_ref[0, pl.ds(k0, NK_WIN), :]
        vt = jnp.concatenate([vwt_ref[j * win_chunks + c] for c in range(win_chunks)], axis=1)
        kpos = k0 + lax.broadcasted_iota(jnp.int32, (NK_WIN, TQ), 0)
        tqk = s0 + lax.broadcasted_iota(jnp.int32, (NK_WIN, TQ), 1)
        bias = jnp.where(kpos <= tqk, jnp.where(kpos > tqk - WINDOW, 0.0, NEG_INF), NEG_INF)
        _flash_step(kt, vt, bias, qs_ref, m_ref, l_ref, acc_ref)
        return carry

    first_win = jnp.maximum(s0 - (WINDOW - 1), 0) // NK_WIN
    lax.fori_loop(first_win, (s0 + TQ - 1) // NK_WIN + 1, win_body, 0)
    fold(2)

    for pair in range(HPG // 2):
        halves = []
        for par in range(2):
            hh = 2 * pair + par
            th = tot_ref[:, hh * TQ:(hh + 1) * TQ].T
            halves.append(jnp.where(g == par, th, pltpu.roll(th, HEAD_DIM, 1)))
        o_ref[0, :, pair * LANES:(pair + 1) * LANES] = jnp.where(
            lane < HEAD_DIM, halves[0], halves[1]).astype(BF16)


def _attention(q3, gn3, kcmp, vcmp_t, ks3, vst, kw3, vwt, ovt, et):
    b, seq, _ = q3.shape
    n_cmp = kcmp.shape[2]
    n_slc = seq // SEL_BLOCK
    chunks_per_seq = seq // VT_CHUNK
    whole = lambda bi, g, i: (bi, 0, 0)
    return pl.pallas_call(
        _attn_kernel,
        grid=(b, N_KV_GROUPS, seq // TQ),
        in_specs=[
            pl.BlockSpec((1, TQ, HPG * HEAD_DIM), lambda bi, g, i: (bi, i, g)),
            pl.BlockSpec((1, TQ, LANES), lambda bi, g, i: (bi, i, 0)),
            pl.BlockSpec((None, None, n_cmp, LANES), lambda bi, g, i: (bi, g, 0, 0)),
            pl.BlockSpec((None, None, LANES, n_cmp), lambda bi, g, i: (bi, N_KV_GROUPS + g, 0, 0)),
            pl.BlockSpec((1, seq, KV_WIDTH), whole),
            pl.BlockSpec((chunks_per_seq, KV_WIDTH, VT_CHUNK), whole),
            pl.BlockSpec((1, seq, KV_WIDTH), whole),
            pl.BlockSpec((chunks_per_seq, KV_WIDTH, VT_CHUNK), whole),
            pl.BlockSpec((n_slc, n_cmp), lambda bi, g, i: (0, 0)),
            pl.BlockSpec((seq // NK_SEL, NK_SEL, n_slc), lambda bi, g, i: (0, 0, 0)),
        ],
        out_specs=pl.BlockSpec((1, TQ, HPG * HEAD_DIM), lambda bi, g, i: (bi, i, g)),
        out_shape=jax.ShapeDtypeStruct((b, seq, NSA_WIDTH), BF16),
        scratch_shapes=[
            pltpu.VMEM((M_ATT, LANES), BF16),
            pltpu.VMEM((1, M_ATT), F32),
            pltpu.VMEM((1, M_ATT), F32),
            pltpu.VMEM((LANES, M_ATT), F32),
            pltpu.VMEM((LANES, M_ATT), F32),
            pltpu.VMEM((n_slc, TQ), BF16),
        ],
        compiler_params=pltpu.CompilerParams(
            dimension_semantics=("arbitrary", "arbitrary", "arbitrary"),
            vmem_limit_bytes=VMEM_LIMIT),
        name="nsa_attention",
    )(q3, gn3, kcmp, vcmp_t, ks3, vst, kw3, vwt, ovt, et)


def _mlp_kernel(x_ref, yp_ref, yn_ref, gm_ref, wpp_ref, wpn_ref, wo_ref, nm_ref, w1_ref, w2_ref,
                nf_ref, o_ref, *, final):
    p1 = _dot(yp_ref[...], wpp_ref[...])
    p2 = _dot(yn_ref[...], wpn_ref[...])
    ga = jax.nn.sigmoid(gm_ref[:, 0:D_MODEL])
    gb = jax.nn.sigmoid(gm_ref[:, D_MODEL:2 * D_MODEL])
    merged = ga * p1 + gb * p2
    x = x_ref[...] + _dot(merged.astype(BF16), wo_ref[...])
    h = _rms(x, nm_ref[...]).astype(BF16)
    acc = jnp.zeros((TM_MLP, D_MODEL), F32)
    for c in range(D_FF // FF_CHUNK):
        sl = slice(c * FF_CHUNK, (c + 1) * FF_CHUNK)
        a = jnp.square(jnp.maximum(_dot(h, w1_ref[:, sl]), 0.0)).astype(BF16)
        acc = acc + _dot(a, w2_ref[sl, :])
    x = x + acc
    if final:
        x = _rms(x, nf_ref[...])
    o_ref[...] = x


def _merge_mlp(x2, yp, yn, gm, wpp, wpn, wo, nm, w1, w2, nf, final):
    n = x2.shape[0]
    row = lambda w_: pl.BlockSpec((TM_MLP, w_), lambda i: (i, 0))
    res = lambda a: _resident(a.shape, lambda i: (0,) * a.ndim)
    return pl.pallas_call(
        functools.partial(_mlp_kernel, final=final),
        grid=(n // TM_MLP,),
        in_specs=[row(D_MODEL), row(POOL_WIDTH), row(NSA_WIDTH), row(2 * D_MODEL),
                  res(wpp), res(wpn), res(wo), res(nm), res(w1), res(w2), res(nf)],
        out_specs=row(D_MODEL),
        out_shape=jax.ShapeDtypeStruct((n, D_MODEL), F32),
        compiler_params=pltpu.CompilerParams(
            dimension_semantics=("arbitrary",), vmem_limit_bytes=VMEM_LIMIT),
        name="merge_mlp",
    )(x2, yp, yn, gm, wpp, wpn, wo, nm, w1, w2, nf)


def _rope_tables(pos):
    inv = ROPE_THETA ** (-jnp.arange(0, HEAD_DIM, 2, dtype=F32) / HEAD_DIM)
    ang = pos.astype(F32)[:, None] * inv[None, :]
    ang = jnp.concatenate([ang, ang, ang, ang], axis=-1)
    first_half = (jnp.arange(LANES) % HEAD_DIM) < HEAD_DIM // 2
    cos, sin = jnp.cos(ang), jnp.sin(ang)
    return cos, jnp.where(first_half, -sin, 0.0), jnp.where(first_half, 0.0, sin)


def _permute_w_in(w_in):
    o_q = POOL_WIDTH
    o_kv = o_q + NSA_WIDTH
    o_gn = o_kv + 6 * KV_WIDTH
    o_gm = o_gn + N_GATE
    depth = w_in.shape[0]
    gn = w_in[:, :, o_gn:o_gm].reshape(depth, D_MODEL, N_KV_GROUPS, HPG, 3)
    gn = gn.transpose(0, 1, 2, 4, 3).reshape(depth, D_MODEL, N_KV_GROUPS, 3 * HPG)
    gn = jnp.pad(gn, ((0, 0), (0, 0), (0, 0), (0, LANES // N_KV_GROUPS - 3 * HPG)))
    gn = gn.reshape(depth, D_MODEL, LANES)
    return jnp.concatenate(
        [w_in[:, :, 0:o_gn], w_in[:, :, o_gm:], gn], axis=-1).astype(BF16)


def kernel(x, norm_mix, w_in, w_pool, pool_scale, pe_k, pe_v, w_ck1, w_ck2, w_cv1, w_cv2,
           w_proj_pool, w_proj_nsa, w_out, norm_mlp, w_ff1, w_ff2, norm_final):
    b, seq, d = x.shape
    depth = w_in.shape[0]
    n = b * seq
    n_chunks = seq // CMP_STRIDE
    n_slc = seq // SEL_BLOCK

    w_in_p = _permute_w_in(w_in)
    w_pool_b = w_pool.astype(BF16)
    pe = jnp.stack([pe_k, pe_v], axis=1).reshape(depth, 2, 1, CMP_LEN * HEAD_DIM)
    w_c1 = jnp.stack([w_ck1, w_cv1], axis=1).astype(BF16)
    zeros = jnp.zeros_like(w_ck2)
    w_c2 = jnp.stack([
        jnp.concatenate([w_ck2, zeros], axis=-1), jnp.concatenate([zeros, w_ck2], axis=-1),
        jnp.concatenate([w_cv2, zeros], axis=-1), jnp.concatenate([zeros, w_cv2], axis=-1),
    ], axis=1).astype(BF16)
    wpp, wpn, wo = w_proj_pool.astype(BF16), w_proj_nsa.astype(BF16), w_out.astype(BF16)
    w1, w2 = w_ff1.astype(BF16), w_ff2.astype(BF16)

    cos, slo, shi = _rope_tables(jnp.arange(seq))
    ccos, cslo, cshi = _rope_tables(jnp.arange(n_chunks) * CMP_STRIDE + CMP_LEN - 1)
    ident = (jnp.ones_like(ccos), jnp.zeros_like(cslo), jnp.zeros_like(cshi))
    cmp_tabs = [jnp.stack([t, i_], axis=0) for t, i_ in zip((ccos, cslo, cshi), ident)]
    cmp_start = jnp.arange(n_chunks) * CMP_STRIDE
    slc_start = jnp.arange(n_slc) * SEL_BLOCK
    ovt = ((cmp_start[None, :] <= slc_start[:, None] + SEL_BLOCK - 1)
           & (cmp_start[None, :] + CMP_LEN - 1 >= slc_start[:, None])).astype(BF16)
    et = (jnp.arange(seq)[:, None] // SEL_BLOCK == jnp.arange(n_slc)[None, :]).astype(BF16)
    et = et.reshape(seq // NK_SEL, NK_SEL, n_slc)

    x2 = x.reshape(n, d)
    for l in range(depth):
        u, q, kcvc, ks, vst, kw, vwt, gm, gn = _inproj(
            x2, norm_mix[l][None, :], w_in_p[l], cos, slo, shi, seq)
        y_pool = _pool(u.reshape(b, seq, POOL_WIDTH), w_pool_b[l], pool_scale[l][None, :])
        c4 = kcvc.reshape(b, seq, 2 * N_KV_GROUPS, HEAD_DIM).transpose(0, 2, 1, 3)
        c4 = c4.reshape(b, 2 * N_KV_GROUPS, n_chunks, CMP_STRIDE * HEAD_DIM)
        cmp_n, cmp_t = _compress(c4, pe[l], w_c1[l], w_c2[l], *cmp_tabs)
        y_nsa = _attention(
            q.reshape(b, seq, NSA_WIDTH), gn.reshape(b, seq, LANES), cmp_n, cmp_t,
            ks.reshape(b, seq, KV_WIDTH), vst, kw.reshape(b, seq, KV_WIDTH), vwt, ovt, et)
        x2 = _merge_mlp(
            x2, y_pool.reshape(n, POOL_WIDTH), y_nsa.reshape(n, NSA_WIDTH), gm,
            wpp[l], wpn[l], wo[l], norm_mlp[l][None, :], w1[l], w2[l], norm_final[None, :],
            final=(l == depth - 1))
    return x2.reshape(b, seq, d)
```

```python
import functools

import jax
import jax.numpy as jnp
import numpy as np
from jax import lax
from jax.experimental import pallas as pl
from jax.experimental.pallas import tpu as pltpu

F32 = jnp.float32
BF16 = jnp.bfloat16

D_MODEL = 1024
POOL_WINDOWS = (2, 4, 8, 16)
POOL_WIDTH = 512
POOL_GW = 128
N_HEADS = 16
HEAD_DIM = 64
N_KV_GROUPS = 2
HPG = 8
NSA_WIDTH = 1024
KV_WIDTH = 128
CMP_LEN = 32
CMP_STRIDE = 16
CMP_HIDDEN = 256
SEL_BLOCK = 64
N_SEL = 16
WINDOW = 512
SEL_BONUS = 1e4
NEG_INF = -1e30
ROPE_THETA = 10000.0
D_FF = 4096
RMS_EPS = 1e-6
N_GATE = 3 * N_HEADS

LANES = 128
VMEM_LIMIT = 56 * 1024 * 1024

C_U = 0
C_Q = C_U + POOL_WIDTH
C_KV = C_Q + NSA_WIDTH
C_GM = C_KV + 6 * KV_WIDTH
C_GN = C_GM + 2 * D_MODEL
N_INP = C_GN + LANES

TM_IN = 512
TS_POOL = 512
POOL_HALO = 16
TM_MLP = 512
FF_CHUNK = 1024
TQ = 128
M_ATT = HPG * TQ
CW = 2 * TQ
NK_SEL = 256
WIN_KEYS = WINDOW + TQ
VT_CHUNK = 128
V_ROWS = HEAD_DIM + 16


def _dot(a, b):
    return jnp.dot(a, b, preferred_element_type=F32)


def _rms(x, g):
    return x * lax.rsqrt(jnp.mean(x * x, axis=-1, keepdims=True) + RMS_EPS) * g


def _rope(t, cos, sin_lo, sin_hi):
    return t * cos + pltpu.roll(t, LANES - 32, 1) * sin_lo + pltpu.roll(t, 32, 1) * sin_hi


def _resident(shape, index_map):
    return pl.BlockSpec(shape, index_map, pipeline_mode=pl.Buffered(1))


def _value_rows(v_t):
    tail_row = lax.broadcasted_iota(jnp.int32, (V_ROWS - HEAD_DIM, v_t.shape[1]), 0)
    return jnp.concatenate([v_t, jnp.where(tail_row == 0, 1.0, 0.0)], axis=0)


def _inproj_kernel(x_ref, g_ref, w_ref, cos_ref, slo_ref, shi_ref,
                   u_ref, q_ref, kcvc_ref, ks_ref, vst_ref, kw_ref, vwt_ref, gm_ref, gn_ref,
                   *, tiles_per_seq):
    h = _rms(x_ref[...], g_ref[...]).astype(BF16)
    cos, slo, shi = cos_ref[...], slo_ref[...], shi_ref[...]
    lane = lax.broadcasted_iota(jnp.int32, (TM_IN, LANES), 1)
    low = lane < HEAD_DIM
    pos = (pl.program_id(0) % tiles_per_seq) * TM_IN + lax.broadcasted_iota(
        jnp.int32, (TM_IN, LANES), 0)
    block_onehot = jnp.where(lane - HEAD_DIM == pos // SEL_BLOCK, 1.0, 0.0)

    u_ref[...] = _dot(h, w_ref[:, C_U:C_Q])
    q = _dot(h, w_ref[:, C_Q:C_KV])
    for k in range(NSA_WIDTH // LANES):
        qt = _rope(q[:, k * LANES:(k + 1) * LANES], cos, slo, shi) * (HEAD_DIM ** -0.5)
        q_ref[:, (2 * k) * LANES:(2 * k + 1) * LANES] = jnp.where(low, qt, 0.0).astype(BF16)
        q_ref[:, (2 * k + 1) * LANES:(2 * k + 2) * LANES] = jnp.where(
            low, pltpu.roll(qt, HEAD_DIM, 1), 0.0).astype(BF16)
    kv = _dot(h, w_ref[:, C_KV:C_GM])
    kcvc_ref[...] = kv[:, 0:2 * KV_WIDTH]
    ks = _rope(kv[:, 2 * KV_WIDTH:3 * KV_WIDTH], cos, slo, shi)
    kw = _rope(kv[:, 4 * KV_WIDTH:5 * KV_WIDTH], cos, slo, shi)
    vs_t = kv[:, 3 * KV_WIDTH:4 * KV_WIDTH].T
    vw_t = kv[:, 5 * KV_WIDTH:6 * KV_WIDTH].T
    for g in range(N_KV_GROUPS):
        ks_g = ks if g == 0 else pltpu.roll(ks, HEAD_DIM, 1)
        kw_g = kw if g == 0 else pltpu.roll(kw, HEAD_DIM, 1)
        ks_ref[0, g] = jnp.where(low, ks_g, block_onehot).astype(BF16)
        kw_ref[0, g] = jnp.where(low, kw_g, 0.0).astype(BF16)
        vs_g = _value_rows(vs_t[g * HEAD_DIM:(g + 1) * HEAD_DIM, :]).astype(BF16)
        vw_g = _value_rows(vw_t[g * HEAD_DIM:(g + 1) * HEAD_DIM, :]).astype(BF16)
        for c in range(TM_IN // VT_CHUNK):
            sl = slice(c * VT_CHUNK, (c + 1) * VT_CHUNK)
            vst_ref[0, g, c] = vs_g[:, sl]
            vwt_ref[0, g, c] = vw_g[:, sl]
    gm_ref[...] = _dot(h, w_ref[:, C_GM:C_GN])
    gn_ref[...] = jax.nn.sigmoid(_dot(h, w_ref[:, C_GN:N_INP]))


def _inproj(x2, g, w, cos, slo, shi, b, seq):
    n = x2.shape[0]
    tiles_per_seq = seq // TM_IN
    n_chunks = TM_IN // VT_CHUNK
    row = lambda w_: pl.BlockSpec((TM_IN, w_), lambda i: (i, 0))
    tab = pl.BlockSpec((TM_IN, LANES), lambda i: (i % tiles_per_seq, 0))
    kg = pl.BlockSpec((1, N_KV_GROUPS, TM_IN, LANES),
                      lambda i: (i // tiles_per_seq, 0, i % tiles_per_seq, 0))
    vt = pl.BlockSpec((1, N_KV_GROUPS, n_chunks, V_ROWS, VT_CHUNK),
                      lambda i: (i // tiles_per_seq, 0, i % tiles_per_seq, 0, 0))
    k_shape = jax.ShapeDtypeStruct((b, N_KV_GROUPS, seq, LANES), BF16)
    vt_shape = jax.ShapeDtypeStruct((b, N_KV_GROUPS, seq // VT_CHUNK, V_ROWS, VT_CHUNK), BF16)
    return pl.pallas_call(
        functools.partial(_inproj_kernel, tiles_per_seq=tiles_per_seq),
        grid=(n // TM_IN,),
        in_specs=[row(D_MODEL), _resident((1, D_MODEL), lambda i: (0, 0)),
                  _resident((D_MODEL, N_INP), lambda i: (0, 0)), tab, tab, tab],
        out_specs=[row(POOL_WIDTH), row(2 * NSA_WIDTH), row(2 * KV_WIDTH), kg, vt,
                   kg, vt, row(2 * D_MODEL), row(LANES)],
        out_shape=[
            jax.ShapeDtypeStruct((n, POOL_WIDTH), F32),
            jax.ShapeDtypeStruct((n, 2 * NSA_WIDTH), BF16),
            jax.ShapeDtypeStruct((n, 2 * KV_WIDTH), F32),
            k_shape, vt_shape, k_shape, vt_shape,
            jax.ShapeDtypeStruct((n, 2 * D_MODEL), F32),
            jax.ShapeDtypeStruct((n, LANES), F32),
        ],
        compiler_params=pltpu.CompilerParams(
            dimension_semantics=("arbitrary",), vmem_limit_bytes=VMEM_LIMIT),
        name="in_proj",
    )(x2, g, w, cos, slo, shi)


def _pool_kernel(u_ref, up_ref, wp_ref, sc_ref, o_ref):
    i = pl.program_id(1)
    cur = u_ref[0]
    prev = jnp.where(i > 0, up_ref[0], 0.0)
    t = i * TS_POOL + lax.broadcasted_iota(jnp.int32, (TS_POOL, POOL_GW), 0)
    for g, w in enumerate(POOL_WINDOWS):
        sl = slice(g * POOL_GW, (g + 1) * POOL_GW)
        cg = cur[:, sl]
        s = jnp.concatenate([prev[:, sl], cg], axis=0)
        sh = 1
        while sh < w:
            s = s + pltpu.roll(s, sh, 0)
            sh *= 2
        cnt = jnp.minimum(t + 1, w).astype(F32)
        d = s[POOL_HALO:] / cnt - cg
        y = _dot(d.astype(BF16), wp_ref[g]) * sc_ref[:, sl]
        o_ref[0, :, sl] = y.astype(BF16)


def _pool(u3, wp, sc):
    b, seq, _ = u3.shape
    halo_per_tile = TS_POOL // POOL_HALO
    return pl.pallas_call(
        _pool_kernel,
        grid=(b, seq // TS_POOL),
        in_specs=[
            pl.BlockSpec((1, TS_POOL, POOL_WIDTH), lambda bi, i: (bi, i, 0)),
            pl.BlockSpec((1, POOL_HALO, POOL_WIDTH),
                         lambda bi, i: (bi, jnp.maximum(i * halo_per_tile - 1, 0), 0)),
            pl.BlockSpec((len(POOL_WINDOWS), POOL_GW, POOL_GW), lambda bi, i: (0, 0, 0)),
            pl.BlockSpec((1, POOL_WIDTH), lambda bi, i: (0, 0)),
        ],
        out_specs=pl.BlockSpec((1, TS_POOL, POOL_WIDTH), lambda bi, i: (bi, i, 0)),
        out_shape=jax.ShapeDtypeStruct((b, seq, POOL_WIDTH), BF16),
        compiler_params=pltpu.CompilerParams(dimension_semantics=("arbitrary", "arbitrary")),
        name="pool_mixer",
    )(u3, u3, wp, sc)


def _compress_kernel(c_ref, pe_ref, w1_ref, w2_ref, cos_ref, slo_ref, shi_ref, o_ref, ot_ref):
    half = CMP_STRIDE * HEAD_DIM
    c = c_ref[...]
    a = _dot((c + pe_ref[:, 0:half]).astype(BF16), w1_ref[0:half, :])
    b = _dot((c + pe_ref[:, half:2 * half]).astype(BF16), w1_ref[half:2 * half, :])
    n_rows = c.shape[0]
    hid = a + pltpu.roll(b, n_rows - 1, 0)
    act = jax.nn.gelu(hid, approximate=True)
    out = _dot(act.astype(BF16), w2_ref[...])
    out = _rope(out, cos_ref[...], slo_ref[...], shi_ref[...])
    row = lax.broadcasted_iota(jnp.int32, out.shape, 0)
    out = jnp.where(row < n_rows - 1, out, 0.0)
    o_ref[...] = out.astype(BF16)
    ot_ref[...] = out.T[0:HEAD_DIM, :].astype(BF16)


def _compress(c4, pe, w1, w2p, cos, slo, shi):
    b, n_kv, n_chunks, half = c4.shape
    return pl.pallas_call(
        _compress_kernel,
        grid=(b, n_kv),
        in_specs=[
            pl.BlockSpec((None, None, n_chunks, half), lambda bi, j: (bi, j, 0, 0)),
            pl.BlockSpec((None, 1, 2 * half), lambda bi, j: (j // N_KV_GROUPS, 0, 0)),
            pl.BlockSpec((None, 2 * half, CMP_HIDDEN), lambda bi, j: (j // N_KV_GROUPS, 0, 0)),
            pl.BlockSpec((None, CMP_HIDDEN, LANES), lambda bi, j: (j, 0, 0)),
            pl.BlockSpec((None, n_chunks, LANES), lambda bi, j: (j // N_KV_GROUPS, 0, 0)),
            pl.BlockSpec((None, n_chunks, LANES), lambda bi, j: (j // N_KV_GROUPS, 0, 0)),
            pl.BlockSpec((None, n_chunks, LANES), lambda bi, j: (j // N_KV_GROUPS, 0, 0)),
        ],
        out_specs=[
            pl.BlockSpec((None, None, n_chunks, LANES), lambda bi, j: (bi, j, 0, 0)),
            pl.BlockSpec((None, None, HEAD_DIM, n_chunks), lambda bi, j: (bi, j, 0, 0)),
        ],
        out_shape=[
            jax.ShapeDtypeStruct((b, n_kv, n_chunks, LANES), BF16),
            jax.ShapeDtypeStruct((b, n_kv, HEAD_DIM, n_chunks), BF16),
        ],
        compiler_params=pltpu.CompilerParams(dimension_semantics=("arbitrary", "arbitrary")),
        name="compress",
    )(c4, pe, w1, w2p, cos, slo, shi)


def _rank_select(score, jb):
    n_slc = score.shape[0]
    sub = 8
    ranks = []
    for v in range(n_slc // sub):
        blk = score[v * sub:(v + 1) * sub, :]
        jb_v = jb[v * sub:(v + 1) * sub, :]
        r = jnp.zeros(blk.shape, F32)
        for jp in range(n_slc):
            row = score[jp:jp + 1, :]
            ge = jnp.where(row >= blk, 1.0, 0.0)
            gt = jnp.where(row > blk, 1.0, 0.0)
            if jp < v * sub:
                r = r + ge
            elif jp >= (v + 1) * sub:
                r = r + gt
            else:
                r = r + jnp.where(jb_v > jp, ge, gt)
        ranks.append(r)
    return jnp.concatenate(ranks, axis=0) < float(N_SEL)


def _attn_kernel(q_ref, gn_ref, kc_ref, vct_ref, ks_ref, vst_ref, kw_ref, vwt_ref, ovt_ref,
                 o_ref, qs_ref, qsel_ref, s_ref, p_ref, al_ref, mw_ref, ms_ref, accw_ref, accs_ref,
                 tot_ref, gt_ref):
    g = pl.program_id(1)
    i = pl.program_id(2)
    s0 = i * TQ
    n_cmp = kc_ref.shape[0]
    n_slc = ovt_ref.shape[0]
    n_ch = M_ATT // CW
    chunk = lambda ch: slice(ch * CW, (ch + 1) * CW)

    for hh in range(HPG):
        qs_ref[:, hh * TQ:(hh + 1) * TQ] = (
            q_ref[0, :, hh * LANES:(hh + 1) * LANES].astype(F32).T.astype(BF16))

    gt_ref[...] = gn_ref[0].T
    g_base = g * HEAD_DIM

    def gate_rows(c, ch):
        return jnp.concatenate(
            [gt_ref[pl.ds(g_base + c * HPG + 2 * ch + par, 1), :] for par in range(2)], axis=1)

    def two_heads(a):
        return jnp.concatenate([a, a], axis=1)

    def key_tile(ref, k0, nk):
        return ref[pl.ds(pl.multiple_of(k0, VT_CHUNK), nk), :]

    def value_tile(ref, k0, nk):
        c0 = k0 // VT_CHUNK
        return jnp.concatenate([ref[c0 + c] for c in range(nk // VT_CHUNK)], axis=1)

    imp_parts = []

    def run_step(soft=None, score=None, value=None, cmp_soft=None, cmp_value=False):
        for ch in range(n_ch):
            cs = chunk(ch)
            if soft is not None:
                nk, m_ref, bias, first = soft
                s = s_ref[0:nk, cs]
                if bias is not None:
                    s = s + bias
                mx = jnp.max(s, axis=0, keepdims=True)
                if first:
                    m_new, al_new = mx, None
                else:
                    m_prev = m_ref[:, cs]
                    m_new = jnp.maximum(m_prev, mx)
                    al_new = jnp.exp(m_prev - m_new)
                p_new = jnp.exp(s - m_new).astype(BF16)
            if cmp_soft is not None:
                s = s_ref[0:n_cmp, cs] + cmp_soft
                mx = jnp.max(s, axis=0, keepdims=True)
                p = jnp.where(s > 0.5 * NEG_INF, jnp.exp(s - mx), 0.0)
                lsum = jnp.sum(p, axis=0, keepdims=True)
                pn_new = (p * (1.0 / jnp.where(lsum > 0.0, lsum, 1.0))).astype(BF16)
            if score is not None:
                kt, q_t_ref = score
                s_ref[0:kt.shape[0], cs] = _dot(kt, q_t_ref[:, cs])
            if value is not None:
                vt, acc_ref, vfirst = value
                pv = _dot(vt, p_ref[0:vt.shape[1], cs])
                acc_ref[:, cs] = pv if vfirst else al_ref[:, cs] * acc_ref[:, cs] + pv
            if cmp_value:
                pn = p_ref[0:n_cmp, cs]
                tot_ref[:, cs] = _dot(vct_ref[...], pn) * gate_rows(0, ch)
                for par in range(2):
                    imp_parts.append(_dot(ovt_ref[...], pn[:, par * TQ:(par + 1) * TQ]))
            if soft is not None:
                p_ref[0:nk, cs] = p_new
                m_ref[:, cs] = m_new
                if al_new is not None:
                    al_ref[:, cs] = al_new
            if cmp_soft is not None:
                p_ref[0:n_cmp, cs] = pn_new

    n_idx = lax.broadcasted_iota(jnp.int32, (n_cmp, TQ), 0)
    t_cmp = s0 + lax.broadcasted_iota(jnp.int32, (n_cmp, TQ), 1)
    cmp_bias = two_heads(jnp.where(n_idx * CMP_STRIDE + CMP_LEN - 1 <= t_cmp, 0.0, NEG_INF))

    w0 = jnp.maximum(s0 - WINDOW, 0)
    win_tiles = [(0, NK_SEL), (NK_SEL, NK_SEL), (2 * NK_SEL, WIN_KEYS - 2 * NK_SEL)]

    def win_bias(off, nk):
        kpos = w0 + off + lax.broadcasted_iota(jnp.int32, (nk, TQ), 0)
        t_w = s0 + lax.broadcasted_iota(jnp.int32, (nk, TQ), 1)
        return two_heads(
            jnp.where(kpos <= t_w, jnp.where(kpos > t_w - WINDOW, 0.0, NEG_INF), NEG_INF))

    def win_keys(w):
        return key_tile(kw_ref, w0 + win_tiles[w][0], win_tiles[w][1])

    def win_values(w):
        return value_tile(vwt_ref, w0 + win_tiles[w][0], win_tiles[w][1])

    def win_soft(w, first):
        return (win_tiles[w][1], mw_ref, win_bias(*win_tiles[w]), first)

    run_step(score=(kc_ref[...], qs_ref))
    run_step(cmp_soft=cmp_bias, score=(win_keys(0), qs_ref))
    run_step(soft=win_soft(0, True), score=(win_keys(1), qs_ref), cmp_value=True)
    run_step(soft=win_soft(1, False), score=(win_keys(2), qs_ref),
             value=(win_values(0), accw_ref, True))

    imp = functools.reduce(lambda a, b: a + b, imp_parts)
    jb = lax.broadcasted_iota(jnp.int32, (n_slc, TQ), 0)
    tq = s0 + lax.broadcasted_iota(jnp.int32, (n_slc, TQ), 1)
    causal = jb * SEL_BLOCK <= tq
    near = jnp.logical_or(jb == 0, jb >= tq // SEL_BLOCK - 1)
    score = jnp.where(causal, jnp.where(near, SEL_BONUS, imp), NEG_INF)
    picked = _rank_select(score, jb)
    sel_bias = jnp.where(causal, jnp.where(picked, 0.0, NEG_INF), NEG_INF)
    parts = [jnp.zeros((HEAD_DIM, TQ), F32), sel_bias]
    if n_slc < HEAD_DIM:
        parts.append(jnp.zeros((HEAD_DIM - n_slc, TQ), F32))
    sel_rows = jnp.concatenate(parts, axis=0).astype(BF16)
    for hh in range(HPG):
        hs = slice(hh * TQ, (hh + 1) * TQ)
        qsel_ref[:, hs] = qs_ref[:, hs] + sel_rows

    ms_ref[...] = jnp.full(ms_ref.shape, NEG_INF, F32)
    accs_ref[...] = jnp.zeros(accs_ref.shape, F32)
    j_diag = s0 // NK_SEL

    def sel_values(j, live):
        vt = value_tile(vst_ref, jnp.maximum(j, 0) * NK_SEL, NK_SEL)
        return jnp.where(live, vt, jnp.zeros_like(vt))

    run_step(soft=win_soft(2, False), score=(key_tile(ks_ref, 0, NK_SEL), qsel_ref),
             value=(win_values(1), accw_ref, False))
    run_step(value=(win_values(2), accw_ref, False))

    def sel_body(j, carry):
        run_step(soft=(NK_SEL, ms_ref, None, False),
                 score=(key_tile(ks_ref, j * NK_SEL, NK_SEL), qsel_ref),
                 value=(sel_values(j - 2, j >= 2), accs_ref, False))
        return carry

    lax.fori_loop(1, j_diag + 1, sel_body, 0)
    kpos = j_diag * NK_SEL + lax.broadcasted_iota(jnp.int32, (NK_SEL, TQ), 0)
    t_sel = s0 + lax.broadcasted_iota(jnp.int32, (NK_SEL, TQ), 1)
    diag_bias = two_heads(jnp.where(kpos <= t_sel, 0.0, NEG_INF))
    run_step(soft=(NK_SEL, ms_ref, diag_bias, False),
             value=(sel_values(j_diag - 1, j_diag >= 1), accs_ref, False))
    run_step(value=(sel_values(j_diag, True), accs_ref, False))

    for ch in range(n_ch):
        cs = chunk(ch)
        coef_w = gate_rows(2, ch) / accw_ref[HEAD_DIM:HEAD_DIM + 1, cs]
        coef_s = gate_rows(1, ch) / accs_ref[HEAD_DIM:HEAD_DIM + 1, cs]
        tot = (tot_ref[:, cs] + accw_ref[0:HEAD_DIM, cs] * coef_w
               + accs_ref[0:HEAD_DIM, cs] * coef_s)
        both = jnp.concatenate([tot[:, 0:TQ], tot[:, TQ:2 * TQ]], axis=0)
        o_ref[0, :, ch * LANES:(ch + 1) * LANES] = both.T.astype(BF16)


def _attention(q3, gn3, kcmp, vcmp_t, ks4, vst, kw4, vwt, ovt):
    b, seq, _ = q3.shape
    n_cmp = kcmp.shape[2]
    n_slc = seq // SEL_BLOCK
    chunks_per_seq = seq // VT_CHUNK
    per_group = lambda bi, g, i: (bi, g, 0, 0)
    return pl.pallas_call(
        _attn_kernel,
        grid=(b, N_KV_GROUPS, seq // TQ),
        in_specs=[
            pl.BlockSpec((1, TQ, HPG * LANES), lambda bi, g, i: (bi, i, g)),
            pl.BlockSpec((1, TQ, LANES), lambda bi, g, i: (bi, i, 0)),
            pl.BlockSpec((None, None, n_cmp, LANES), per_group),
            pl.BlockSpec((None, None, HEAD_DIM, n_cmp),
                         lambda bi, g, i: (bi, N_KV_GROUPS + g, 0, 0)),
            pl.BlockSpec((None, None, seq, LANES), per_group),
            pl.BlockSpec((None, None, chunks_per_seq, V_ROWS, VT_CHUNK),
                         lambda bi, g, i: (bi, g, 0, 0, 0)),
            pl.BlockSpec((None, None, seq, LANES), per_group),
            pl.BlockSpec((None, None, chunks_per_seq, V_ROWS, VT_CHUNK),
                         lambda bi, g, i: (bi, g, 0, 0, 0)),
            pl.BlockSpec((n_slc, n_cmp), lambda bi, g, i: (0, 0)),
        ],
        out_specs=pl.BlockSpec((1, TQ, HPG * HEAD_DIM), lambda bi, g, i: (bi, i, g)),
        out_shape=jax.ShapeDtypeStruct((b, seq, NSA_WIDTH), BF16),
        scratch_shapes=[
            pltpu.VMEM((LANES, M_ATT), BF16),
            pltpu.VMEM((LANES, M_ATT), BF16),
            pltpu.VMEM((NK_SEL, M_ATT), F32),
            pltpu.VMEM((NK_SEL, M_ATT), BF16),
            pltpu.VMEM((1, M_ATT), F32),
            pltpu.VMEM((1, M_ATT), F32),
            pltpu.VMEM((1, M_ATT), F32),
            pltpu.VMEM((V_ROWS, M_ATT), F32),
            pltpu.VMEM((V_ROWS, M_ATT), F32),
            pltpu.VMEM((HEAD_DIM, M_ATT), F32),
            pltpu.VMEM((LANES, TQ), F32),
        ],
        compiler_params=pltpu.CompilerParams(
            dimension_semantics=("arbitrary", "arbitrary", "arbitrary"),
            vmem_limit_bytes=VMEM_LIMIT),
        name="nsa_attention",
    )(q3, gn3, kcmp, vcmp_t, ks4, vst, kw4, vwt, ovt)


def _mlp_kernel(x_ref, yp_ref, yn_ref, gm_ref, wpp_ref, wpn_ref, wo_ref, nm_ref, w1_ref, w2_ref,
                nf_ref, o_ref, *, final):
    p1 = _dot(yp_ref[...], wpp_ref[...])
    p2 = _dot(yn_ref[...], wpn_ref[...])
    ga = jax.nn.sigmoid(gm_ref[:, 0:D_MODEL])
    gb = jax.nn.sigmoid(gm_ref[:, D_MODEL:2 * D_MODEL])
    merged = ga * p1 + gb * p2
    x = x_ref[...] + _dot(merged.astype(BF16), wo_ref[...])
    h = _rms(x, nm_ref[...]).astype(BF16)
    acc = jnp.zeros((TM_MLP, D_MODEL), F32)
    for c in range(D_FF // FF_CHUNK):
        sl = slice(c * FF_CHUNK, (c + 1) * FF_CHUNK)
        a = jnp.square(jnp.maximum(_dot(h, w1_ref[:, sl]), 0.0)).astype(BF16)
        acc = acc + _dot(a, w2_ref[sl, :])
    x = x + acc
    if final:
        x = _rms(x, nf_ref[...])
    o_ref[...] = x


def _merge_mlp(x2, yp, yn, gm, wpp, wpn, wo, nm, w1, w2, nf, final):
    n = x2.shape[0]
    row = lambda w_: pl.BlockSpec((TM_MLP, w_), lambda i: (i, 0))
    res = lambda a: _resident(a.shape, lambda i: (0,) * a.ndim)
    return pl.pallas_call(
        functools.partial(_mlp_kernel, final=final),
        grid=(n // TM_MLP,),
        in_specs=[row(D_MODEL), row(POOL_WIDTH), row(NSA_WIDTH), row(2 * D_MODEL),
                  res(wpp), res(wpn), res(wo), res(nm), res(w1), res(w2), res(nf)],
        out_specs=row(D_MODEL),
        out_shape=jax.ShapeDtypeStruct((n, D_MODEL), F32),
        compiler_params=pltpu.CompilerParams(
            dimension_semantics=("arbitrary",), vmem_limit_bytes=VMEM_LIMIT),
        name="merge_mlp",
    )(x2, yp, yn, gm, wpp, wpn, wo, nm, w1, w2, nf)


def _rope_tables(pos):
    inv = ROPE_THETA ** (-jnp.arange(0, HEAD_DIM, 2, dtype=F32) / HEAD_DIM)
    ang = pos.astype(F32)[:, None] * inv[None, :]
    ang = jnp.concatenate([ang, ang, ang, ang], axis=-1)
    first_half = (jnp.arange(LANES) % HEAD_DIM) < HEAD_DIM // 2
    cos, sin = jnp.cos(ang), jnp.sin(ang)
    return cos, jnp.where(first_half, -sin, 0.0), jnp.where(first_half, 0.0, sin)


def _permute_w_in(w_in):
    o_q = POOL_WIDTH
    o_kv = o_q + NSA_WIDTH
    o_gn = o_kv + 6 * KV_WIDTH
    o_gm = o_gn + N_GATE
    depth = w_in.shape[0]
    gn = w_in[:, :, o_gn:o_gm].reshape(depth, D_MODEL, N_KV_GROUPS, HPG, 3)
    gn = gn.transpose(0, 1, 2, 4, 3).reshape(depth, D_MODEL, N_KV_GROUPS, 3 * HPG)
    gn = jnp.pad(gn, ((0, 0), (0, 0), (0, 0), (0, LANES // N_KV_GROUPS - 3 * HPG)))
    gn = gn.reshape(depth, D_MODEL, LANES)
    return jnp.concatenate(
        [w_in[:, :, 0:o_gn], w_in[:, :, o_gm:], gn], axis=-1).astype(BF16)


def kernel(x, norm_mix, w_in, w_pool, pool_scale, pe_k, pe_v, w_ck1, w_ck2, w_cv1, w_cv2,
           w_proj_pool, w_proj_nsa, w_out, norm_mlp, w_ff1, w_ff2, norm_final):
    b, seq, d = x.shape
    depth = w_in.shape[0]
    n = b * seq
    n_chunks = seq // CMP_STRIDE
    n_slc = seq // SEL_BLOCK
    assert n_slc <= HEAD_DIM, "the selection one-hot shares the 64 spare key lanes"
    assert seq >= WIN_KEYS and seq % TM_IN == 0

    w_in_p = _permute_w_in(w_in)
    w_pool_b = w_pool.astype(BF16)
    pe = jnp.stack([pe_k, pe_v], axis=1).reshape(depth, 2, 1, CMP_LEN * HEAD_DIM)
    w_c1 = jnp.stack([w_ck1, w_cv1], axis=1).astype(BF16)
    pad = jnp.zeros_like(w_ck2)
    w_c2 = jnp.stack([jnp.concatenate([w, pad], axis=-1) for w in (w_ck2, w_ck2, w_cv2, w_cv2)],
                     axis=1).astype(BF16)
    wpp, wpn, wo = w_proj_pool.astype(BF16), w_proj_nsa.astype(BF16), w_out.astype(BF16)
    w1, w2 = w_ff1.astype(BF16), w_ff2.astype(BF16)

    cos, slo, shi = _rope_tables(jnp.arange(seq))
    ccos, cslo, cshi = _rope_tables(jnp.arange(n_chunks) * CMP_STRIDE + CMP_LEN - 1)
    ident = (jnp.ones_like(ccos), jnp.zeros_like(cslo), jnp.zeros_like(cshi))
    cmp_tabs = [jnp.stack([t, i_], axis=0) for t, i_ in zip((ccos, cslo, cshi), ident)]
    cmp_start = jnp.arange(n_chunks) * CMP_STRIDE
    slc_start = jnp.arange(n_slc) * SEL_BLOCK
    ovt = ((cmp_start[None, :] <= slc_start[:, None] + SEL_BLOCK - 1)
           & (cmp_start[None, :] + CMP_LEN - 1 >= slc_start[:, None])).astype(BF16)

    x2 = x.reshape(n, d)
    for l in range(depth):
        u, q, kcvc, ks, vst, kw, vwt, gm, gn = _inproj(
            x2, norm_mix[l][None, :], w_in_p[l], cos, slo, shi, b, seq)
        y_pool = _pool(u.reshape(b, seq, POOL_WIDTH), w_pool_b[l], pool_scale[l][None, :])
        c4 = kcvc.reshape(b, seq, 2 * N_KV_GROUPS, HEAD_DIM).transpose(0, 2, 1, 3)
        c4 = c4.reshape(b, 2 * N_KV_GROUPS, n_chunks, CMP_STRIDE * HEAD_DIM)
        cmp_n, cmp_t = _compress(c4, pe[l], w_c1[l], w_c2[l], *cmp_tabs)
        y_nsa = _attention(
            q.reshape(b, seq, 2 * NSA_WIDTH), gn.reshape(b, seq, LANES), cmp_n, cmp_t,
            ks, vst, kw, vwt, ovt)
        x2 = _merge_mlp(
            x2, y_pool.reshape(n, POOL_WIDTH), y_nsa.reshape(n, NSA_WIDTH), gm,
            wpp[l], wpn[l], wo[l], norm_mlp[l][None, :], w1[l], w2[l], norm_final[None, :],
            final=(l == depth - 1))
    return x2.reshape(b, seq, d)
```

```python
import functools

import jax
import jax.numpy as jnp
import numpy as np
from jax import lax
from jax.experimental import pallas as pl
from jax.experimental.pallas import tpu as pltpu

F32 = jnp.float32
BF16 = jnp.bfloat16

D_MODEL = 1024
POOL_WINDOWS = (2, 4, 8, 16)
POOL_WIDTH = 512
POOL_GW = 128
N_HEADS = 16
HEAD_DIM = 64
N_KV_GROUPS = 2
HPG = 8
NSA_WIDTH = 1024
KV_WIDTH = 128
CMP_LEN = 32
CMP_STRIDE = 16
CMP_HIDDEN = 256
SEL_BLOCK = 64
N_SEL = 16
WINDOW = 512
SEL_BONUS = 1e4
NEG_INF = -1e30
ROPE_THETA = 10000.0
D_FF = 4096
RMS_EPS = 1e-6
N_GATE = 3 * N_HEADS

LANES = 128
VMEM_LIMIT = 56 * 1024 * 1024

C_U = 0
C_Q = C_U + POOL_WIDTH
C_KV = C_Q + NSA_WIDTH
C_GM = C_KV + 6 * KV_WIDTH
C_GN = C_GM + 2 * D_MODEL
N_INP = C_GN + LANES

TM_IN = 512
TS_POOL = 512
POOL_HALO = 16
TM_MLP = 512
FF_CHUNK = 1024
TQ = 256
M_ATT = HPG * TQ
CW = 256
HPC = CW // TQ
NK_SEL = 256
WIN_KEYS = WINDOW + TQ
VT_CHUNK = 128
V_ROWS = HEAD_DIM + 16


def _dot(a, b):
    return jnp.dot(a, b, preferred_element_type=F32)


def _rms(x, g):
    return x * lax.rsqrt(jnp.mean(x * x, axis=-1, keepdims=True) + RMS_EPS) * g


def _rope(t, cos, sin_lo, sin_hi):
    return t * cos + pltpu.roll(t, LANES - 32, 1) * sin_lo + pltpu.roll(t, 32, 1) * sin_hi


def _resident(shape, index_map):
    return pl.BlockSpec(shape, index_map, pipeline_mode=pl.Buffered(1))


def _value_rows(v_t):
    tail_row = lax.broadcasted_iota(jnp.int32, (V_ROWS - HEAD_DIM, v_t.shape[1]), 0)
    return jnp.concatenate([v_t, jnp.where(tail_row == 0, 1.0, 0.0)], axis=0)


def _inproj_kernel(x_ref, g_ref, w_ref, cos_ref, slo_ref, shi_ref,
                   u_ref, q_ref, kcvc_ref, ks_ref, vst_ref, kw_ref, vwt_ref, gm_ref, gn_ref,
                   *, tiles_per_seq):
    h = _rms(x_ref[...], g_ref[...]).astype(BF16)
    cos, slo, shi = cos_ref[...], slo_ref[...], shi_ref[...]
    lane = lax.broadcasted_iota(jnp.int32, (TM_IN, LANES), 1)
    low = lane < HEAD_DIM
    pos = (pl.program_id(0) % tiles_per_seq) * TM_IN + lax.broadcasted_iota(
        jnp.int32, (TM_IN, LANES), 0)
    block_onehot = jnp.where(lane - HEAD_DIM == pos // SEL_BLOCK, 1.0, 0.0)

    u_ref[...] = _dot(h, w_ref[:, C_U:C_Q])
    q = _dot(h, w_ref[:, C_Q:C_KV])
    for k in range(NSA_WIDTH // LANES):
        qt = _rope(q[:, k * LANES:(k + 1) * LANES], cos, slo, shi) * (HEAD_DIM ** -0.5)
        q_ref[:, (2 * k) * LANES:(2 * k + 1) * LANES] = jnp.where(low, qt, 0.0).astype(BF16)
        q_ref[:, (2 * k + 1) * LANES:(2 * k + 2) * LANES] = jnp.where(
            low, pltpu.roll(qt, HEAD_DIM, 1), 0.0).astype(BF16)
    kv = _dot(h, w_ref[:, C_KV:C_GM])
    kcvc_ref[...] = kv[:, 0:2 * KV_WIDTH]
    ks = _rope(kv[:, 2 * KV_WIDTH:3 * KV_WIDTH], cos, slo, shi)
    kw = _rope(kv[:, 4 * KV_WIDTH:5 * KV_WIDTH], cos, slo, shi)
    vs_t = kv[:, 3 * KV_WIDTH:4 * KV_WIDTH].T
    vw_t = kv[:, 5 * KV_WIDTH:6 * KV_WIDTH].T
    for g in range(N_KV_GROUPS):
        ks_g = ks if g == 0 else pltpu.roll(ks, HEAD_DIM, 1)
        kw_g = kw if g == 0 else pltpu.roll(kw, HEAD_DIM, 1)
        ks_ref[0, g] = jnp.where(low, ks_g, block_onehot).astype(BF16)
        kw_ref[0, g] = jnp.where(low, kw_g, 0.0).astype(BF16)
        vs_g = _value_rows(vs_t[g * HEAD_DIM:(g + 1) * HEAD_DIM, :]).astype(BF16)
        vw_g = _value_rows(vw_t[g * HEAD_DIM:(g + 1) * HEAD_DIM, :]).astype(BF16)
        for c in range(TM_IN // VT_CHUNK):
            sl = slice(c * VT_CHUNK, (c + 1) * VT_CHUNK)
            vst_ref[0, g, c] = vs_g[:, sl]
            vwt_ref[0, g, c] = vw_g[:, sl]
    gm_ref[...] = _dot(h, w_ref[:, C_GM:C_GN])
    gn_ref[...] = jax.nn.sigmoid(_dot(h, w_ref[:, C_GN:N_INP]))


def _inproj(x2, g, w, cos, slo, shi, b, seq):
    n = x2.shape[0]
    tiles_per_seq = seq // TM_IN
    n_chunks = TM_IN // VT_CHUNK
    row = lambda w_: pl.BlockSpec((TM_IN, w_), lambda i: (i, 0))
    tab = pl.BlockSpec((TM_IN, LANES), lambda i: (i % tiles_per_seq, 0))
    kg = pl.BlockSpec((1, N_KV_GROUPS, TM_IN, LANES),
                      lambda i: (i // tiles_per_seq, 0, i % tiles_per_seq, 0))
    vt = pl.BlockSpec((1, N_KV_GROUPS, n_chunks, V_ROWS, VT_CHUNK),
                      lambda i: (i // tiles_per_seq, 0, i % tiles_per_seq, 0, 0))
    k_shape = jax.ShapeDtypeStruct((b, N_KV_GROUPS, seq, LANES), BF16)
    vt_shape = jax.ShapeDtypeStruct((b, N_KV_GROUPS, seq // VT_CHUNK, V_ROWS, VT_CHUNK), BF16)
    return pl.pallas_call(
        functools.partial(_inproj_kernel, tiles_per_seq=tiles_per_seq),
        grid=(n // TM_IN,),
        in_specs=[row(D_MODEL), _resident((1, D_MODEL), lambda i: (0, 0)),
                  _resident((D_MODEL, N_INP), lambda i: (0, 0)), tab, tab, tab],
        out_specs=[row(POOL_WIDTH), row(2 * NSA_WIDTH), row(2 * KV_WIDTH), kg, vt,
                   kg, vt, row(2 * D_MODEL), row(LANES)],
        out_shape=[
            jax.ShapeDtypeStruct((n, POOL_WIDTH), F32),
            jax.ShapeDtypeStruct((n, 2 * NSA_WIDTH), BF16),
            jax.ShapeDtypeStruct((n, 2 * KV_WIDTH), F32),
            k_shape, vt_shape, k_shape, vt_shape,
            jax.ShapeDtypeStruct((n, 2 * D_MODEL), F32),
            jax.ShapeDtypeStruct((n, LANES), F32),
        ],
        compiler_params=pltpu.CompilerParams(
            dimension_semantics=("arbitrary",), vmem_limit_bytes=VMEM_LIMIT),
        name="in_proj",
    )(x2, g, w, cos, slo, shi)


def _pool_kernel(u_ref, up_ref, wp_ref, sc_ref, o_ref):
    i = pl.program_id(1)
    cur = u_ref[0]
    prev = jnp.where(i > 0, up_ref[0], 0.0)
    t = i * TS_POOL + lax.broadcasted_iota(jnp.int32, (TS_POOL, POOL_GW), 0)
    for g, w in enumerate(POOL_WINDOWS):
        sl = slice(g * POOL_GW, (g + 1) * POOL_GW)
        cg = cur[:, sl]
        s = jnp.concatenate([prev[:, sl], cg], axis=0)
        sh = 1
        while sh < w:
            s = s + pltpu.roll(s, sh, 0)
            sh *= 2
        cnt = jnp.minimum(t + 1, w).astype(F32)
        d = s[POOL_HALO:] / cnt - cg
        y = _dot(d.astype(BF16), wp_ref[g]) * sc_ref[:, sl]
        o_ref[0, :, sl] = y.astype(BF16)


def _pool(u3, wp, sc):
    b, seq, _ = u3.shape
    halo_per_tile = TS_POOL // POOL_HALO
    return pl.pallas_call(
        _pool_kernel,
        grid=(b, seq // TS_POOL),
        in_specs=[
            pl.BlockSpec((1, TS_POOL, POOL_WIDTH), lambda bi, i: (bi, i, 0)),
            pl.BlockSpec((1, POOL_HALO, POOL_WIDTH),
                         lambda bi, i: (bi, jnp.maximum(i * halo_per_tile - 1, 0), 0)),
            pl.BlockSpec((len(POOL_WINDOWS), POOL_GW, POOL_GW), lambda bi, i: (0, 0, 0)),
            pl.BlockSpec((1, POOL_WIDTH), lambda bi, i: (0, 0)),
        ],
        out_specs=pl.BlockSpec((1, TS_POOL, POOL_WIDTH), lambda bi, i: (bi, i, 0)),
        out_shape=jax.ShapeDtypeStruct((b, seq, POOL_WIDTH), BF16),
        compiler_params=pltpu.CompilerParams(dimension_semantics=("arbitrary", "arbitrary")),
        name="pool_mixer",
    )(u3, u3, wp, sc)


def _compress_kernel(c_ref, pe_ref, w1_ref, w2_ref, cos_ref, slo_ref, shi_ref, o_ref, ot_ref):
    half = CMP_STRIDE * HEAD_DIM
    c = c_ref[...]
    a = _dot((c + pe_ref[:, 0:half]).astype(BF16), w1_ref[0:half, :])
    b = _dot((c + pe_ref[:, half:2 * half]).astype(BF16), w1_ref[half:2 * half, :])
    n_rows = c.shape[0]
    hid = a + pltpu.roll(b, n_rows - 1, 0)
    act = jax.nn.gelu(hid, approximate=True)
    out = _dot(act.astype(BF16), w2_ref[...])
    out = _rope(out, cos_ref[...], slo_ref[...], shi_ref[...])
    row = lax.broadcasted_iota(jnp.int32, out.shape, 0)
    out = jnp.where(row < n_rows - 1, out, 0.0)
    o_ref[...] = out.astype(BF16)
    ot_ref[...] = out.T[0:HEAD_DIM, :].astype(BF16)


def _compress(c4, pe, w1, w2p, cos, slo, shi):
    b, n_kv, n_chunks, half = c4.shape
    return pl.pallas_call(
        _compress_kernel,
        grid=(b, n_kv),
        in_specs=[
            pl.BlockSpec((None, None, n_chunks, half), lambda bi, j: (bi, j, 0, 0)),
            pl.BlockSpec((None, 1, 2 * half), lambda bi, j: (j // N_KV_GROUPS, 0, 0)),
            pl.BlockSpec((None, 2 * half, CMP_HIDDEN), lambda bi, j: (j // N_KV_GROUPS, 0, 0)),
            pl.BlockSpec((None, CMP_HIDDEN, LANES), lambda bi, j: (j, 0, 0)),
            pl.BlockSpec((None, n_chunks, LANES), lambda bi, j: (j // N_KV_GROUPS, 0, 0)),
            pl.BlockSpec((None, n_chunks, LANES), lambda bi, j: (j // N_KV_GROUPS, 0, 0)),
            pl.BlockSpec((None, n_chunks, LANES), lambda bi, j: (j // N_KV_GROUPS, 0, 0)),
        ],
        out_specs=[
            pl.BlockSpec((None, None, n_chunks, LANES), lambda bi, j: (bi, j, 0, 0)),
            pl.BlockSpec((None, None, HEAD_DIM, n_chunks), lambda bi, j: (bi, j, 0, 0)),
        ],
        out_shape=[
            jax.ShapeDtypeStruct((b, n_kv, n_chunks, LANES), BF16),
            jax.ShapeDtypeStruct((b, n_kv, HEAD_DIM, n_chunks), BF16),
        ],
        compiler_params=pltpu.CompilerParams(dimension_semantics=("arbitrary", "arbitrary")),
        name="compress",
    )(c4, pe, w1, w2p, cos, slo, shi)


def _block_rank(score):
    n_slc = score.shape[0]
    sub = 8
    ranks = []
    for v in range(n_slc // sub):
        blk = score[v * sub:(v + 1) * sub, :]
        jb_v = v * sub + lax.broadcasted_iota(jnp.int32, blk.shape, 0)
        r = jnp.zeros(blk.shape, F32)
        for jp in range(n_slc):
            row = score[jp:jp + 1, :]
            ge = jnp.where(row >= blk, 1.0, 0.0)
            gt = jnp.where(row > blk, 1.0, 0.0)
            if jp < v * sub:
                r = r + ge
            elif jp >= (v + 1) * sub:
                r = r + gt
            else:
                r = r + jnp.where(jb_v > jp, ge, gt)
        ranks.append(r)
    return jnp.concatenate(ranks, axis=0)


def _attn_kernel(q_ref, gn_ref, kc_ref, vct_ref, ks_ref, vst_ref, kw_ref, vwt_ref, ovt_ref,
                 o_ref, qs_ref, qsel_ref, s_ref, p_ref, al_ref, mw_ref, ms_ref, accw_ref, accs_ref,
                 tot_ref, gt_ref):
    g = pl.program_id(1)
    i = pl.program_id(2)
    s0 = i * TQ
    n_cmp = kc_ref.shape[0]
    n_slc = ovt_ref.shape[0]
    n_ch = M_ATT // CW
    chunk = lambda ch: slice(ch * CW, (ch + 1) * CW)

    for hh in range(HPG):
        qs_ref[:, hh * TQ:(hh + 1) * TQ] = (
            q_ref[0, :, hh * LANES:(hh + 1) * LANES].astype(F32).T.astype(BF16))

    gt_ref[...] = gn_ref[0].T
    g_base = g * HEAD_DIM

    def gate_rows(c, ch):
        rows = [gt_ref[pl.ds(g_base + c * HPG + HPC * ch + par, 1), :] for par in range(HPC)]
        return rows[0] if HPC == 1 else jnp.concatenate(rows, axis=1)

    def two_heads(a):
        return a if HPC == 1 else jnp.concatenate([a] * HPC, axis=1)

    def key_tile(ref, k0, nk):
        return ref[pl.ds(pl.multiple_of(k0, VT_CHUNK), nk), :]

    def value_tile(ref, k0, nk):
        c0 = k0 // VT_CHUNK
        return jnp.concatenate([ref[c0 + c] for c in range(nk // VT_CHUNK)], axis=1)

    imp_parts = []

    def run_step(soft=None, score=None, value=None, cmp_soft=None, cmp_value=False):
        for ch in range(n_ch):
            cs = chunk(ch)
            if soft is not None:
                nk, m_ref, bias, first = soft
                s = s_ref[0:nk, cs]
                if bias is not None:
                    s = s + bias
                mx = jnp.max(s, axis=0, keepdims=True)
                if first:
                    m_new, al_new = mx, None
                else:
                    m_prev = m_ref[:, cs]
                    m_new = jnp.maximum(m_prev, mx)
                    al_new = jnp.exp(m_prev - m_new)
                p_new = jnp.exp(s - m_new).astype(BF16)
            if cmp_soft is not None:
                s = s_ref[0:n_cmp, cs] + cmp_soft
                mx = jnp.max(s, axis=0, keepdims=True)
                p = jnp.exp(s - mx)
                lsum = jnp.sum(p, axis=0, keepdims=True)
                pn_new = (p * jnp.where(mx > 0.5 * NEG_INF, 1.0 / lsum, 0.0)).astype(BF16)
            if score is not None:
                kt, q_t_ref = score
                s_ref[0:kt.shape[0], cs] = _dot(kt, q_t_ref[:, cs])
            if value is not None:
                vt, acc_ref, vfirst = value
                pv = _dot(vt, p_ref[0:vt.shape[1], cs])
                acc_ref[:, cs] = pv if vfirst else al_ref[:, cs] * acc_ref[:, cs] + pv
            if cmp_value:
                pn = p_ref[0:n_cmp, cs]
                tot_ref[:, cs] = _dot(vct_ref[...], pn) * gate_rows(0, ch)
                for par in range(HPC):
                    imp_parts.append(_dot(ovt_ref[...], pn[:, par * TQ:(par + 1) * TQ]))
            if soft is not None:
                p_ref[0:nk, cs] = p_new
                m_ref[:, cs] = m_new
                if al_new is not None:
                    al_ref[:, cs] = al_new
            if cmp_soft is not None:
                p_ref[0:n_cmp, cs] = pn_new

    n_idx = lax.broadcasted_iota(jnp.int32, (n_cmp, TQ), 0)
    t_cmp = s0 + lax.broadcasted_iota(jnp.int32, (n_cmp, TQ), 1)
    cmp_bias = two_heads(jnp.where(n_idx * CMP_STRIDE + CMP_LEN - 1 <= t_cmp, 0.0, NEG_INF))

    w0 = jnp.maximum(s0 - WINDOW, 0)
    win_tiles = [(off, min(NK_SEL, WIN_KEYS - off)) for off in range(0, WIN_KEYS, NK_SEL)]
    n_win = len(win_tiles)

    def win_bias(off, nk):
        kpos = w0 + off + lax.broadcasted_iota(jnp.int32, (nk, TQ), 0)
        t_w = s0 + lax.broadcasted_iota(jnp.int32, (nk, TQ), 1)
        return two_heads(
            jnp.where(kpos <= t_w, jnp.where(kpos > t_w - WINDOW, 0.0, NEG_INF), NEG_INF))

    def win_keys(w):
        return key_tile(kw_ref, w0 + win_tiles[w][0], win_tiles[w][1])

    def win_values(w):
        return value_tile(vwt_ref, w0 + win_tiles[w][0], win_tiles[w][1])

    def win_soft(w, first):
        return (win_tiles[w][1], mw_ref, win_bias(*win_tiles[w]), first)

    def win_step(k, last_score=None):
        args = {}
        if k < n_win:
            args["score"] = (win_keys(k), qs_ref)
        elif k == n_win and last_score is not None:
            args["score"] = last_score
        if 1 <= k <= n_win:
            args["soft"] = win_soft(k - 1, k == 1)
        if 2 <= k <= n_win + 1:
            args["value"] = (win_values(k - 2), accw_ref, k == 2)
        return args

    run_step(score=(kc_ref[...], qs_ref))
    run_step(cmp_soft=cmp_bias, **win_step(0))
    run_step(cmp_value=True, **win_step(1))
    for k in range(2, n_win):
        run_step(**win_step(k))

    imp = functools.reduce(lambda a, b: a + b, imp_parts)
    jb = lax.broadcasted_iota(jnp.int32, (n_slc, TQ), 0)
    tq = s0 + lax.broadcasted_iota(jnp.int32, (n_slc, TQ), 1)
    causal = jb * SEL_BLOCK <= tq
    near = jnp.logical_or(jb == 0, jb >= tq // SEL_BLOCK - 1)
    score = jnp.where(causal, jnp.where(near, SEL_BONUS, imp), NEG_INF)
    rank = jnp.concatenate(
        [_block_rank(score[:, c0:c0 + LANES])
         for c0 in range(0, TQ, LANES)], axis=1)
    picked = rank < float(N_SEL)
    sel_bias = jnp.where(causal, jnp.where(picked, 0.0, NEG_INF), NEG_INF)
    parts = [jnp.zeros((HEAD_DIM, TQ), F32), sel_bias]
    if n_slc < HEAD_DIM:
        parts.append(jnp.zeros((HEAD_DIM - n_slc, TQ), F32))
    sel_rows = jnp.concatenate(parts, axis=0).astype(BF16)
    for hh in range(HPG):
        hs = slice(hh * TQ, (hh + 1) * TQ)
        qsel_ref[:, hs] = qs_ref[:, hs] + sel_rows

    ms_ref[...] = jnp.full(ms_ref.shape, NEG_INF, F32)
    accs_ref[...] = jnp.zeros(accs_ref.shape, F32)
    j_diag = s0 // NK_SEL

    def sel_values(j, live):
        vt = value_tile(vst_ref, jnp.maximum(j, 0) * NK_SEL, NK_SEL)
        return jnp.where(live, vt, jnp.zeros_like(vt))

    run_step(**win_step(n_win, last_score=(key_tile(ks_ref, 0, NK_SEL), qsel_ref)))
    run_step(**win_step(n_win + 1))

    def sel_body(j, carry):
        run_step(soft=(NK_SEL, ms_ref, None, False),
                 score=(key_tile(ks_ref, j * NK_SEL, NK_SEL), qsel_ref),
                 value=(sel_values(j - 2, j >= 2), accs_ref, False))
        return carry

    lax.fori_loop(1, j_diag + 1, sel_body, 0)
    kpos = j_diag * NK_SEL + lax.broadcasted_iota(jnp.int32, (NK_SEL, TQ), 0)
    t_sel = s0 + lax.broadcasted_iota(jnp.int32, (NK_SEL, TQ), 1)
    diag_bias = two_heads(jnp.where(kpos <= t_sel, 0.0, NEG_INF))
    run_step(soft=(NK_SEL, ms_ref, diag_bias, False),
             value=(sel_values(j_diag - 1, j_diag >= 1), accs_ref, False))
    run_step(value=(sel_values(j_diag, True), accs_ref, False))

    heads = []
    for ch in range(n_ch):
        cs = chunk(ch)
        coef_w = gate_rows(2, ch) / accw_ref[HEAD_DIM:HEAD_DIM + 1, cs]
        coef_s = gate_rows(1, ch) / accs_ref[HEAD_DIM:HEAD_DIM + 1, cs]
        tot = (tot_ref[:, cs] + accw_ref[0:HEAD_DIM, cs] * coef_w
               + accs_ref[0:HEAD_DIM, cs] * coef_s)
        heads += [tot[:, par * TQ:(par + 1) * TQ] for par in range(HPC)]
    for pair in range(HPG // 2):
        both = jnp.concatenate(heads[2 * pair:2 * pair + 2], axis=0)
        o_ref[0, :, pair * LANES:(pair + 1) * LANES] = both.T.astype(BF16)


def _attention(q3, gn3, kcmp, vcmp_t, ks4, vst, kw4, vwt, ovt):
    b, seq, _ = q3.shape
    n_cmp = kcmp.shape[2]
    n_slc = seq // SEL_BLOCK
    chunks_per_seq = seq // VT_CHUNK
    per_group = lambda bi, g, i: (bi, g, 0, 0)
    return pl.pallas_call(
        _attn_kernel,
        grid=(b, N_KV_GROUPS, seq // TQ),
        in_specs=[
            pl.BlockSpec((1, TQ, HPG * LANES), lambda bi, g, i: (bi, i, g)),
            pl.BlockSpec((1, TQ, LANES), lambda bi, g, i: (bi, i, 0)),
            pl.BlockSpec((None, None, n_cmp, LANES), per_group),
            pl.BlockSpec((None, None, HEAD_DIM, n_cmp),
                         lambda bi, g, i: (bi, N_KV_GROUPS + g, 0, 0)),
            pl.BlockSpec((None, None, seq, LANES), per_group),
            pl.BlockSpec((None, None, chunks_per_seq, V_ROWS, VT_CHUNK),
                         lambda bi, g, i: (bi, g, 0, 0, 0)),
            pl.BlockSpec((None, None, seq, LANES), per_group),
            pl.BlockSpec((None, None, chunks_per_seq, V_ROWS, VT_CHUNK),
                         lambda bi, g, i: (bi, g, 0, 0, 0)),
            pl.BlockSpec((n_slc, n_cmp), lambda bi, g, i: (0, 0)),
        ],
        out_specs=pl.BlockSpec((1, TQ, HPG * HEAD_DIM), lambda bi, g, i: (bi, i, g)),
        out_shape=jax.ShapeDtypeStruct((b, seq, NSA_WIDTH), BF16),
        scratch_shapes=[
            pltpu.VMEM((LANES, M_ATT), BF16),
            pltpu.VMEM((LANES, M_ATT), BF16),
            pltpu.VMEM((NK_SEL, M_ATT), F32),
            pltpu.VMEM((NK_SEL, M_ATT), BF16),
            pltpu.VMEM((1, M_ATT), F32),
            pltpu.VMEM((1, M_ATT), F32),
            pltpu.VMEM((1, M_ATT), F32),
            pltpu.VMEM((V_ROWS, M_ATT), F32),
            pltpu.VMEM((V_ROWS, M_ATT), F32),
            pltpu.VMEM((HEAD_DIM, M_ATT), F32),
            pltpu.VMEM((LANES, TQ), F32),
        ],
        compiler_params=pltpu.CompilerParams(
            dimension_semantics=("arbitrary", "arbitrary", "arbitrary"),
            vmem_limit_bytes=VMEM_LIMIT),
        name="nsa_attention",
    )(q3, gn3, kcmp, vcmp_t, ks4, vst, kw4, vwt, ovt)


def _mlp_kernel(x_ref, yp_ref, yn_ref, gm_ref, wpp_ref, wpn_ref, wo_ref, nm_ref, w1_ref, w2_ref,
                nf_ref, o_ref, *, final):
    p1 = _dot(yp_ref[...], wpp_ref[...])
    p2 = _dot(yn_ref[...], wpn_ref[...])
    ga = jax.nn.sigmoid(gm_ref[:, 0:D_MODEL])
    gb = jax.nn.sigmoid(gm_ref[:, D_MODEL:2 * D_MODEL])
    merged = ga * p1 + gb * p2
    x = x_ref[...] + _dot(merged.astype(BF16), wo_ref[...])
    h = _rms(x, nm_ref[...]).astype(BF16)
    acc = jnp.zeros((TM_MLP, D_MODEL), F32)
    for c in range(D_FF // FF_CHUNK):
        sl = slice(c * FF_CHUNK, (c + 1) * FF_CHUNK)
        a = jnp.square(jnp.maximum(_dot(h, w1_ref[:, sl]), 0.0)).astype(BF16)
        acc = acc + _dot(a, w2_ref[sl, :])
    x = x + acc
    if final:
        x = _rms(x, nf_ref[...])
    o_ref[...] = x


def _merge_mlp(x2, yp, yn, gm, wpp, wpn, wo, nm, w1, w2, nf, final):
    n = x2.shape[0]
    row = lambda w_: pl.BlockSpec((TM_MLP, w_), lambda i: (i, 0))
    res = lambda a: _resident(a.shape, lambda i: (0,) * a.ndim)
    return pl.pallas_call(
        functools.partial(_mlp_kernel, final=final),
        grid=(n // TM_MLP,),
        in_specs=[row(D_MODEL), row(POOL_WIDTH), row(NSA_WIDTH), row(2 * D_MODEL),
                  res(wpp), res(wpn), res(wo), res(nm), res(w1), res(w2), res(nf)],
        out_specs=row(D_MODEL),
        out_shape=jax.ShapeDtypeStruct((n, D_MODEL), F32),
        compiler_params=pltpu.CompilerParams(
            dimension_semantics=("arbitrary",), vmem_limit_bytes=VMEM_LIMIT),
        name="merge_mlp",
    )(x2, yp, yn, gm, wpp, wpn, wo, nm, w1, w2, nf)


def _rope_tables(pos):
    inv = ROPE_THETA ** (-jnp.arange(0, HEAD_DIM, 2, dtype=F32) / HEAD_DIM)
    ang = pos.astype(F32)[:, None] * inv[None, :]
    ang = jnp.concatenate([ang, ang, ang, ang], axis=-1)
    first_half = (jnp.arange(LANES) % HEAD_DIM) < HEAD_DIM // 2
    cos, sin = jnp.cos(ang), jnp.sin(ang)
    return cos, jnp.where(first_half, -sin, 0.0), jnp.where(first_half, 0.0, sin)


def _permute_w_in(w_in):
    o_q = POOL_WIDTH
    o_kv = o_q + NSA_WIDTH
    o_gn = o_kv + 6 * KV_WIDTH
    o_gm = o_gn + N_GATE
    depth = w_in.shape[0]
    gn = w_in[:, :, o_gn:o_gm].reshape(depth, D_MODEL, N_KV_GROUPS, HPG, 3)
    gn = gn.transpose(0, 1, 2, 4, 3).reshape(depth, D_MODEL, N_KV_GROUPS, 3 * HPG)
    gn = jnp.pad(gn, ((0, 0), (0, 0), (0, 0), (0, LANES // N_KV_GROUPS - 3 * HPG)))
    gn = gn.reshape(depth, D_MODEL, LANES)
    return jnp.concatenate(
        [w_in[:, :, 0:o_gn], w_in[:, :, o_gm:], gn], axis=-1).astype(BF16)


def kernel(x, norm_mix, w_in, w_pool, pool_scale, pe_k, pe_v, w_ck1, w_ck2, w_cv1, w_cv2,
           w_proj_pool, w_proj_nsa, w_out, norm_mlp, w_ff1, w_ff2, norm_final):
    b, seq, d = x.shape
    depth = w_in.shape[0]
    n = b * seq
    n_chunks = seq // CMP_STRIDE
    n_slc = seq // SEL_BLOCK
    assert n_slc <= HEAD_DIM, "the selection one-hot shares the 64 spare key lanes"
    assert seq >= WIN_KEYS and seq % TM_IN == 0

    w_in_p = _permute_w_in(w_in)
    w_pool_b = w_pool.astype(BF16)
    pe = jnp.stack([pe_k, pe_v], axis=1).reshape(depth, 2, 1, CMP_LEN * HEAD_DIM)
    w_c1 = jnp.stack([w_ck1, w_cv1], axis=1).astype(BF16)
    pad = jnp.zeros_like(w_ck2)
    w_c2 = jnp.stack([jnp.concatenate([w, pad], axis=-1) for w in (w_ck2, w_ck2, w_cv2, w_cv2)],
                     axis=1).astype(BF16)
    wpp, wpn, wo = w_proj_pool.astype(BF16), w_proj_nsa.astype(BF16), w_out.astype(BF16)
    w1, w2 = w_ff1.astype(BF16), w_ff2.astype(BF16)

    cos, slo, shi = _rope_tables(jnp.arange(seq))
    ccos, cslo, cshi = _rope_tables(jnp.arange(n_chunks) * CMP_STRIDE + CMP_LEN - 1)
    ident = (jnp.ones_like(ccos), jnp.zeros_like(cslo), jnp.zeros_like(cshi))
    cmp_tabs = [jnp.stack([t, i_], axis=0) for t, i_ in zip((ccos, cslo, cshi), ident)]
    cmp_start = jnp.arange(n_chunks) * CMP_STRIDE
    slc_start = jnp.arange(n_slc) * SEL_BLOCK
    ovt = ((cmp_start[None, :] <= slc_start[:, None] + SEL_BLOCK - 1)
           & (cmp_start[None, :] + CMP_LEN - 1 >= slc_start[:, None])).astype(BF16)

    x2 = x.reshape(n, d)
    for l in range(depth):
        u, q, kcvc, ks, vst, kw, vwt, gm, gn = _inproj(
            x2, norm_mix[l][None, :], w_in_p[l], cos, slo, shi, b, seq)
        y_pool = _pool(u.reshape(b, seq, POOL_WIDTH), w_pool_b[l], pool_scale[l][None, :])
        c4 = kcvc.reshape(b, seq, 2 * N_KV_GROUPS, HEAD_DIM).transpose(0, 2, 1, 3)
        c4 = c4.reshape(b, 2 * N_KV_GROUPS, n_chunks, CMP_STRIDE * HEAD_DIM)
        cmp_n, cmp_t = _compress(c4, pe[l], w_c1[l], w_c2[l], *cmp_tabs)
        y_nsa = _attention(
            q.reshape(b, seq, 2 * NSA_WIDTH), gn.reshape(b, seq, LANES), cmp_n, cmp_t,
            ks, vst, kw, vwt, ovt)
        x2 = _merge_mlp(
            x2, y_pool.reshape(n, POOL_WIDTH), y_nsa.reshape(n, NSA_WIDTH), gm,
            wpp[l], wpn[l], wo[l], norm_mlp[l][None, :], w1[l], w2[l], norm_final[None, :],
            final=(l == depth - 1))
    return x2.reshape(b, seq, d)
```

```python
import functools

import jax
import jax.numpy as jnp
import numpy as np
from jax import lax
from jax.experimental import pallas as pl
from jax.experimental.pallas import tpu as pltpu

F32 = jnp.float32
BF16 = jnp.bfloat16

D_MODEL = 1024
POOL_WINDOWS = (2, 4, 8, 16)
POOL_WIDTH = 512
POOL_GW = 128
N_HEADS = 16
HEAD_DIM = 64
N_KV_GROUPS = 2
HPG = 8
NSA_WIDTH = 1024
KV_WIDTH = 128
CMP_LEN = 32
CMP_STRIDE = 16
CMP_HIDDEN = 256
SEL_BLOCK = 64
N_SEL = 16
WINDOW = 512
SEL_BONUS = 1e4
NEG_INF = -1e30
ROPE_THETA = 10000.0
D_FF = 4096
RMS_EPS = 1e-6
N_GATE = 3 * N_HEADS
Q_SCALE = HEAD_DIM ** -0.5 * float(np.log2(np.e))

LANES = 128
VMEM_LIMIT = 56 * 1024 * 1024

C_U = 0
C_Q = C_U + POOL_WIDTH
C_KV = C_Q + NSA_WIDTH
C_GM = C_KV + 6 * KV_WIDTH
C_GN = C_GM + 2 * D_MODEL
N_INP = C_GN + LANES

TM_IN = 512
TS_POOL = 512
POOL_HALO = 16
TM_MLP = 512
FF_CHUNK = 1024
TQ = 256
M_ATT = HPG * TQ
CW = 256
HPC = CW // TQ
NK_SEL = 256
WIN_KEYS = WINDOW + TQ
VT_CHUNK = 128
V_ROWS = HEAD_DIM + 16


def _dot(a, b):
    return jnp.dot(a, b, preferred_element_type=F32)


def _rms(x, g):
    return x * lax.rsqrt(jnp.mean(x * x, axis=-1, keepdims=True) + RMS_EPS) * g


def _rope(t, cos, sin_lo, sin_hi):
    return t * cos + pltpu.roll(t, LANES - 32, 1) * sin_lo + pltpu.roll(t, 32, 1) * sin_hi


def _resident(shape, index_map):
    return pl.BlockSpec(shape, index_map, pipeline_mode=pl.Buffered(1))


def _value_rows(v_t):
    tail_row = lax.broadcasted_iota(jnp.int32, (V_ROWS - HEAD_DIM, v_t.shape[1]), 0)
    return jnp.concatenate([v_t, jnp.where(tail_row == 0, 1.0, 0.0)], axis=0)


def _inproj_kernel(x_ref, g_ref, w_ref, cos_ref, slo_ref, shi_ref,
                   u_ref, q_ref, kcvc_ref, ks_ref, vst_ref, kw_ref, vwt_ref, gm_ref, gn_ref,
                   *, tiles_per_seq):
    h = _rms(x_ref[...], g_ref[...]).astype(BF16)
    cos, slo, shi = cos_ref[...], slo_ref[...], shi_ref[...]
    lane = lax.broadcasted_iota(jnp.int32, (TM_IN, LANES), 1)
    low = lane < HEAD_DIM
    pos = (pl.program_id(0) % tiles_per_seq) * TM_IN + lax.broadcasted_iota(
        jnp.int32, (TM_IN, LANES), 0)
    block_onehot = jnp.where(lane - HEAD_DIM == pos // SEL_BLOCK, 1.0, 0.0)

    u_ref[...] = _dot(h, w_ref[:, C_U:C_Q])
    q = _dot(h, w_ref[:, C_Q:C_KV])
    for k in range(NSA_WIDTH // LANES):
        qt = _rope(q[:, k * LANES:(k + 1) * LANES], cos, slo, shi) * Q_SCALE
        q_ref[:, (2 * k) * LANES:(2 * k + 1) * LANES] = jnp.where(low, qt, 0.0).astype(BF16)
        q_ref[:, (2 * k + 1) * LANES:(2 * k + 2) * LANES] = jnp.where(
            low, pltpu.roll(qt, HEAD_DIM, 1), 0.0).astype(BF16)
    kv = _dot(h, w_ref[:, C_KV:C_GM])
    kcvc_ref[...] = kv[:, 0:2 * KV_WIDTH]
    ks = _rope(kv[:, 2 * KV_WIDTH:3 * KV_WIDTH], cos, slo, shi)
    kw = _rope(kv[:, 4 * KV_WIDTH:5 * KV_WIDTH], cos, slo, shi)
    vs_t = kv[:, 3 * KV_WIDTH:4 * KV_WIDTH].T
    vw_t = kv[:, 5 * KV_WIDTH:6 * KV_WIDTH].T
    for g in range(N_KV_GROUPS):
        ks_g = ks if g == 0 else pltpu.roll(ks, HEAD_DIM, 1)
        kw_g = kw if g == 0 else pltpu.roll(kw, HEAD_DIM, 1)
        ks_ref[0, g] = jnp.where(low, ks_g, block_onehot).astype(BF16)
        kw_ref[0, g] = jnp.where(low, kw_g, 0.0).astype(BF16)
        vs_g = _value_rows(vs_t[g * HEAD_DIM:(g + 1) * HEAD_DIM, :]).astype(BF16)
        vw_g = _value_rows(vw_t[g * HEAD_DIM:(g + 1) * HEAD_DIM, :]).astype(BF16)
        for c in range(TM_IN // VT_CHUNK):
            sl = slice(c * VT_CHUNK, (c + 1) * VT_CHUNK)
            vst_ref[0, g, c] = vs_g[:, sl]
            vwt_ref[0, g, c] = vw_g[:, sl]
    gm_ref[...] = _dot(h, w_ref[:, C_GM:C_GN])
    gn_ref[...] = jax.nn.sigmoid(_dot(h, w_ref[:, C_GN:N_INP]))


def _inproj(x2, g, w, cos, slo, shi, b, seq):
    n = x2.shape[0]
    tiles_per_seq = seq // TM_IN
    n_chunks = TM_IN // VT_CHUNK
    row = lambda w_: pl.BlockSpec((TM_IN, w_), lambda i: (i, 0))
    tab = pl.BlockSpec((TM_IN, LANES), lambda i: (i % tiles_per_seq, 0))
    kg = pl.BlockSpec((1, N_KV_GROUPS, TM_IN, LANES),
                      lambda i: (i // tiles_per_seq, 0, i % tiles_per_seq, 0))
    vt = pl.BlockSpec((1, N_KV_GROUPS, n_chunks, V_ROWS, VT_CHUNK),
                      lambda i: (i // tiles_per_seq, 0, i % tiles_per_seq, 0, 0))
    k_shape = jax.ShapeDtypeStruct((b, N_KV_GROUPS, seq, LANES), BF16)
    vt_shape = jax.ShapeDtypeStruct((b, N_KV_GROUPS, seq // VT_CHUNK, V_ROWS, VT_CHUNK), BF16)
    return pl.pallas_call(
        functools.partial(_inproj_kernel, tiles_per_seq=tiles_per_seq),
        grid=(n // TM_IN,),
        in_specs=[row(D_MODEL), _resident((1, D_MODEL), lambda i: (0, 0)),
                  _resident((D_MODEL, N_INP), lambda i: (0, 0)), tab, tab, tab],
        out_specs=[row(POOL_WIDTH), row(2 * NSA_WIDTH), row(2 * KV_WIDTH), kg, vt,
                   kg, vt, row(2 * D_MODEL), row(LANES)],
        out_shape=[
            jax.ShapeDtypeStruct((n, POOL_WIDTH), F32),
            jax.ShapeDtypeStruct((n, 2 * NSA_WIDTH), BF16),
            jax.ShapeDtypeStruct((n, 2 * KV_WIDTH), F32),
            k_shape, vt_shape, k_shape, vt_shape,
            jax.ShapeDtypeStruct((n, 2 * D_MODEL), F32),
            jax.ShapeDtypeStruct((n, LANES), F32),
        ],
        compiler_params=pltpu.CompilerParams(
            dimension_semantics=("arbitrary",), vmem_limit_bytes=VMEM_LIMIT),
        name="in_proj",
    )(x2, g, w, cos, slo, shi)


def _pool_kernel(u_ref, up_ref, wp_ref, sc_ref, o_ref):
    i = pl.program_id(1)
    cur = u_ref[0]
    prev = jnp.where(i > 0, up_ref[0], 0.0)
    t = i * TS_POOL + lax.broadcasted_iota(jnp.int32, (TS_POOL, POOL_GW), 0)
    for g, w in enumerate(POOL_WINDOWS):
        sl = slice(g * POOL_GW, (g + 1) * POOL_GW)
        cg = cur[:, sl]
        s = jnp.concatenate([prev[:, sl], cg], axis=0)
        sh = 1
        while sh < w:
            s = s + pltpu.roll(s, sh, 0)
            sh *= 2
        cnt = jnp.minimum(t + 1, w).astype(F32)
        d = s[POOL_HALO:] / cnt - cg
        y = _dot(d.astype(BF16), wp_ref[g]) * sc_ref[:, sl]
        o_ref[0, :, sl] = y.astype(BF16)


def _pool(u3, wp, sc):
    b, seq, _ = u3.shape
    halo_per_tile = TS_POOL // POOL_HALO
    return pl.pallas_call(
        _pool_kernel,
        grid=(b, seq // TS_POOL),
        in_specs=[
            pl.BlockSpec((1, TS_POOL, POOL_WIDTH), lambda bi, i: (bi, i, 0)),
            pl.BlockSpec((1, POOL_HALO, POOL_WIDTH),
                         lambda bi, i: (bi, jnp.maximum(i * halo_per_tile - 1, 0), 0)),
            pl.BlockSpec((len(POOL_WINDOWS), POOL_GW, POOL_GW), lambda bi, i: (0, 0, 0)),
            pl.BlockSpec((1, POOL_WIDTH), lambda bi, i: (0, 0)),
        ],
        out_specs=pl.BlockSpec((1, TS_POOL, POOL_WIDTH), lambda bi, i: (bi, i, 0)),
        out_shape=jax.ShapeDtypeStruct((b, seq, POOL_WIDTH), BF16),
        compiler_params=pltpu.CompilerParams(dimension_semantics=("arbitrary", "arbitrary")),
        name="pool_mixer",
    )(u3, u3, wp, sc)


def _compress_kernel(c_ref, pe_ref, w1_ref, w2_ref, cos_ref, slo_ref, shi_ref, o_ref, ot_ref):
    half = CMP_STRIDE * HEAD_DIM
    c = c_ref[...]
    a = _dot((c + pe_ref[:, 0:half]).astype(BF16), w1_ref[0:half, :])
    b = _dot((c + pe_ref[:, half:2 * half]).astype(BF16), w1_ref[half:2 * half, :])
    n_rows = c.shape[0]
    hid = a + pltpu.roll(b, n_rows - 1, 0)
    act = jax.nn.gelu(hid, approximate=True)
    out = _dot(act.astype(BF16), w2_ref[...])
    out = _rope(out, cos_ref[...], slo_ref[...], shi_ref[...])
    row = lax.broadcasted_iota(jnp.int32, out.shape, 0)
    out = jnp.where(row < n_rows - 1, out, 0.0)
    o_ref[...] = out.astype(BF16)
    ot_ref[...] = out.T[0:HEAD_DIM, :].astype(BF16)


def _compress(c4, pe, w1, w2p, cos, slo, shi):
    b, n_kv, n_chunks, half = c4.shape
    return pl.pallas_call(
        _compress_kernel,
        grid=(b, n_kv),
        in_specs=[
            pl.BlockSpec((None, None, n_chunks, half), lambda bi, j: (bi, j, 0, 0)),
            pl.BlockSpec((None, 1, 2 * half), lambda bi, j: (j // N_KV_GROUPS, 0, 0)),
            pl.BlockSpec((None, 2 * half, CMP_HIDDEN), lambda bi, j: (j // N_KV_GROUPS, 0, 0)),
            pl.BlockSpec((None, CMP_HIDDEN, LANES), lambda bi, j: (j, 0, 0)),
            pl.BlockSpec((None, n_chunks, LANES), lambda bi, j: (j // N_KV_GROUPS, 0, 0)),
            pl.BlockSpec((None, n_chunks, LANES), lambda bi, j: (j // N_KV_GROUPS, 0, 0)),
            pl.BlockSpec((None, n_chunks, LANES), lambda bi, j: (j // N_KV_GROUPS, 0, 0)),
        ],
        out_specs=[
            pl.BlockSpec((None, None, n_chunks, LANES), lambda bi, j: (bi, j, 0, 0)),
            pl.BlockSpec((None, None, HEAD_DIM, n_chunks), lambda bi, j: (bi, j, 0, 0)),
        ],
        out_shape=[
            jax.ShapeDtypeStruct((b, n_kv, n_chunks, LANES), BF16),
            jax.ShapeDtypeStruct((b, n_kv, HEAD_DIM, n_chunks), BF16),
        ],
        compiler_params=pltpu.CompilerParams(dimension_semantics=("arbitrary", "arbitrary")),
        name="compress",
    )(c4, pe, w1, w2p, cos, slo, shi)


def _block_rank(score):
    n_slc = score.shape[0]
    sub = 8
    ranks = []
    for v in range(n_slc // sub):
        blk = score[v * sub:(v + 1) * sub, :]
        jb_v = v * sub + lax.broadcasted_iota(jnp.int32, blk.shape, 0)
        r = jnp.zeros(blk.shape, F32)
        for jp in range(n_slc):
            row = score[jp:jp + 1, :]
            ge = jnp.where(row >= blk, 1.0, 0.0)
            gt = jnp.where(row > blk, 1.0, 0.0)
            if jp < v * sub:
                r = r + ge
            elif jp >= (v + 1) * sub:
                r = r + gt
            else:
                r = r + jnp.where(jb_v > jp, ge, gt)
        ranks.append(r)
    return jnp.concatenate(ranks, axis=0)


def _attn_kernel(q_ref, gn_ref, kc_ref, vct_ref, ks_ref, vst_ref, kw_ref, vwt_ref, ovt_ref,
                 o_ref, qs_ref, qsel_ref, s_ref, p_ref, al_ref, mw_ref, ms_ref, accw_ref, accs_ref,
                 tot_ref, gt_ref):
    g = pl.program_id(1)
    i = pl.program_id(2)
    s0 = i * TQ
    n_cmp = kc_ref.shape[0]
    n_slc = ovt_ref.shape[0]
    n_ch = M_ATT // CW
    chunk = lambda ch: slice(ch * CW, (ch + 1) * CW)

    for hh in range(HPG):
        qs_ref[:, hh * TQ:(hh + 1) * TQ] = (
            q_ref[0, :, hh * LANES:(hh + 1) * LANES].astype(F32).T.astype(BF16))

    gt_ref[...] = gn_ref[0].T
    g_base = g * HEAD_DIM

    def gate_rows(c, ch):
        rows = [gt_ref[pl.ds(g_base + c * HPG + HPC * ch + par, 1), :] for par in range(HPC)]
        return rows[0] if HPC == 1 else jnp.concatenate(rows, axis=1)

    def two_heads(a):
        return a if HPC == 1 else jnp.concatenate([a] * HPC, axis=1)

    def key_tile(ref, k0, nk):
        return ref[pl.ds(pl.multiple_of(k0, VT_CHUNK), nk), :]

    def value_tile(ref, k0, nk):
        c0 = k0 // VT_CHUNK
        return jnp.concatenate([ref[c0 + c] for c in range(nk // VT_CHUNK)], axis=1)

    imp_parts = []

    def run_step(soft=None, score=None, value=None, cmp_soft=None, cmp_value=False):
        for ch in range(n_ch):
            cs = chunk(ch)
            if soft is not None:
                nk, m_ref, bias, first = soft
                s = s_ref[0:nk, cs]
                if bias is not None:
                    s = s + bias
                mx = jnp.max(s, axis=0, keepdims=True)
                if first:
                    m_new, al_new = mx, None
                else:
                    m_prev = m_ref[:, cs]
                    m_new = jnp.maximum(m_prev, mx)
                    al_new = jnp.exp2(m_prev - m_new)
                p_new = jnp.exp2(s - m_new).astype(BF16)
            if cmp_soft is not None:
                s = s_ref[0:n_cmp, cs] + cmp_soft
                mx = jnp.max(s, axis=0, keepdims=True)
                p = jnp.exp2(s - mx)
                lsum = jnp.sum(p, axis=0, keepdims=True)
                pn_new = (p * jnp.where(mx > 0.5 * NEG_INF, 1.0 / lsum, 0.0)).astype(BF16)
            if score is not None:
                kt, q_t_ref = score
                s_ref[0:kt.shape[0], cs] = _dot(kt, q_t_ref[:, cs])
            if value is not None:
                vt, acc_ref, vfirst = value
                pv = _dot(vt, p_ref[0:vt.shape[1], cs])
                acc_ref[:, cs] = pv if vfirst else al_ref[:, cs] * acc_ref[:, cs] + pv
            if cmp_value:
                pn = p_ref[0:n_cmp, cs]
                tot_ref[:, cs] = _dot(vct_ref[...], pn) * gate_rows(0, ch)
                for par in range(HPC):
                    imp_parts.append(_dot(ovt_ref[...], pn[:, par * TQ:(par + 1) * TQ]))
            if soft is not None:
                p_ref[0:nk, cs] = p_new
                m_ref[:, cs] = m_new
                if al_new is not None:
                    al_ref[:, cs] = al_new
            if cmp_soft is not None:
                p_ref[0:n_cmp, cs] = pn_new

    n_idx = lax.broadcasted_iota(jnp.int32, (n_cmp, TQ), 0)
    t_cmp = s0 + lax.broadcasted_iota(jnp.int32, (n_cmp, TQ), 1)
    cmp_bias = two_heads(jnp.where(n_idx * CMP_STRIDE + CMP_LEN - 1 <= t_cmp, 0.0, NEG_INF))

    w0 = jnp.maximum(s0 - WINDOW, 0)
    win_tiles = [(off, min(NK_SEL, WIN_KEYS - off)) for off in range(0, WIN_KEYS, NK_SEL)]
    n_win = len(win_tiles)

    def win_bias(off, nk):
        kpos = w0 + off + lax.broadcasted_iota(jnp.int32, (nk, TQ), 0)
        t_w = s0 + lax.broadcasted_iota(jnp.int32, (nk, TQ), 1)
        return two_heads(
            jnp.where(kpos <= t_w, jnp.where(kpos > t_w - WINDOW, 0.0, NEG_INF), NEG_INF))

    def win_keys(w):
        return key_tile(kw_ref, w0 + win_tiles[w][0], win_tiles[w][1])

    def win_values(w):
        return value_tile(vwt_ref, w0 + win_tiles[w][0], win_tiles[w][1])

    def win_soft(w, first):
        return (win_tiles[w][1], mw_ref, win_bias(*win_tiles[w]), first)

    def win_step(k, last_score=None):
        args = {}
        if k < n_win:
            args["score"] = (win_keys(k), qs_ref)
        elif k == n_win and last_score is not None:
            args["score"] = last_score
        if 1 <= k <= n_win:
            args["soft"] = win_soft(k - 1, k == 1)
        if 2 <= k <= n_win + 1:
            args["value"] = (win_values(k - 2), accw_ref, k == 2)
        return args

    run_step(score=(kc_ref[...], qs_ref))
    run_step(cmp_soft=cmp_bias, **win_step(0))
    run_step(cmp_value=True, **win_step(1))
    for k in range(2, n_win):
        run_step(**win_step(k))

    imp = functools.reduce(lambda a, b: a + b, imp_parts)
    jb = lax.broadcasted_iota(jnp.int32, (n_slc, TQ), 0)
    tq = s0 + lax.broadcasted_iota(jnp.int32, (n_slc, TQ), 1)
    causal = jb * SEL_BLOCK <= tq
    near = jnp.logical_or(jb == 0, jb >= tq // SEL_BLOCK - 1)
    score = jnp.where(causal, jnp.where(near, SEL_BONUS, imp), NEG_INF)
    rank = jnp.concatenate(
        [_block_rank(score[:, c0:c0 + LANES])
         for c0 in range(0, TQ, LANES)], axis=1)
    picked = rank < float(N_SEL)
    sel_bias = jnp.where(causal, jnp.where(picked, 0.0, NEG_INF), NEG_INF)
    parts = [jnp.zeros((HEAD_DIM, TQ), F32), sel_bias]
    if n_slc < HEAD_DIM:
        parts.append(jnp.zeros((HEAD_DIM - n_slc, TQ), F32))
    sel_rows = jnp.concatenate(parts, axis=0).astype(BF16)
    for hh in range(HPG):
        hs = slice(hh * TQ, (hh + 1) * TQ)
        qsel_ref[:, hs] = qs_ref[:, hs] + sel_rows

    ms_ref[...] = jnp.full(ms_ref.shape, NEG_INF, F32)
    accs_ref[...] = jnp.zeros(accs_ref.shape, F32)
    j_diag = s0 // NK_SEL

    def sel_values(j, live):
        vt = value_tile(vst_ref, jnp.maximum(j, 0) * NK_SEL, NK_SEL)
        return jnp.where(live, vt, jnp.zeros_like(vt))

    run_step(**win_step(n_win, last_score=(key_tile(ks_ref, 0, NK_SEL), qsel_ref)))
    run_step(**win_step(n_win + 1))

    def sel_body(j, carry):
        run_step(soft=(NK_SEL, ms_ref, None, False),
                 score=(key_tile(ks_ref, j * NK_SEL, NK_SEL), qsel_ref),
                 value=(sel_values(j - 2, j >= 2), accs_ref, False))
        return carry

    lax.fori_loop(1, j_diag + 1, sel_body, 0)
    kpos = j_diag * NK_SEL + lax.broadcasted_iota(jnp.int32, (NK_SEL, TQ), 0)
    t_sel = s0 + lax.broadcasted_iota(jnp.int32, (NK_SEL, TQ), 1)
    diag_bias = two_heads(jnp.where(kpos <= t_sel, 0.0, NEG_INF))
    run_step(soft=(NK_SEL, ms_ref, diag_bias, False),
             value=(sel_values(j_diag - 1, j_diag >= 1), accs_ref, False))
    run_step(value=(sel_values(j_diag, True), accs_ref, False))

    heads = []
    for ch in range(n_ch):
        cs = chunk(ch)
        coef_w = gate_rows(2, ch) / accw_ref[HEAD_DIM:HEAD_DIM + 1, cs]
        coef_s = gate_rows(1, ch) / accs_ref[HEAD_DIM:HEAD_DIM + 1, cs]
        tot = (tot_ref[:, cs] + accw_ref[0:HEAD_DIM, cs] * coef_w
               + accs_ref[0:HEAD_DIM, cs] * coef_s)
        heads += [tot[:, par * TQ:(par + 1) * TQ] for par in range(HPC)]
    for pair in range(HPG // 2):
        both = jnp.concatenate(heads[2 * pair:2 * pair + 2], axis=0)
        o_ref[0, :, pair * LANES:(pair + 1) * LANES] = both.T.astype(BF16)


def _attention(q3, gn3, kcmp, vcmp_t, ks4, vst, kw4, vwt, ovt):
    b, seq, _ = q3.shape
    n_cmp = kcmp.shape[2]
    n_slc = seq // SEL_BLOCK
    chunks_per_seq = seq // VT_CHUNK
    per_group = lambda bi, g, i: (bi, g, 0, 0)
    return pl.pallas_call(
        _attn_kernel,
        grid=(b, N_KV_GROUPS, seq // TQ),
        in_specs=[
            pl.BlockSpec((1, TQ, HPG * LANES), lambda bi, g, i: (bi, i, g)),
            pl.BlockSpec((1, TQ, LANES), lambda bi, g, i: (bi, i, 0)),
            pl.BlockSpec((None, None, n_cmp, LANES), per_group),
            pl.BlockSpec((None, None, HEAD_DIM, n_cmp),
                         lambda bi, g, i: (bi, N_KV_GROUPS + g, 0, 0)),
            pl.BlockSpec((None, None, seq, LANES), per_group),
            pl.BlockSpec((None, None, chunks_per_seq, V_ROWS, VT_CHUNK),
                         lambda bi, g, i: (bi, g, 0, 0, 0)),
            pl.BlockSpec((None, None, seq, LANES), per_group),
            pl.BlockSpec((None, None, chunks_per_seq, V_ROWS, VT_CHUNK),
                         lambda bi, g, i: (bi, g, 0, 0, 0)),
            pl.BlockSpec((n_slc, n_cmp), lambda bi, g, i: (0, 0)),
        ],
        out_specs=pl.BlockSpec((1, TQ, HPG * HEAD_DIM), lambda bi, g, i: (bi, i, g)),
        out_shape=jax.ShapeDtypeStruct((b, seq, NSA_WIDTH), BF16),
        scratch_shapes=[
            pltpu.VMEM((LANES, M_ATT), BF16),
            pltpu.VMEM((LANES, M_ATT), BF16),
            pltpu.VMEM((NK_SEL, M_ATT), F32),
            pltpu.VMEM((NK_SEL, M_ATT), BF16),
            pltpu.VMEM((1, M_ATT), F32),
            pltpu.VMEM((1, M_ATT), F32),
            pltpu.VMEM((1, M_ATT), F32),
            pltpu.VMEM((V_ROWS, M_ATT), F32),
            pltpu.VMEM((V_ROWS, M_ATT), F32),
            pltpu.VMEM((HEAD_DIM, M_ATT), F32),
            pltpu.VMEM((LANES, TQ), F32),
        ],
        compiler_params=pltpu.CompilerParams(
            dimension_semantics=("arbitrary", "arbitrary", "arbitrary"),
            vmem_limit_bytes=VMEM_LIMIT),
        name="nsa_attention",
    )(q3, gn3, kcmp, vcmp_t, ks4, vst, kw4, vwt, ovt)


def _mlp_kernel(x_ref, yp_ref, yn_ref, gm_ref, wpp_ref, wpn_ref, wo_ref, nm_ref, w1_ref, w2_ref,
                nf_ref, o_ref, *, final):
    p1 = _dot(yp_ref[...], wpp_ref[...])
    p2 = _dot(yn_ref[...], wpn_ref[...])
    ga = jax.nn.sigmoid(gm_ref[:, 0:D_MODEL])
    gb = jax.nn.sigmoid(gm_ref[:, D_MODEL:2 * D_MODEL])
    merged = ga * p1 + gb * p2
    x = x_ref[...] + _dot(merged.astype(BF16), wo_ref[...])
    h = _rms(x, nm_ref[...]).astype(BF16)
    acc = jnp.zeros((TM_MLP, D_MODEL), F32)
    for c in range(D_FF // FF_CHUNK):
        sl = slice(c * FF_CHUNK, (c + 1) * FF_CHUNK)
        a = jnp.square(jnp.maximum(_dot(h, w1_ref[:, sl]), 0.0)).astype(BF16)
        acc = acc + _dot(a, w2_ref[sl, :])
    x = x + acc
    if final:
        x = _rms(x, nf_ref[...])
    o_ref[...] = x


def _merge_mlp(x2, yp, yn, gm, wpp, wpn, wo, nm, w1, w2, nf, final):
    n = x2.shape[0]
    row = lambda w_: pl.BlockSpec((TM_MLP, w_), lambda i: (i, 0))
    res = lambda a: _resident(a.shape, lambda i: (0,) * a.ndim)
    return pl.pallas_call(
        functools.partial(_mlp_kernel, final=final),
        grid=(n // TM_MLP,),
        in_specs=[row(D_MODEL), row(POOL_WIDTH), row(NSA_WIDTH), row(2 * D_MODEL),
                  res(wpp), res(wpn), res(wo), res(nm), res(w1), res(w2), res(nf)],
        out_specs=row(D_MODEL),
        out_shape=jax.ShapeDtypeStruct((n, D_MODEL), F32),
        compiler_params=pltpu.CompilerParams(
            dimension_semantics=("arbitrary",), vmem_limit_bytes=VMEM_LIMIT),
        name="merge_mlp",
    )(x2, yp, yn, gm, wpp, wpn, wo, nm, w1, w2, nf)


def _rope_tables(pos):
    inv = ROPE_THETA ** (-jnp.arange(0, HEAD_DIM, 2, dtype=F32) / HEAD_DIM)
    ang = pos.astype(F32)[:, None] * inv[None, :]
    ang = jnp.concatenate([ang, ang, ang, ang], axis=-1)
    first_half = (jnp.arange(LANES) % HEAD_DIM) < HEAD_DIM // 2
    cos, sin = jnp.cos(ang), jnp.sin(ang)
    return cos, jnp.where(first_half, -sin, 0.0), jnp.where(first_half, 0.0, sin)


def _permute_w_in(w_in):
    o_q = POOL_WIDTH
    o_kv = o_q + NSA_WIDTH
    o_gn = o_kv + 6 * KV_WIDTH
    o_gm = o_gn + N_GATE
    depth = w_in.shape[0]
    gn = w_in[:, :, o_gn:o_gm].reshape(depth, D_MODEL, N_KV_GROUPS, HPG, 3)
    gn = gn.transpose(0, 1, 2, 4, 3).reshape(depth, D_MODEL, N_KV_GROUPS, 3 * HPG)
    gn = jnp.pad(gn, ((0, 0), (0, 0), (0, 0), (0, LANES // N_KV_GROUPS - 3 * HPG)))
    gn = gn.reshape(depth, D_MODEL, LANES)
    return jnp.concatenate(
        [w_in[:, :, 0:o_gn], w_in[:, :, o_gm:], gn], axis=-1).astype(BF16)


def kernel(x, norm_mix, w_in, w_pool, pool_scale, pe_k, pe_v, w_ck1, w_ck2, w_cv1, w_cv2,
           w_proj_pool, w_proj_nsa, w_out, norm_mlp, w_ff1, w_ff2, norm_final):
    b, seq, d = x.shape
    depth = w_in.shape[0]
    n = b * seq
    n_chunks = seq // CMP_STRIDE
    n_slc = seq // SEL_BLOCK
    assert n_slc <= HEAD_DIM, "the selection one-hot shares the 64 spare key lanes"
    assert seq >= WIN_KEYS and seq % TM_IN == 0

    w_in_p = _permute_w_in(w_in)
    w_pool_b = w_pool.astype(BF16)
    pe = jnp.stack([pe_k, pe_v], axis=1).reshape(depth, 2, 1, CMP_LEN * HEAD_DIM)
    w_c1 = jnp.stack([w_ck1, w_cv1], axis=1).astype(BF16)
    pad = jnp.zeros_like(w_ck2)
    w_c2 = jnp.stack([jnp.concatenate([w, pad], axis=-1) for w in (w_ck2, w_ck2, w_cv2, w_cv2)],
                     axis=1).astype(BF16)
    wpp, wpn, wo = w_proj_pool.astype(BF16), w_proj_nsa.astype(BF16), w_out.astype(BF16)
    w1, w2 = w_ff1.astype(BF16), w_ff2.astype(BF16)

    cos, slo, shi = _rope_tables(jnp.arange(seq))
    ccos, cslo, cshi = _rope_tables(jnp.arange(n_chunks) * CMP_STRIDE + CMP_LEN - 1)
    ident = (jnp.ones_like(ccos), jnp.zeros_like(cslo), jnp.zeros_like(cshi))
    cmp_tabs = [jnp.stack([t, i_], axis=0) for t, i_ in zip((ccos, cslo, cshi), ident)]
    cmp_start = jnp.arange(n_chunks) * CMP_STRIDE
    slc_start = jnp.arange(n_slc) * SEL_BLOCK
    ovt = ((cmp_start[None, :] <= slc_start[:, None] + SEL_BLOCK - 1)
           & (cmp_start[None, :] + CMP_LEN - 1 >= slc_start[:, None])).astype(BF16)

    x2 = x.reshape(n, d)
    for l in range(depth):
        u, q, kcvc, ks, vst, kw, vwt, gm, gn = _inproj(
            x2, norm_mix[l][None, :], w_in_p[l], cos, slo, shi, b, seq)
        y_pool = _pool(u.reshape(b, seq, POOL_WIDTH), w_pool_b[l], pool_scale[l][None, :])
        c4 = kcvc.reshape(b, seq, 2 * N_KV_GROUPS, HEAD_DIM).transpose(0, 2, 1, 3)
        c4 = c4.reshape(b, 2 * N_KV_GROUPS, n_chunks, CMP_STRIDE * HEAD_DIM)
        cmp_n, cmp_t = _compress(c4, pe[l], w_c1[l], w_c2[l], *cmp_tabs)
        y_nsa = _attention(
            q.reshape(b, seq, 2 * NSA_WIDTH), gn.reshape(b, seq, LANES), cmp_n, cmp_t,
            ks, vst, kw, vwt, ovt)
        x2 = _merge_mlp(
            x2, y_pool.reshape(n, POOL_WIDTH), y_nsa.reshape(n, NSA_WIDTH), gm,
            wpp[l], wpn[l], wo[l], norm_mlp[l][None, :], w1[l], w2[l], norm_final[None, :],
            final=(l == depth - 1))
    return x2.reshape(b, seq, d)
```

```python
import functools

import jax
import jax.numpy as jnp
import numpy as np
from jax import lax
from jax.experimental import pallas as pl
from jax.experimental.pallas import tpu as pltpu

F32 = jnp.float32
BF16 = jnp.bfloat16

D_MODEL = 1024
POOL_WINDOWS = (2, 4, 8, 16)
POOL_WIDTH = 512
POOL_GW = 128
N_HEADS = 16
HEAD_DIM = 64
N_KV_GROUPS = 2
HPG = 8
NSA_WIDTH = 1024
KV_WIDTH = 128
CMP_LEN = 32
CMP_STRIDE = 16
CMP_HIDDEN = 256
SEL_BLOCK = 64
N_SEL = 16
WINDOW = 512
SEL_BONUS = 1e4
NEG_INF = -1e30
ROPE_THETA = 10000.0
D_FF = 4096
RMS_EPS = 1e-6
N_GATE = 3 * N_HEADS
Q_SCALE = HEAD_DIM ** -0.5 * float(np.log2(np.e))

LANES = 128
VMEM_LIMIT = 56 * 1024 * 1024

C_U = 0
C_Q = C_U + POOL_WIDTH
C_KV = C_Q + NSA_WIDTH
C_GM = C_KV + 6 * KV_WIDTH
C_GN = C_GM + 2 * D_MODEL
N_INP = C_GN + LANES

TM_IN = 512
TS_POOL = 512
POOL_HALO = 16
TM_MLP = 512
FF_CHUNK = 1024
TQ = 256
M_ATT = HPG * TQ
CW = 256
HPC = CW // TQ
NK_SEL = 256
WIN_KEYS = WINDOW + TQ
VT_CHUNK = 128
V_ROWS = HEAD_DIM + 16


def _dot(a, b):
    return jnp.dot(a, b, preferred_element_type=F32)


def _rms(x, g):
    return x * lax.rsqrt(jnp.mean(x * x, axis=-1, keepdims=True) + RMS_EPS) * g


def _rope(t, cos, sin_lo, sin_hi):
    return t * cos + pltpu.roll(t, LANES - 32, 1) * sin_lo + pltpu.roll(t, 32, 1) * sin_hi


def _resident(shape, index_map):
    return pl.BlockSpec(shape, index_map, pipeline_mode=pl.Buffered(1))


def _value_rows(v_t):
    tail_row = lax.broadcasted_iota(jnp.int32, (V_ROWS - HEAD_DIM, v_t.shape[1]), 0)
    return jnp.concatenate([v_t, jnp.where(tail_row == 0, 1.0, 0.0)], axis=0)


def _inproj_kernel(x_ref, g_ref, w_ref, cos_ref, slo_ref, shi_ref,
                   u_ref, q_ref, kcvc_ref, ks_ref, vst_ref, kw_ref, vwt_ref, gm_ref, gn_ref,
                   *, tiles_per_seq):
    h = _rms(x_ref[...], g_ref[...]).astype(BF16)
    cos, slo, shi = cos_ref[...], slo_ref[...], shi_ref[...]
    lane = lax.broadcasted_iota(jnp.int32, (TM_IN, LANES), 1)
    low = lane < HEAD_DIM
    pos = (pl.program_id(0) % tiles_per_seq) * TM_IN + lax.broadcasted_iota(
        jnp.int32, (TM_IN, LANES), 0)
    block_onehot = jnp.where(lane - HEAD_DIM == pos // SEL_BLOCK, 1.0, 0.0)

    u_ref[...] = _dot(h, w_ref[:, C_U:C_Q])
    q = _dot(h, w_ref[:, C_Q:C_KV])
    for k in range(NSA_WIDTH // LANES):
        qt = _rope(q[:, k * LANES:(k + 1) * LANES], cos, slo, shi) * Q_SCALE
        q_ref[:, (2 * k) * LANES:(2 * k + 1) * LANES] = jnp.where(low, qt, 0.0).astype(BF16)
        q_ref[:, (2 * k + 1) * LANES:(2 * k + 2) * LANES] = jnp.where(
            low, pltpu.roll(qt, HEAD_DIM, 1), 0.0).astype(BF16)
    kv = _dot(h, w_ref[:, C_KV:C_GM])
    kcvc_ref[...] = kv[:, 0:2 * KV_WIDTH]
    ks = _rope(kv[:, 2 * KV_WIDTH:3 * KV_WIDTH], cos, slo, shi)
    kw = _rope(kv[:, 4 * KV_WIDTH:5 * KV_WIDTH], cos, slo, shi)
    vs_t = kv[:, 3 * KV_WIDTH:4 * KV_WIDTH].T
    vw_t = kv[:, 5 * KV_WIDTH:6 * KV_WIDTH].T
    for g in range(N_KV_GROUPS):
        ks_g = ks if g == 0 else pltpu.roll(ks, HEAD_DIM, 1)
        kw_g = kw if g == 0 else pltpu.roll(kw, HEAD_DIM, 1)
        ks_ref[0, g] = jnp.where(low, ks_g, block_onehot).astype(BF16)
        kw_ref[0, g] = jnp.where(low, kw_g, 0.0).astype(BF16)
        vs_g = _value_rows(vs_t[g * HEAD_DIM:(g + 1) * HEAD_DIM, :]).astype(BF16)
        vw_g = _value_rows(vw_t[g * HEAD_DIM:(g + 1) * HEAD_DIM, :]).astype(BF16)
        for c in range(TM_IN // VT_CHUNK):
            sl = slice(c * VT_CHUNK, (c + 1) * VT_CHUNK)
            vst_ref[0, g, c] = vs_g[:, sl]
            vwt_ref[0, g, c] = vw_g[:, sl]
    gm_ref[...] = _dot(h, w_ref[:, C_GM:C_GN])
    gn_ref[...] = jax.nn.sigmoid(_dot(h, w_ref[:, C_GN:N_INP]))


def _inproj(x2, g, w, cos, slo, shi, b, seq):
    n = x2.shape[0]
    tiles_per_seq = seq // TM_IN
    n_chunks = TM_IN // VT_CHUNK
    row = lambda w_: pl.BlockSpec((TM_IN, w_), lambda i: (i, 0))
    tab = pl.BlockSpec((TM_IN, LANES), lambda i: (i % tiles_per_seq, 0))
    kg = pl.BlockSpec((1, N_KV_GROUPS, TM_IN, LANES),
                      lambda i: (i // tiles_per_seq, 0, i % tiles_per_seq, 0))
    vt = pl.BlockSpec((1, N_KV_GROUPS, n_chunks, V_ROWS, VT_CHUNK),
                      lambda i: (i // tiles_per_seq, 0, i % tiles_per_seq, 0, 0))
    k_shape = jax.ShapeDtypeStruct((b, N_KV_GROUPS, seq, LANES), BF16)
    vt_shape = jax.ShapeDtypeStruct((b, N_KV_GROUPS, seq // VT_CHUNK, V_ROWS, VT_CHUNK), BF16)
    return pl.pallas_call(
        functools.partial(_inproj_kernel, tiles_per_seq=tiles_per_seq),
        grid=(n // TM_IN,),
        in_specs=[row(D_MODEL), _resident((1, D_MODEL), lambda i: (0, 0)),
                  _resident((D_MODEL, N_INP), lambda i: (0, 0)), tab, tab, tab],
        out_specs=[row(POOL_WIDTH), row(2 * NSA_WIDTH), row(2 * KV_WIDTH), kg, vt,
                   kg, vt, row(2 * D_MODEL), row(LANES)],
        out_shape=[
            jax.ShapeDtypeStruct((n, POOL_WIDTH), F32),
            jax.ShapeDtypeStruct((n, 2 * NSA_WIDTH), BF16),
            jax.ShapeDtypeStruct((n, 2 * KV_WIDTH), F32),
            k_shape, vt_shape, k_shape, vt_shape,
            jax.ShapeDtypeStruct((n, 2 * D_MODEL), F32),
            jax.ShapeDtypeStruct((n, LANES), F32),
        ],
        compiler_params=pltpu.CompilerParams(
            dimension_semantics=("arbitrary",), vmem_limit_bytes=VMEM_LIMIT),
        name="in_proj",
    )(x2, g, w, cos, slo, shi)


def _pool_kernel(u_ref, up_ref, wp_ref, sc_ref, o_ref):
    i = pl.program_id(1)
    cur = u_ref[0]
    prev = jnp.where(i > 0, up_ref[0], 0.0)
    t = i * TS_POOL + lax.broadcasted_iota(jnp.int32, (TS_POOL, POOL_GW), 0)
    for g, w in enumerate(POOL_WINDOWS):
        sl = slice(g * POOL_GW, (g + 1) * POOL_GW)
        cg = cur[:, sl]
        s = jnp.concatenate([prev[:, sl], cg], axis=0)
        sh = 1
        while sh < w:
            s = s + pltpu.roll(s, sh, 0)
            sh *= 2
        cnt = jnp.minimum(t + 1, w).astype(F32)
        d = s[POOL_HALO:] / cnt - cg
        y = _dot(d.astype(BF16), wp_ref[g]) * sc_ref[:, sl]
        o_ref[0, :, sl] = y.astype(BF16)


def _pool(u3, wp, sc):
    b, seq, _ = u3.shape
    halo_per_tile = TS_POOL // POOL_HALO
    return pl.pallas_call(
        _pool_kernel,
        grid=(b, seq // TS_POOL),
        in_specs=[
            pl.BlockSpec((1, TS_POOL, POOL_WIDTH), lambda bi, i: (bi, i, 0)),
            pl.BlockSpec((1, POOL_HALO, POOL_WIDTH),
                         lambda bi, i: (bi, jnp.maximum(i * halo_per_tile - 1, 0), 0)),
            pl.BlockSpec((len(POOL_WINDOWS), POOL_GW, POOL_GW), lambda bi, i: (0, 0, 0)),
            pl.BlockSpec((1, POOL_WIDTH), lambda bi, i: (0, 0)),
        ],
        out_specs=pl.BlockSpec((1, TS_POOL, POOL_WIDTH), lambda bi, i: (bi, i, 0)),
        out_shape=jax.ShapeDtypeStruct((b, seq, POOL_WIDTH), BF16),
        compiler_params=pltpu.CompilerParams(dimension_semantics=("arbitrary", "arbitrary")),
        name="pool_mixer",
    )(u3, u3, wp, sc)


def _compress_kernel(c_ref, pe_ref, w1_ref, w2_ref, cos_ref, slo_ref, shi_ref, o_ref, ot_ref):
    half = CMP_STRIDE * HEAD_DIM
    c = c_ref[...]
    a = _dot((c + pe_ref[:, 0:half]).astype(BF16), w1_ref[0:half, :])
    b = _dot((c + pe_ref[:, half:2 * half]).astype(BF16), w1_ref[half:2 * half, :])
    n_rows = c.shape[0]
    hid = a + pltpu.roll(b, n_rows - 1, 0)
    act = jax.nn.gelu(hid, approximate=True)
    out = _dot(act.astype(BF16), w2_ref[...])
    out = _rope(out, cos_ref[...], slo_ref[...], shi_ref[...])
    row = lax.broadcasted_iota(jnp.int32, out.shape, 0)
    out = jnp.where(row < n_rows - 1, out, 0.0)
    o_ref[...] = out.astype(BF16)
    ot_ref[...] = _value_rows(out.T[0:HEAD_DIM, :]).astype(BF16)


def _compress(c4, pe, w1, w2p, cos, slo, shi):
    b, n_kv, n_chunks, half = c4.shape
    return pl.pallas_call(
        _compress_kernel,
        grid=(b, n_kv),
        in_specs=[
            pl.BlockSpec((None, None, n_chunks, half), lambda bi, j: (bi, j, 0, 0)),
            pl.BlockSpec((None, 1, 2 * half), lambda bi, j: (j // N_KV_GROUPS, 0, 0)),
            pl.BlockSpec((None, 2 * half, CMP_HIDDEN), lambda bi, j: (j // N_KV_GROUPS, 0, 0)),
            pl.BlockSpec((None, CMP_HIDDEN, LANES), lambda bi, j: (j, 0, 0)),
            pl.BlockSpec((None, n_chunks, LANES), lambda bi, j: (j // N_KV_GROUPS, 0, 0)),
            pl.BlockSpec((None, n_chunks, LANES), lambda bi, j: (j // N_KV_GROUPS, 0, 0)),
            pl.BlockSpec((None, n_chunks, LANES), lambda bi, j: (j // N_KV_GROUPS, 0, 0)),
        ],
        out_specs=[
            pl.BlockSpec((None, None, n_chunks, LANES), lambda bi, j: (bi, j, 0, 0)),
            pl.BlockSpec((None, None, V_ROWS, n_chunks), lambda bi, j: (bi, j, 0, 0)),
        ],
        out_shape=[
            jax.ShapeDtypeStruct((b, n_kv, n_chunks, LANES), BF16),
            jax.ShapeDtypeStruct((b, n_kv, V_ROWS, n_chunks), BF16),
        ],
        compiler_params=pltpu.CompilerParams(dimension_semantics=("arbitrary", "arbitrary")),
        name="compress",
    )(c4, pe, w1, w2p, cos, slo, shi)


def _block_rank(score):
    n_slc = score.shape[0]
    sub = 8
    ranks = []
    for v in range(n_slc // sub):
        blk = score[v * sub:(v + 1) * sub, :]
        jb_v = v * sub + lax.broadcasted_iota(jnp.int32, blk.shape, 0)
        r = jnp.zeros(blk.shape, F32)
        for jp in range(n_slc):
            row = score[jp:jp + 1, :]
            ge = jnp.where(row >= blk, 1.0, 0.0)
            gt = jnp.where(row > blk, 1.0, 0.0)
            if jp < v * sub:
                r = r + ge
            elif jp >= (v + 1) * sub:
                r = r + gt
            else:
                r = r + jnp.where(jb_v > jp, ge, gt)
        ranks.append(r)
    return jnp.concatenate(ranks, axis=0)


def _attn_kernel(q_ref, gn_ref, kc_ref, vct_ref, ks_ref, vst_ref, kw_ref, vwt_ref, ovt_ref,
                 o_ref, qs_ref, qsel_ref, s_ref, p_ref, al_ref, mc_ref, mw_ref, ms_ref,
                 accc_ref, accw_ref, accs_ref, gt_ref):
    g = pl.program_id(1)
    i = pl.program_id(2)
    s0 = i * TQ
    n_cmp = kc_ref.shape[0]
    n_slc = ovt_ref.shape[0]
    n_ch = M_ATT // CW
    chunk = lambda ch: slice(ch * CW, (ch + 1) * CW)

    for hh in range(HPG):
        qs_ref[:, hh * TQ:(hh + 1) * TQ] = (
            q_ref[0, :, hh * LANES:(hh + 1) * LANES].astype(F32).T.astype(BF16))

    gt_ref[...] = gn_ref[0].T
    g_base = g * HEAD_DIM

    def gate_rows(c, ch):
        rows = [gt_ref[pl.ds(g_base + c * HPG + HPC * ch + par, 1), :] for par in range(HPC)]
        return rows[0] if HPC == 1 else jnp.concatenate(rows, axis=1)

    def per_head(a):
        return a if HPC == 1 else jnp.concatenate([a] * HPC, axis=1)

    def key_tile(ref, k0, nk):
        return ref[pl.ds(pl.multiple_of(k0, VT_CHUNK), nk), :]

    def value_tile(ref, k0, nk):
        c0 = k0 // VT_CHUNK
        return jnp.concatenate([ref[c0 + c] for c in range(nk // VT_CHUNK)], axis=1)

    imp_raw = []

    def run_step(soft=None, score=None, value=None, importance=False):
        for ch in range(n_ch):
            cs = chunk(ch)
            if soft is not None:
                nk, m_ref, sbias, first = soft
                s = s_ref[0:nk, cs]
                if sbias is not None:
                    s = s + sbias
                mx = jnp.max(s, axis=0, keepdims=True)
                if first:
                    m_new, al_new = mx, None
                else:
                    m_prev = m_ref[:, cs]
                    m_new = jnp.maximum(m_prev, mx)
                    al_new = jnp.exp2(m_prev - m_new)
                p_new = jnp.exp2(s - m_new).astype(BF16)
            if score is not None:
                kt, q_t_ref, bias = score
                sc = _dot(kt, q_t_ref[:, cs])
                s_ref[0:kt.shape[0], cs] = sc if bias is None else sc + bias
            if value is not None:
                vt, acc_ref, vfirst = value
                pv = _dot(vt, p_ref[0:vt.shape[1], cs])
                acc_ref[:, cs] = pv if vfirst else al_ref[:, cs] * acc_ref[:, cs] + pv
            if importance:
                for par in range(HPC):
                    imp_raw.append(_dot(ovt_ref[...], p_ref[0:n_cmp, ch * CW + par * TQ:
                                                            ch * CW + (par + 1) * TQ]))
            if soft is not None:
                p_ref[0:nk, cs] = p_new
                m_ref[:, cs] = m_new
                if al_new is not None:
                    al_ref[:, cs] = al_new

    n_idx = lax.broadcasted_iota(jnp.int32, (n_cmp, TQ), 0)
    t_cmp = s0 + lax.broadcasted_iota(jnp.int32, (n_cmp, TQ), 1)
    cmp_bias = per_head(jnp.where(n_idx * CMP_STRIDE + CMP_LEN - 1 <= t_cmp, 0.0, NEG_INF))

    w0 = jnp.maximum(s0 - WINDOW, 0)
    win_tiles = [(off, min(NK_SEL, WIN_KEYS - off)) for off in range(0, WIN_KEYS, NK_SEL)]
    n_win = len(win_tiles)

    def win_bias(off, nk):
        kpos = w0 + off + lax.broadcasted_iota(jnp.int32, (nk, TQ), 0)
        t_w = s0 + lax.broadcasted_iota(jnp.int32, (nk, TQ), 1)
        return per_head(
            jnp.where(kpos <= t_w, jnp.where(kpos > t_w - WINDOW, 0.0, NEG_INF), NEG_INF))

    def win_step(k, last_score=None):
        args = {}
        if k < n_win:
            off, nk = win_tiles[k]
            args["score"] = (key_tile(kw_ref, w0 + off, nk), qs_ref, win_bias(off, nk))
        elif k == n_win and last_score is not None:
            args["score"] = last_score
        if 1 <= k <= n_win:
            args["soft"] = (win_tiles[k - 1][1], mw_ref, None, k == 1)
        if 2 <= k <= n_win + 1:
            off, nk = win_tiles[k - 2]
            args["value"] = (value_tile(vwt_ref, w0 + off, nk), accw_ref, k == 2)
        return args

    run_step(score=(kc_ref[...], qs_ref, cmp_bias))
    run_step(soft=(n_cmp, mc_ref, None, True), **win_step(0))
    run_step(value=(vct_ref[...], accc_ref, True), importance=True, **win_step(1))
    for k in range(2, n_win):
        run_step(**win_step(k))

    cmp_scale = jnp.where(mc_ref[...] > 0.5 * NEG_INF,
                          1.0 / accc_ref[HEAD_DIM:HEAD_DIM + 1, :], 0.0)

    imp = functools.reduce(
        lambda a, b: a + b,
        [r * cmp_scale[:, hh * TQ:(hh + 1) * TQ] for hh, r in enumerate(imp_raw)])
    jb = lax.broadcasted_iota(jnp.int32, (n_slc, TQ), 0)
    tq = s0 + lax.broadcasted_iota(jnp.int32, (n_slc, TQ), 1)
    causal = jb * SEL_BLOCK <= tq
    near = jnp.logical_or(jb == 0, jb >= tq // SEL_BLOCK - 1)
    score = jnp.where(causal, jnp.where(near, SEL_BONUS, imp), NEG_INF)
    rank = jnp.concatenate(
        [_block_rank(score[:, c0:c0 + LANES]) for c0 in range(0, TQ, LANES)], axis=1)
    sel_bias = jnp.where(causal, jnp.where(rank < float(N_SEL), 0.0, NEG_INF), NEG_INF)
    parts = [jnp.zeros((HEAD_DIM, TQ), F32), sel_bias]
    if n_slc < HEAD_DIM:
        parts.append(jnp.zeros((HEAD_DIM - n_slc, TQ), F32))
    sel_rows = jnp.concatenate(parts, axis=0).astype(BF16)
    for hh in range(HPG):
        hs = slice(hh * TQ, (hh + 1) * TQ)
        qsel_ref[:, hs] = qs_ref[:, hs] + sel_rows

    ms_ref[...] = jnp.full(ms_ref.shape, NEG_INF, F32)
    accs_ref[...] = jnp.zeros(accs_ref.shape, F32)
    j_diag = s0 // NK_SEL

    def sel_scores(j):
        return (key_tile(ks_ref, j * NK_SEL, NK_SEL), qsel_ref, None)

    def sel_values(j, live):
        vt = value_tile(vst_ref, jnp.maximum(j, 0) * NK_SEL, NK_SEL)
        return jnp.where(live, vt, jnp.zeros_like(vt))

    run_step(**win_step(n_win, last_score=sel_scores(0)))
    run_step(**win_step(n_win + 1))

    def sel_body(j, carry):
        run_step(soft=(NK_SEL, ms_ref, None, False), score=sel_scores(j),
                 value=(sel_values(j - 2, j >= 2), accs_ref, False))
        return carry

    lax.fori_loop(1, j_diag + 1, sel_body, 0)
    kpos = j_diag * NK_SEL + lax.broadcasted_iota(jnp.int32, (NK_SEL, TQ), 0)
    t_sel = s0 + lax.broadcasted_iota(jnp.int32, (NK_SEL, TQ), 1)
    diag_bias = per_head(jnp.where(kpos <= t_sel, 0.0, NEG_INF))
    run_step(soft=(NK_SEL, ms_ref, diag_bias, False),
             value=(sel_values(j_diag - 1, j_diag >= 1), accs_ref, False))
    run_step(value=(sel_values(j_diag, True), accs_ref, False))

    heads = []
    for ch in range(n_ch):
        cs = chunk(ch)
        coef_c = gate_rows(0, ch) * cmp_scale[:, cs]
        coef_w = gate_rows(2, ch) / accw_ref[HEAD_DIM:HEAD_DIM + 1, cs]
        coef_s = gate_rows(1, ch) / accs_ref[HEAD_DIM:HEAD_DIM + 1, cs]
        tot = (accc_ref[0:HEAD_DIM, cs] * coef_c + accw_ref[0:HEAD_DIM, cs] * coef_w
               + accs_ref[0:HEAD_DIM, cs] * coef_s)
        heads += [tot[:, par * TQ:(par + 1) * TQ] for par in range(HPC)]
    for pair in range(HPG // 2):
        both = jnp.concatenate(heads[2 * pair:2 * pair + 2], axis=0)
        o_ref[0, :, pair * LANES:(pair + 1) * LANES] = both.T.astype(BF16)


def _attention(q3, gn3, kcmp, vcmp_t, ks4, vst, kw4, vwt, ovt):
    b, seq, _ = q3.shape
    n_cmp = kcmp.shape[2]
    n_slc = seq // SEL_BLOCK
    chunks_per_seq = seq // VT_CHUNK
    per_group = lambda bi, g, i: (bi, g, 0, 0)
    return pl.pallas_call(
        _attn_kernel,
        grid=(b, N_KV_GROUPS, seq // TQ),
        in_specs=[
            pl.BlockSpec((1, TQ, HPG * LANES), lambda bi, g, i: (bi, i, g)),
            pl.BlockSpec((1, TQ, LANES), lambda bi, g, i: (bi, i, 0)),
            pl.BlockSpec((None, None, n_cmp, LANES), per_group),
            pl.BlockSpec((None, None, V_ROWS, n_cmp),
                         lambda bi, g, i: (bi, N_KV_GROUPS + g, 0, 0)),
            pl.BlockSpec((None, None, seq, LANES), per_group),
            pl.BlockSpec((None, None, chunks_per_seq, V_ROWS, VT_CHUNK),
                         lambda bi, g, i: (bi, g, 0, 0, 0)),
            pl.BlockSpec((None, None, seq, LANES), per_group),
            pl.BlockSpec((None, None, chunks_per_seq, V_ROWS, VT_CHUNK),
                         lambda bi, g, i: (bi, g, 0, 0, 0)),
            pl.BlockSpec((n_slc, n_cmp), lambda bi, g, i: (0, 0)),
        ],
        out_specs=pl.BlockSpec((1, TQ, HPG * HEAD_DIM), lambda bi, g, i: (bi, i, g)),
        out_shape=jax.ShapeDtypeStruct((b, seq, NSA_WIDTH), BF16),
        scratch_shapes=[
            pltpu.VMEM((LANES, M_ATT), BF16),
            pltpu.VMEM((LANES, M_ATT), BF16),
            pltpu.VMEM((NK_SEL, M_ATT), F32),
            pltpu.VMEM((NK_SEL, M_ATT), BF16),
            pltpu.VMEM((1, M_ATT), F32),
            pltpu.VMEM((1, M_ATT), F32),
            pltpu.VMEM((1, M_ATT), F32),
            pltpu.VMEM((1, M_ATT), F32),
            pltpu.VMEM((V_ROWS, M_ATT), F32),
            pltpu.VMEM((V_ROWS, M_ATT), F32),
            pltpu.VMEM((V_ROWS, M_ATT), F32),
            pltpu.VMEM((LANES, TQ), F32),
        ],
        compiler_params=pltpu.CompilerParams(
            dimension_semantics=("arbitrary", "arbitrary", "arbitrary"),
            vmem_limit_bytes=VMEM_LIMIT),
        name="nsa_attention",
    )(q3, gn3, kcmp, vcmp_t, ks4, vst, kw4, vwt, ovt)


def _mlp_kernel(x_ref, yp_ref, yn_ref, gm_ref, wpp_ref, wpn_ref, wo_ref, nm_ref, w1_ref, w2_ref,
                nf_ref, o_ref, *, final):
    p1 = _dot(yp_ref[...], wpp_ref[...])
    p2 = _dot(yn_ref[...], wpn_ref[...])
    ga = jax.nn.sigmoid(gm_ref[:, 0:D_MODEL])
    gb = jax.nn.sigmoid(gm_ref[:, D_MODEL:2 * D_MODEL])
    merged = ga * p1 + gb * p2
    x = x_ref[...] + _dot(merged.astype(BF16), wo_ref[...])
    h = _rms(x, nm_ref[...]).astype(BF16)
    acc = jnp.zeros((TM_MLP, D_MODEL), F32)
    for c in range(D_FF // FF_CHUNK):
        sl = slice(c * FF_CHUNK, (c + 1) * FF_CHUNK)
        a = jnp.square(jnp.maximum(_dot(h, w1_ref[:, sl]), 0.0)).astype(BF16)
        acc = acc + _dot(a, w2_ref[sl, :])
    x = x + acc
    if final:
        x = _rms(x, nf_ref[...])
    o_ref[...] = x


def _merge_mlp(x2, yp, yn, gm, wpp, wpn, wo, nm, w1, w2, nf, final):
    n = x2.shape[0]
    row = lambda w_: pl.BlockSpec((TM_MLP, w_), lambda i: (i, 0))
    res = lambda a: _resident(a.shape, lambda i: (0,) * a.ndim)
    return pl.pallas_call(
        functools.partial(_mlp_kernel, final=final),
        grid=(n // TM_MLP,),
        in_specs=[row(D_MODEL), row(POOL_WIDTH), row(NSA_WIDTH), row(2 * D_MODEL),
                  res(wpp), res(wpn), res(wo), res(nm), res(w1), res(w2), res(nf)],
        out_specs=row(D_MODEL),
        out_shape=jax.ShapeDtypeStruct((n, D_MODEL), F32),
        compiler_params=pltpu.CompilerParams(
            dimension_semantics=("arbitrary",), vmem_limit_bytes=VMEM_LIMIT),
        name="merge_mlp",
    )(x2, yp, yn, gm, wpp, wpn, wo, nm, w1, w2, nf)


def _rope_tables(pos):
    inv = ROPE_THETA ** (-jnp.arange(0, HEAD_DIM, 2, dtype=F32) / HEAD_DIM)
    ang = pos.astype(F32)[:, None] * inv[None, :]
    ang = jnp.concatenate([ang, ang, ang, ang], axis=-1)
    first_half = (jnp.arange(LANES) % HEAD_DIM) < HEAD_DIM // 2
    cos, sin = jnp.cos(ang), jnp.sin(ang)
    return cos, jnp.where(first_half, -sin, 0.0), jnp.where(first_half, 0.0, sin)


def _permute_w_in(w_in):
    o_q = POOL_WIDTH
    o_kv = o_q + NSA_WIDTH
    o_gn = o_kv + 6 * KV_WIDTH
    o_gm = o_gn + N_GATE
    depth = w_in.shape[0]
    gn = w_in[:, :, o_gn:o_gm].reshape(depth, D_MODEL, N_KV_GROUPS, HPG, 3)
    gn = gn.transpose(0, 1, 2, 4, 3).reshape(depth, D_MODEL, N_KV_GROUPS, 3 * HPG)
    gn = jnp.pad(gn, ((0, 0), (0, 0), (0, 0), (0, LANES // N_KV_GROUPS - 3 * HPG)))
    gn = gn.reshape(depth, D_MODEL, LANES)
    return jnp.concatenate(
        [w_in[:, :, 0:o_gn], w_in[:, :, o_gm:], gn], axis=-1).astype(BF16)


def kernel(x, norm_mix, w_in, w_pool, pool_scale, pe_k, pe_v, w_ck1, w_ck2, w_cv1, w_cv2,
           w_proj_pool, w_proj_nsa, w_out, norm_mlp, w_ff1, w_ff2, norm_final):
    b, seq, d = x.shape
    depth = w_in.shape[0]
    n = b * seq
    n_chunks = seq // CMP_STRIDE
    n_slc = seq // SEL_BLOCK
    assert n_slc <= HEAD_DIM, "the selection one-hot shares the 64 spare key lanes"
    assert seq >= WIN_KEYS and seq % TM_IN == 0

    w_in_p = _permute_w_in(w_in)
    w_pool_b = w_pool.astype(BF16)
    pe = jnp.stack([pe_k, pe_v], axis=1).reshape(depth, 2, 1, CMP_LEN * HEAD_DIM)
    w_c1 = jnp.stack([w_ck1, w_cv1], axis=1).astype(BF16)
    pad = jnp.zeros_like(w_ck2)
    w_c2 = jnp.stack([jnp.concatenate([w, pad], axis=-1) for w in (w_ck2, w_ck2, w_cv2, w_cv2)],
                     axis=1).astype(BF16)
    wpp, wpn, wo = w_proj_pool.astype(BF16), w_proj_nsa.astype(BF16), w_out.astype(BF16)
    w1, w2 = w_ff1.astype(BF16), w_ff2.astype(BF16)

    cos, slo, shi = _rope_tables(jnp.arange(seq))
    ccos, cslo, cshi = _rope_tables(jnp.arange(n_chunks) * CMP_STRIDE + CMP_LEN - 1)
    ident = (jnp.ones_like(ccos), jnp.zeros_like(cslo), jnp.zeros_like(cshi))
    cmp_tabs = [jnp.stack([t, i_], axis=0) for t, i_ in zip((ccos, cslo, cshi), ident)]
    cmp_start = jnp.arange(n_chunks) * CMP_STRIDE
    slc_start = jnp.arange(n_slc) * SEL_BLOCK
    ovt = ((cmp_start[None, :] <= slc_start[:, None] + SEL_BLOCK - 1)
           & (cmp_start[None, :] + CMP_LEN - 1 >= slc_start[:, None])).astype(BF16)

    x2 = x.reshape(n, d)
    for l in range(depth):
        u, q, kcvc, ks, vst, kw, vwt, gm, gn = _inproj(
            x2, norm_mix[l][None, :], w_in_p[l], cos, slo, shi, b, seq)
        y_pool = _pool(u.reshape(b, seq, POOL_WIDTH), w_pool_b[l], pool_scale[l][None, :])
        c4 = kcvc.reshape(b, seq, 2 * N_KV_GROUPS, HEAD_DIM).transpose(0, 2, 1, 3)
        c4 = c4.reshape(b, 2 * N_KV_GROUPS, n_chunks, CMP_STRIDE * HEAD_DIM)
        cmp_n, cmp_t = _compress(c4, pe[l], w_c1[l], w_c2[l], *cmp_tabs)
        y_nsa = _attention(
            q.reshape(b, seq, 2 * NSA_WIDTH), gn.reshape(b, seq, LANES), cmp_n, cmp_t,
            ks, vst, kw, vwt, ovt)
        x2 = _merge_mlp(
            x2, y_pool.reshape(n, POOL_WIDTH), y_nsa.reshape(n, NSA_WIDTH), gm,
            wpp[l], wpn[l], wo[l], norm_mlp[l][None, :], w1[l], w2[l], norm_final[None, :],
            final=(l == depth - 1))
    return x2.reshape(b, seq, d)
```

```python
import functools

import jax
import jax.numpy as jnp
import numpy as np
from jax import lax
from jax.experimental import pallas as pl
from jax.experimental.pallas import tpu as pltpu

F32 = jnp.float32
BF16 = jnp.bfloat16

D_MODEL = 1024
POOL_WINDOWS = (2, 4, 8, 16)
POOL_WIDTH = 512
POOL_GW = 128
N_HEADS = 16
HEAD_DIM = 64
N_KV_GROUPS = 2
HPG = 8
NSA_WIDTH = 1024
KV_WIDTH = 128
CMP_LEN = 32
CMP_STRIDE = 16
CMP_HIDDEN = 256
SEL_BLOCK = 64
N_SEL = 16
WINDOW = 512
SEL_BONUS = 1e4
NEG_INF = -1e30
ROPE_THETA = 10000.0
D_FF = 4096
RMS_EPS = 1e-6
N_GATE = 3 * N_HEADS
Q_SCALE = HEAD_DIM ** -0.5 * float(np.log2(np.e))

LANES = 128
VMEM_LIMIT = 56 * 1024 * 1024

C_U = 0
C_Q = C_U + POOL_WIDTH
C_KV = C_Q + NSA_WIDTH
C_GM = C_KV + 6 * KV_WIDTH
C_GN = C_GM + 2 * D_MODEL
N_INP = C_GN + LANES

TM_IN = 512
TS_POOL = 512
POOL_HALO = 16
TM_MLP = 512
FF_CHUNK = 1024
TQ = 256
M_ATT = HPG * TQ
CW = 256
HPC = CW // TQ
NK_SEL = 256
WIN_KEYS = WINDOW + TQ
VT_CHUNK = 128
V_ROWS = HEAD_DIM + 16


def _dot(a, b):
    return jnp.dot(a, b, preferred_element_type=F32)


def _rms(x, g):
    return x * lax.rsqrt(jnp.mean(x * x, axis=-1, keepdims=True) + RMS_EPS) * g


def _rope(t, cos, sin_lo, sin_hi):
    return t * cos + pltpu.roll(t, LANES - 32, 1) * sin_lo + pltpu.roll(t, 32, 1) * sin_hi


def _resident(shape, index_map):
    return pl.BlockSpec(shape, index_map, pipeline_mode=pl.Buffered(1))


def _layer_resident(stacked, layer):
    nd = stacked.ndim - 1
    return _resident((None,) + stacked.shape[1:], lambda i: (layer,) + (0,) * nd)


def _value_rows(v_t):
    tail_row = lax.broadcasted_iota(jnp.int32, (V_ROWS - HEAD_DIM, v_t.shape[1]), 0)
    return jnp.concatenate([v_t, jnp.where(tail_row == 0, 1.0, 0.0)], axis=0)


def _inproj_kernel(x_ref, g_ref, w_ref, cos_ref, slo_ref, shi_ref,
                   u_ref, q_ref, cmp_ref, ks_ref, vst_ref, kw_ref, vwt_ref, gm_ref, gn_ref,
                   kv_scr, *, tiles_per_seq):
    h = _rms(x_ref[...], g_ref[...]).astype(BF16)
    cos, slo, shi = cos_ref[...], slo_ref[...], shi_ref[...]
    lane = lax.broadcasted_iota(jnp.int32, (TM_IN, LANES), 1)
    low = lane < HEAD_DIM
    pos = (pl.program_id(0) % tiles_per_seq) * TM_IN + lax.broadcasted_iota(
        jnp.int32, (TM_IN, LANES), 0)
    block_onehot = jnp.where(lane - HEAD_DIM == pos // SEL_BLOCK, 1.0, 0.0)

    u_ref[...] = _dot(h, w_ref[:, C_U:C_Q])
    q = _dot(h, w_ref[:, C_Q:C_KV])
    for k in range(NSA_WIDTH // LANES):
        qt = _rope(q[:, k * LANES:(k + 1) * LANES], cos, slo, shi) * Q_SCALE
        q_ref[:, (2 * k) * LANES:(2 * k + 1) * LANES] = jnp.where(low, qt, 0.0).astype(BF16)
        q_ref[:, (2 * k + 1) * LANES:(2 * k + 2) * LANES] = jnp.where(
            low, pltpu.roll(qt, HEAD_DIM, 1), 0.0).astype(BF16)
    kv = _dot(h, w_ref[:, C_KV:C_GM])
    n_rows = TM_IN // CMP_STRIDE
    low_c = lax.broadcasted_iota(jnp.int32, (n_rows, LANES), 1) < HEAD_DIM
    for t in range(2):
        kv_scr[t] = kv[:, t * LANES:(t + 1) * LANES]
        for pp in range(CMP_STRIDE // 2):
            a = kv_scr[t, pl.ds(2 * pp, n_rows, stride=CMP_STRIDE), :]
            b = kv_scr[t, pl.ds(2 * pp + 1, n_rows, stride=CMP_STRIDE), :]
            sl = slice(pp * LANES, (pp + 1) * LANES)
            cmp_ref[0, 2 * t, :, sl] = jnp.where(low_c, a, pltpu.roll(b, HEAD_DIM, 1))
            cmp_ref[0, 2 * t + 1, :, sl] = jnp.where(low_c, pltpu.roll(a, HEAD_DIM, 1), b)
    ks = _rope(kv[:, 2 * KV_WIDTH:3 * KV_WIDTH], cos, slo, shi)
    kw = _rope(kv[:, 4 * KV_WIDTH:5 * KV_WIDTH], cos, slo, shi)
    vs_t = kv[:, 3 * KV_WIDTH:4 * KV_WIDTH].T
    vw_t = kv[:, 5 * KV_WIDTH:6 * KV_WIDTH].T
    for g in range(N_KV_GROUPS):
        ks_g = ks if g == 0 else pltpu.roll(ks, HEAD_DIM, 1)
        kw_g = kw if g == 0 else pltpu.roll(kw, HEAD_DIM, 1)
        ks_ref[0, g] = jnp.where(low, ks_g, block_onehot).astype(BF16)
        kw_ref[0, g] = jnp.where(low, kw_g, 0.0).astype(BF16)
        vs_g = _value_rows(vs_t[g * HEAD_DIM:(g + 1) * HEAD_DIM, :]).astype(BF16)
        vw_g = _value_rows(vw_t[g * HEAD_DIM:(g + 1) * HEAD_DIM, :]).astype(BF16)
        for c in range(TM_IN // VT_CHUNK):
            sl = slice(c * VT_CHUNK, (c + 1) * VT_CHUNK)
            vst_ref[0, g, c] = vs_g[:, sl]
            vwt_ref[0, g, c] = vw_g[:, sl]
    gm_ref[...] = _dot(h, w_ref[:, C_GM:C_GN])
    gn_ref[...] = jax.nn.sigmoid(_dot(h, w_ref[:, C_GN:N_INP]))


def _inproj(x2, g, w, layer, cos, slo, shi, b, seq):
    n = x2.shape[0]
    tiles_per_seq = seq // TM_IN
    n_chunks = TM_IN // VT_CHUNK
    row = lambda w_: pl.BlockSpec((TM_IN, w_), lambda i: (i, 0))
    tab = pl.BlockSpec((TM_IN, LANES), lambda i: (i % tiles_per_seq, 0))
    kg = pl.BlockSpec((1, N_KV_GROUPS, TM_IN, LANES),
                      lambda i: (i // tiles_per_seq, 0, i % tiles_per_seq, 0))
    vt = pl.BlockSpec((1, N_KV_GROUPS, n_chunks, V_ROWS, VT_CHUNK),
                      lambda i: (i // tiles_per_seq, 0, i % tiles_per_seq, 0, 0))
    cmp_spec = pl.BlockSpec(
        (1, 2 * N_KV_GROUPS, TM_IN // CMP_STRIDE, CMP_STRIDE * HEAD_DIM),
        lambda i: (i // tiles_per_seq, 0, i % tiles_per_seq, 0))
    k_shape = jax.ShapeDtypeStruct((b, N_KV_GROUPS, seq, LANES), BF16)
    vt_shape = jax.ShapeDtypeStruct((b, N_KV_GROUPS, seq // VT_CHUNK, V_ROWS, VT_CHUNK), BF16)
    return pl.pallas_call(
        functools.partial(_inproj_kernel, tiles_per_seq=tiles_per_seq),
        grid=(n // TM_IN,),
        in_specs=[row(D_MODEL), _resident((1, D_MODEL), lambda i: (0, 0)),
                  _layer_resident(w, layer), tab, tab, tab],
        out_specs=[row(POOL_WIDTH), row(2 * NSA_WIDTH), cmp_spec, kg, vt,
                   kg, vt, row(2 * D_MODEL), row(LANES)],
        out_shape=[
            jax.ShapeDtypeStruct((n, POOL_WIDTH), F32),
            jax.ShapeDtypeStruct((n, 2 * NSA_WIDTH), BF16),
            jax.ShapeDtypeStruct(
                (b, 2 * N_KV_GROUPS, seq // CMP_STRIDE, CMP_STRIDE * HEAD_DIM), F32),
            k_shape, vt_shape, k_shape, vt_shape,
            jax.ShapeDtypeStruct((n, 2 * D_MODEL), F32),
            jax.ShapeDtypeStruct((n, LANES), F32),
        ],
        scratch_shapes=[pltpu.VMEM((2, TM_IN, LANES), F32)],
        compiler_params=pltpu.CompilerParams(
            dimension_semantics=("arbitrary",), vmem_limit_bytes=VMEM_LIMIT),
        name="in_proj",
    )(x2, g, w, cos, slo, shi)


def _pool_kernel(u_ref, up_ref, wp_ref, sc_ref, o_ref):
    i = pl.program_id(1)
    cur = u_ref[0]
    prev = jnp.where(i > 0, up_ref[0], 0.0)
    t = i * TS_POOL + lax.broadcasted_iota(jnp.int32, (TS_POOL, POOL_GW), 0)
    for g, w in enumerate(POOL_WINDOWS):
        sl = slice(g * POOL_GW, (g + 1) * POOL_GW)
        cg = cur[:, sl]
        s = jnp.concatenate([prev[:, sl], cg], axis=0)
        sh = 1
        while sh < w:
            s = s + pltpu.roll(s, sh, 0)
            sh *= 2
        cnt = jnp.minimum(t + 1, w).astype(F32)
        d = s[POOL_HALO:] / cnt - cg
        y = _dot(d.astype(BF16), wp_ref[g]) * sc_ref[:, sl]
        o_ref[0, :, sl] = y.astype(BF16)


def _pool(u3, wp, sc):
    b, seq, _ = u3.shape
    halo_per_tile = TS_POOL // POOL_HALO
    return pl.pallas_call(
        _pool_kernel,
        grid=(b, seq // TS_POOL),
        in_specs=[
            pl.BlockSpec((1, TS_POOL, POOL_WIDTH), lambda bi, i: (bi, i, 0)),
            pl.BlockSpec((1, POOL_HALO, POOL_WIDTH),
                         lambda bi, i: (bi, jnp.maximum(i * halo_per_tile - 1, 0), 0)),
            pl.BlockSpec((len(POOL_WINDOWS), POOL_GW, POOL_GW), lambda bi, i: (0, 0, 0)),
            pl.BlockSpec((1, POOL_WIDTH), lambda bi, i: (0, 0)),
        ],
        out_specs=pl.BlockSpec((1, TS_POOL, POOL_WIDTH), lambda bi, i: (bi, i, 0)),
        out_shape=jax.ShapeDtypeStruct((b, seq, POOL_WIDTH), BF16),
        compiler_params=pltpu.CompilerParams(dimension_semantics=("arbitrary", "arbitrary")),
        name="pool_mixer",
    )(u3, u3, wp, sc)


def _compress_kernel(c_ref, pe_ref, w1_ref, w2_ref, cos_ref, slo_ref, shi_ref, o_ref, ot_ref):
    half = CMP_STRIDE * HEAD_DIM
    c = c_ref[...]
    a = _dot((c + pe_ref[:, 0:half]).astype(BF16), w1_ref[0:half, :])
    b = _dot((c + pe_ref[:, half:2 * half]).astype(BF16), w1_ref[half:2 * half, :])
    n_rows = c.shape[0]
    hid = a + pltpu.roll(b, n_rows - 1, 0)
    act = jax.nn.gelu(hid, approximate=True)
    out = _dot(act.astype(BF16), w2_ref[...])
    out = _rope(out, cos_ref[...], slo_ref[...], shi_ref[...])
    row = lax.broadcasted_iota(jnp.int32, out.shape, 0)
    out = jnp.where(row < n_rows - 1, out, 0.0)
    o_ref[...] = out.astype(BF16)
    ot_ref[...] = _value_rows(out.T[0:HEAD_DIM, :]).astype(BF16)


def _compress(c4, pe, w1, w2p, cos, slo, shi):
    b, n_kv, n_chunks, half = c4.shape
    return pl.pallas_call(
        _compress_kernel,
        grid=(b, n_kv),
        in_specs=[
            pl.BlockSpec((None, None, n_chunks, half), lambda bi, j: (bi, j, 0, 0)),
            pl.BlockSpec((None, 1, 2 * half), lambda bi, j: (j // N_KV_GROUPS, 0, 0)),
            pl.BlockSpec((None, 2 * half, CMP_HIDDEN), lambda bi, j: (j // N_KV_GROUPS, 0, 0)),
            pl.BlockSpec((None, CMP_HIDDEN, LANES), lambda bi, j: (j, 0, 0)),
            pl.BlockSpec((None, n_chunks, LANES), lambda bi, j: (j // N_KV_GROUPS, 0, 0)),
            pl.BlockSpec((None, n_chunks, LANES), lambda bi, j: (j // N_KV_GROUPS, 0, 0)),
            pl.BlockSpec((None, n_chunks, LANES), lambda bi, j: (j // N_KV_GROUPS, 0, 0)),
        ],
        out_specs=[
            pl.BlockSpec((None, None, n_chunks, LANES), lambda bi, j: (bi, j, 0, 0)),
            pl.BlockSpec((None, None, V_ROWS, n_chunks), lambda bi, j: (bi, j, 0, 0)),
        ],
        out_shape=[
            jax.ShapeDtypeStruct((b, n_kv, n_chunks, LANES), BF16),
            jax.ShapeDtypeStruct((b, n_kv, V_ROWS, n_chunks), BF16),
        ],
        compiler_params=pltpu.CompilerParams(dimension_semantics=("arbitrary", "arbitrary")),
        name="compress",
    )(c4, pe, w1, w2p, cos, slo, shi)


def _block_rank(score):
    n_slc = score.shape[0]
    sub = 8
    ranks = []
    for v in range(n_slc // sub):
        blk = score[v * sub:(v + 1) * sub, :]
        jb_v = v * sub + lax.broadcasted_iota(jnp.int32, blk.shape, 0)
        r = jnp.zeros(blk.shape, F32)
        for jp in range(n_slc):
            row = score[jp:jp + 1, :]
            ge = jnp.where(row >= blk, 1.0, 0.0)
            gt = jnp.where(row > blk, 1.0, 0.0)
            if jp < v * sub:
                r = r + ge
            elif jp >= (v + 1) * sub:
                r = r + gt
            else:
                r = r + jnp.where(jb_v > jp, ge, gt)
        ranks.append(r)
    return jnp.concatenate(ranks, axis=0)


def _attn_kernel(q_ref, gn_ref, kc_ref, vct_ref, ks_ref, vst_ref, kw_ref, vwt_ref, ovt_ref,
                 o_ref, qs_ref, qsel_ref, s_ref, p_ref, al_ref, mc_ref, mw_ref, ms_ref,
                 accc_ref, accw_ref, accs_ref, gt_ref):
    g = pl.program_id(1)
    i = pl.program_id(2)
    s0 = i * TQ
    n_cmp = kc_ref.shape[0]
    n_slc = ovt_ref.shape[0]
    n_ch = M_ATT // CW
    chunk = lambda ch: slice(ch * CW, (ch + 1) * CW)

    for hh in range(HPG):
        qs_ref[:, hh * TQ:(hh + 1) * TQ] = (
            q_ref[0, :, hh * LANES:(hh + 1) * LANES].astype(F32).T.astype(BF16))

    gt_ref[...] = gn_ref[0].T
    g_base = g * HEAD_DIM

    def gate_rows(c, ch):
        rows = [gt_ref[pl.ds(g_base + c * HPG + HPC * ch + par, 1), :] for par in range(HPC)]
        return rows[0] if HPC == 1 else jnp.concatenate(rows, axis=1)

    def per_head(a):
        return a if HPC == 1 else jnp.concatenate([a] * HPC, axis=1)

    def key_tile(ref, k0, nk):
        return ref[pl.ds(pl.multiple_of(k0, VT_CHUNK), nk), :]

    def value_tile(ref, k0, nk):
        c0 = k0 // VT_CHUNK
        return jnp.concatenate([ref[c0 + c] for c in range(nk // VT_CHUNK)], axis=1)

    imp_raw = []

    def run_step(soft=None, score=None, value=None, importance=False):
        for ch in range(n_ch):
            cs = chunk(ch)
            if soft is not None:
                nk, m_ref, sbias, first = soft
                s = s_ref[0:nk, cs]
                if sbias is not None:
                    s = s + sbias
                mx = jnp.max(s, axis=0, keepdims=True)
                if first:
                    m_new, al_new = mx, None
                else:
                    m_prev = m_ref[:, cs]
                    m_new = jnp.maximum(m_prev, mx)
                    al_new = jnp.exp2(m_prev - m_new)
                p_new = jnp.exp2(s - m_new).astype(BF16)
            if score is not None:
                kt, q_t_ref, bias = score
                sc = _dot(kt, q_t_ref[:, cs])
                s_ref[0:kt.shape[0], cs] = sc if bias is None else sc + bias
            if value is not None:
                vt, acc_ref, vfirst = value
                pv = _dot(vt, p_ref[0:vt.shape[1], cs])
                acc_ref[:, cs] = pv if vfirst else al_ref[:, cs] * acc_ref[:, cs] + pv
            if importance:
                for par in range(HPC):
                    imp_raw.append(_dot(ovt_ref[...], p_ref[0:n_cmp, ch * CW + par * TQ:
                                                            ch * CW + (par + 1) * TQ]))
            if soft is not None:
                p_ref[0:nk, cs] = p_new
                m_ref[:, cs] = m_new
                if al_new is not None:
                    al_ref[:, cs] = al_new

    n_idx = lax.broadcasted_iota(jnp.int32, (n_cmp, TQ), 0)
    t_cmp = s0 + lax.broadcasted_iota(jnp.int32, (n_cmp, TQ), 1)
    cmp_bias = per_head(jnp.where(n_idx * CMP_STRIDE + CMP_LEN - 1 <= t_cmp, 0.0, NEG_INF))

    w0 = jnp.maximum(s0 - WINDOW, 0)
    win_tiles = [(off, min(NK_SEL, WIN_KEYS - off)) for off in range(0, WIN_KEYS, NK_SEL)]
    n_win = len(win_tiles)

    def win_bias(off, nk):
        kpos = w0 + off + lax.broadcasted_iota(jnp.int32, (nk, TQ), 0)
        t_w = s0 + lax.broadcasted_iota(jnp.int32, (nk, TQ), 1)
        return per_head(
            jnp.where(kpos <= t_w, jnp.where(kpos > t_w - WINDOW, 0.0, NEG_INF), NEG_INF))

    def win_step(k, last_score=None):
        args = {}
        if k < n_win:
            off, nk = win_tiles[k]
            args["score"] = (key_tile(kw_ref, w0 + off, nk), qs_ref, win_bias(off, nk))
        elif k == n_win and last_score is not None:
            args["score"] = last_score
        if 1 <= k <= n_win:
            args["soft"] = (win_tiles[k - 1][1], mw_ref, None, k == 1)
        if 2 <= k <= n_win + 1:
            off, nk = win_tiles[k - 2]
            args["value"] = (value_tile(vwt_ref, w0 + off, nk), accw_ref, k == 2)
        return args

    run_step(score=(kc_ref[...], qs_ref, cmp_bias))
    run_step(soft=(n_cmp, mc_ref, None, True), **win_step(0))
    run_step(value=(vct_ref[...], accc_ref, True), importance=True, **win_step(1))
    for k in range(2, n_win):
        run_step(**win_step(k))

    cmp_scale = jnp.where(mc_ref[...] > 0.5 * NEG_INF,
                          1.0 / accc_ref[HEAD_DIM:HEAD_DIM + 1, :], 0.0)

    imp = functools.reduce(
        lambda a, b: a + b,
        [r * cmp_scale[:, hh * TQ:(hh + 1) * TQ] for hh, r in enumerate(imp_raw)])
    jb = lax.broadcasted_iota(jnp.int32, (n_slc, TQ), 0)
    tq = s0 + lax.broadcasted_iota(jnp.int32, (n_slc, TQ), 1)
    causal = jb * SEL_BLOCK <= tq
    near = jnp.logical_or(jb == 0, jb >= tq // SEL_BLOCK - 1)
    score = jnp.where(causal, jnp.where(near, SEL_BONUS, imp), NEG_INF)
    rank = jnp.concatenate(
        [_block_rank(score[:, c0:c0 + LANES]) for c0 in range(0, TQ, LANES)], axis=1)
    sel_bias = jnp.where(causal, jnp.where(rank < float(N_SEL), 0.0, NEG_INF), NEG_INF)
    parts = [jnp.zeros((HEAD_DIM, TQ), F32), sel_bias]
    if n_slc < HEAD_DIM:
        parts.append(jnp.zeros((HEAD_DIM - n_slc, TQ), F32))
    sel_rows = jnp.concatenate(parts, axis=0).astype(BF16)
    for hh in range(HPG):
        hs = slice(hh * TQ, (hh + 1) * TQ)
        qsel_ref[:, hs] = qs_ref[:, hs] + sel_rows

    ms_ref[...] = jnp.full(ms_ref.shape, NEG_INF, F32)
    accs_ref[...] = jnp.zeros(accs_ref.shape, F32)
    j_diag = s0 // NK_SEL

    def sel_scores(j):
        return (key_tile(ks_ref, j * NK_SEL, NK_SEL), qsel_ref, None)

    def sel_values(j, live):
        vt = value_tile(vst_ref, jnp.maximum(j, 0) * NK_SEL, NK_SEL)
        return jnp.where(live, vt, jnp.zeros_like(vt))

    run_step(**win_step(n_win, last_score=sel_scores(0)))
    run_step(**win_step(n_win + 1))

    def sel_body(j, carry):
        run_step(soft=(NK_SEL, ms_ref, None, False), score=sel_scores(j),
                 value=(sel_values(j - 2, j >= 2), accs_ref, False))
        return carry

    lax.fori_loop(1, j_diag + 1, sel_body, 0)
    kpos = j_diag * NK_SEL + lax.broadcasted_iota(jnp.int32, (NK_SEL, TQ), 0)
    t_sel = s0 + lax.broadcasted_iota(jnp.int32, (NK_SEL, TQ), 1)
    diag_bias = per_head(jnp.where(kpos <= t_sel, 0.0, NEG_INF))
    run_step(soft=(NK_SEL, ms_ref, diag_bias, False),
             value=(sel_values(j_diag - 1, j_diag >= 1), accs_ref, False))
    run_step(value=(sel_values(j_diag, True), accs_ref, False))

    heads = []
    for ch in range(n_ch):
        cs = chunk(ch)
        coef_c = gate_rows(0, ch) * cmp_scale[:, cs]
        coef_w = gate_rows(2, ch) / accw_ref[HEAD_DIM:HEAD_DIM + 1, cs]
        coef_s = gate_rows(1, ch) / accs_ref[HEAD_DIM:HEAD_DIM + 1, cs]
        tot = (accc_ref[0:HEAD_DIM, cs] * coef_c + accw_ref[0:HEAD_DIM, cs] * coef_w
               + accs_ref[0:HEAD_DIM, cs] * coef_s)
        heads += [tot[:, par * TQ:(par + 1) * TQ] for par in range(HPC)]
    for pair in range(HPG // 2):
        both = jnp.concatenate(heads[2 * pair:2 * pair + 2], axis=0)
        o_ref[0, :, pair * LANES:(pair + 1) * LANES] = both.T.astype(BF16)


def _attention(q3, gn3, kcmp, vcmp_t, ks4, vst, kw4, vwt, ovt):
    b, seq, _ = q3.shape
    n_cmp = kcmp.shape[2]
    n_slc = seq // SEL_BLOCK
    chunks_per_seq = seq // VT_CHUNK
    per_group = lambda bi, g, i: (bi, g, 0, 0)
    return pl.pallas_call(
        _attn_kernel,
        grid=(b, N_KV_GROUPS, seq // TQ),
        in_specs=[
            pl.BlockSpec((1, TQ, HPG * LANES), lambda bi, g, i: (bi, i, g)),
            pl.BlockSpec((1, TQ, LANES), lambda bi, g, i: (bi, i, 0)),
            pl.BlockSpec((None, None, n_cmp, LANES), per_group),
            pl.BlockSpec((None, None, V_ROWS, n_cmp),
                         lambda bi, g, i: (bi, N_KV_GROUPS + g, 0, 0)),
            pl.BlockSpec((None, None, seq, LANES), per_group),
            pl.BlockSpec((None, None, chunks_per_seq, V_ROWS, VT_CHUNK),
                         lambda bi, g, i: (bi, g, 0, 0, 0)),
            pl.BlockSpec((None, None, seq, LANES), per_group),
            pl.BlockSpec((None, None, chunks_per_seq, V_ROWS, VT_CHUNK),
                         lambda bi, g, i: (bi, g, 0, 0, 0)),
            pl.BlockSpec((n_slc, n_cmp), lambda bi, g, i: (0, 0)),
        ],
        out_specs=pl.BlockSpec((1, TQ, HPG * HEAD_DIM), lambda bi, g, i: (bi, i, g)),
        out_shape=jax.ShapeDtypeStruct((b, seq, NSA_WIDTH), BF16),
        scratch_shapes=[
            pltpu.VMEM((LANES, M_ATT), BF16),
            pltpu.VMEM((LANES, M_ATT), BF16),
            pltpu.VMEM((NK_SEL, M_ATT), F32),
            pltpu.VMEM((NK_SEL, M_ATT), BF16),
            pltpu.VMEM((1, M_ATT), F32),
            pltpu.VMEM((1, M_ATT), F32),
            pltpu.VMEM((1, M_ATT), F32),
            pltpu.VMEM((1, M_ATT), F32),
            pltpu.VMEM((V_ROWS, M_ATT), F32),
            pltpu.VMEM((V_ROWS, M_ATT), F32),
            pltpu.VMEM((V_ROWS, M_ATT), F32),
            pltpu.VMEM((LANES, TQ), F32),
        ],
        compiler_params=pltpu.CompilerParams(
            dimension_semantics=("arbitrary", "arbitrary", "arbitrary"),
            vmem_limit_bytes=VMEM_LIMIT),
        name="nsa_attention",
    )(q3, gn3, kcmp, vcmp_t, ks4, vst, kw4, vwt, ovt)


def _mlp_kernel(x_ref, yp_ref, yn_ref, gm_ref, wpp_ref, wpn_ref, wo_ref, nm_ref, w1_ref, w2_ref,
                nf_ref, o_ref, *, final):
    p1 = _dot(yp_ref[...], wpp_ref[...])
    p2 = _dot(yn_ref[...], wpn_ref[...])
    ga = jax.nn.sigmoid(gm_ref[:, 0:D_MODEL])
    gb = jax.nn.sigmoid(gm_ref[:, D_MODEL:2 * D_MODEL])
    merged = ga * p1 + gb * p2
    x = x_ref[...] + _dot(merged.astype(BF16), wo_ref[...])
    h = _rms(x, nm_ref[...]).astype(BF16)
    acc = jnp.zeros((TM_MLP, D_MODEL), F32)
    for c in range(D_FF // FF_CHUNK):
        sl = slice(c * FF_CHUNK, (c + 1) * FF_CHUNK)
        a = jnp.square(jnp.maximum(_dot(h, w1_ref[:, sl]), 0.0)).astype(BF16)
        acc = acc + _dot(a, w2_ref[sl, :])
    x = x + acc
    if final:
        x = _rms(x, nf_ref[...])
    o_ref[...] = x


def _merge_mlp(x2, yp, yn, gm, wpp, wpn, wo, nm, w1, w2, nf, layer, final):
    n = x2.shape[0]
    row = lambda w_: pl.BlockSpec((TM_MLP, w_), lambda i: (i, 0))
    res = lambda a: _resident(a.shape, lambda i: (0,) * a.ndim)
    lay = lambda a: _layer_resident(a, layer)
    return pl.pallas_call(
        functools.partial(_mlp_kernel, final=final),
        grid=(n // TM_MLP,),
        in_specs=[row(D_MODEL), row(POOL_WIDTH), row(NSA_WIDTH), row(2 * D_MODEL),
                  lay(wpp), lay(wpn), lay(wo), res(nm), lay(w1), lay(w2), res(nf)],
        out_specs=row(D_MODEL),
        out_shape=jax.ShapeDtypeStruct((n, D_MODEL), F32),
        compiler_params=pltpu.CompilerParams(
            dimension_semantics=("arbitrary",), vmem_limit_bytes=VMEM_LIMIT),
        name="merge_mlp",
    )(x2, yp, yn, gm, wpp, wpn, wo, nm, w1, w2, nf)


def _rope_tables(pos):
    inv = ROPE_THETA ** (-jnp.arange(0, HEAD_DIM, 2, dtype=F32) / HEAD_DIM)
    ang = pos.astype(F32)[:, None] * inv[None, :]
    ang = jnp.concatenate([ang, ang, ang, ang], axis=-1)
    first_half = (jnp.arange(LANES) % HEAD_DIM) < HEAD_DIM // 2
    cos, sin = jnp.cos(ang), jnp.sin(ang)
    return cos, jnp.where(first_half, -sin, 0.0), jnp.where(first_half, 0.0, sin)


def _permute_w_in(w_in):
    o_q = POOL_WIDTH
    o_kv = o_q + NSA_WIDTH
    o_gn = o_kv + 6 * KV_WIDTH
    o_gm = o_gn + N_GATE
    depth = w_in.shape[0]
    gn = w_in[:, :, o_gn:o_gm].reshape(depth, D_MODEL, N_KV_GROUPS, HPG, 3)
    gn = gn.transpose(0, 1, 2, 4, 3).reshape(depth, D_MODEL, N_KV_GROUPS, 3 * HPG)
    gn = jnp.pad(gn, ((0, 0), (0, 0), (0, 0), (0, LANES // N_KV_GROUPS - 3 * HPG)))
    gn = gn.reshape(depth, D_MODEL, LANES)
    return jnp.concatenate(
        [w_in[:, :, 0:o_gn], w_in[:, :, o_gm:], gn], axis=-1).astype(BF16)


def kernel(x, norm_mix, w_in, w_pool, pool_scale, pe_k, pe_v, w_ck1, w_ck2, w_cv1, w_cv2,
           w_proj_pool, w_proj_nsa, w_out, norm_mlp, w_ff1, w_ff2, norm_final):
    b, seq, d = x.shape
    depth = w_in.shape[0]
    n = b * seq
    n_chunks = seq // CMP_STRIDE
    n_slc = seq // SEL_BLOCK
    assert n_slc <= HEAD_DIM, "the selection one-hot shares the 64 spare key lanes"
    assert seq >= WIN_KEYS and seq % TM_IN == 0

    w_in_p = _permute_w_in(w_in)
    w_pool_b = w_pool.astype(BF16)
    pe = jnp.stack([pe_k, pe_v], axis=1).reshape(depth, 2, 1, CMP_LEN * HEAD_DIM)
    w_c1 = jnp.stack([w_ck1, w_cv1], axis=1).astype(BF16)
    pad = jnp.zeros_like(w_ck2)
    w_c2 = jnp.stack([jnp.concatenate([w, pad], axis=-1) for w in (w_ck2, w_ck2, w_cv2, w_cv2)],
                     axis=1).astype(BF16)
    wpp, wpn, wo = w_proj_pool.astype(BF16), w_proj_nsa.astype(BF16), w_out.astype(BF16)
    w1, w2 = w_ff1.astype(BF16), w_ff2.astype(BF16)

    cos, slo, shi = _rope_tables(jnp.arange(seq))
    ccos, cslo, cshi = _rope_tables(jnp.arange(n_chunks) * CMP_STRIDE + CMP_LEN - 1)
    ident = (jnp.ones_like(ccos), jnp.zeros_like(cslo), jnp.zeros_like(cshi))
    cmp_tabs = [jnp.stack([t, i_], axis=0) for t, i_ in zip((ccos, cslo, cshi), ident)]
    cmp_start = jnp.arange(n_chunks) * CMP_STRIDE
    slc_start = jnp.arange(n_slc) * SEL_BLOCK
    ovt = ((cmp_start[None, :] <= slc_start[:, None] + SEL_BLOCK - 1)
           & (cmp_start[None, :] + CMP_LEN - 1 >= slc_start[:, None])).astype(BF16)

    x2 = x.reshape(n, d)
    for l in range(depth):
        u, q, c4, ks, vst, kw, vwt, gm, gn = _inproj(
            x2, norm_mix[l][None, :], w_in_p, l, cos, slo, shi, b, seq)
        y_pool = _pool(u.reshape(b, seq, POOL_WIDTH), w_pool_b[l], pool_scale[l][None, :])
        cmp_n, cmp_t = _compress(c4, pe[l], w_c1[l], w_c2[l], *cmp_tabs)
        y_nsa = _attention(
            q.reshape(b, seq, 2 * NSA_WIDTH), gn.reshape(b, seq, LANES), cmp_n, cmp_t,
            ks, vst, kw, vwt, ovt)
        x2 = _merge_mlp(
            x2, y_pool.reshape(n, POOL_WIDTH), y_nsa.reshape(n, NSA_WIDTH), gm,
            wpp, wpn, wo, norm_mlp[l][None, :], w1, w2, norm_final[None, :],
            layer=l, final=(l == depth - 1))
    return x2.reshape(b, seq, d)
```

```python
import functools

import jax
import jax.numpy as jnp
import numpy as np
from jax import lax
from jax.experimental import pallas as pl
from jax.experimental.pallas import tpu as pltpu

F32 = jnp.float32
BF16 = jnp.bfloat16

D_MODEL = 1024
POOL_WINDOWS = (2, 4, 8, 16)
POOL_WIDTH = 512
POOL_GW = 128
N_HEADS = 16
HEAD_DIM = 64
N_KV_GROUPS = 2
HPG = 8
NSA_WIDTH = 1024
KV_WIDTH = 128
CMP_LEN = 32
CMP_STRIDE = 16
CMP_HIDDEN = 256
SEL_BLOCK = 64
N_SEL = 16
WINDOW = 512
SEL_BONUS = 1e4
NEG_INF = -1e30
ROPE_THETA = 10000.0
D_FF = 4096
RMS_EPS = 1e-6
N_GATE = 3 * N_HEADS
Q_SCALE = HEAD_DIM ** -0.5 * float(np.log2(np.e))

LANES = 128
VMEM_LIMIT = 56 * 1024 * 1024

C_U = 0
C_Q = C_U + POOL_WIDTH
C_KV = C_Q + NSA_WIDTH
C_GM = C_KV + 6 * KV_WIDTH
C_GN = C_GM + 2 * D_MODEL
N_INP = C_GN + LANES

TM_IN = 512
TS_POOL = 512
POOL_HALO = 16
TM_MLP = 512
FF_CHUNK = 1024
TQ = 256
M_ATT = HPG * TQ
CW = TQ
NK_SEL = 256
WIN_KEYS = WINDOW + TQ
VT_CHUNK = 128
V_ROWS = HEAD_DIM + 16
PAT_ALL, PAT_CAUSAL, PAT_ABOVE, PAT_NONE = 0, 1, 2, 3
assert NK_SEL == TQ and WINDOW == 2 * NK_SEL and WIN_KEYS % NK_SEL == 0


def _dot(a, b):
    return jnp.dot(a, b, preferred_element_type=F32)


def _rms(x, g):
    return x * lax.rsqrt(jnp.mean(x * x, axis=-1, keepdims=True) + RMS_EPS) * g


def _rope(t, cos, sin_lo, sin_hi):
    return t * cos + pltpu.roll(t, LANES - 32, 1) * sin_lo + pltpu.roll(t, 32, 1) * sin_hi


def _resident(shape, index_map):
    return pl.BlockSpec(shape, index_map, pipeline_mode=pl.Buffered(1))


def _layer_resident(stacked, layer):
    nd = stacked.ndim - 1
    return _resident((None,) + stacked.shape[1:], lambda i: (layer,) + (0,) * nd)


def _value_rows(v_t):
    tail_row = lax.broadcasted_iota(jnp.int32, (V_ROWS - HEAD_DIM, v_t.shape[1]), 0)
    return jnp.concatenate([v_t, jnp.where(tail_row == 0, 1.0, 0.0)], axis=0)


def _inproj_kernel(x_ref, g_ref, w_ref, cos_ref, slo_ref, shi_ref,
                   u_ref, q_ref, cmp_ref, ks_ref, vst_ref, kw_ref, vwt_ref, gm_ref, gn_ref,
                   kv_scr, *, tiles_per_seq):
    h = _rms(x_ref[...], g_ref[...]).astype(BF16)
    cos, slo, shi = cos_ref[...], slo_ref[...], shi_ref[...]
    lane = lax.broadcasted_iota(jnp.int32, (TM_IN, LANES), 1)
    low = lane < HEAD_DIM
    pos = (pl.program_id(0) % tiles_per_seq) * TM_IN + lax.broadcasted_iota(
        jnp.int32, (TM_IN, LANES), 0)
    block_onehot = jnp.where(lane - HEAD_DIM == pos // SEL_BLOCK, 1.0, 0.0)

    u_ref[...] = _dot(h, w_ref[:, C_U:C_Q])
    q = _dot(h, w_ref[:, C_Q:C_KV])
    for k in range(NSA_WIDTH // LANES):
        qt = (_rope(q[:, k * LANES:(k + 1) * LANES], cos, slo, shi) * Q_SCALE).T
        spare = jnp.zeros((LANES - HEAD_DIM, TM_IN), F32)
        for par in range(2):
            q_ref[0, 2 * k + par] = jnp.concatenate(
                [qt[par * HEAD_DIM:(par + 1) * HEAD_DIM, :], spare], axis=0).astype(BF16)
    kv = _dot(h, w_ref[:, C_KV:C_GM])
    n_rows = TM_IN // CMP_STRIDE
    low_c = lax.broadcasted_iota(jnp.int32, (n_rows, LANES), 1) < HEAD_DIM
    for t in range(2):
        kv_scr[t] = kv[:, t * LANES:(t + 1) * LANES]
        for pp in range(CMP_STRIDE // 2):
            a = kv_scr[t, pl.ds(2 * pp, n_rows, stride=CMP_STRIDE), :]
            b = kv_scr[t, pl.ds(2 * pp + 1, n_rows, stride=CMP_STRIDE), :]
            sl = slice(pp * LANES, (pp + 1) * LANES)
            cmp_ref[0, 2 * t, :, sl] = jnp.where(low_c, a, pltpu.roll(b, HEAD_DIM, 1))
            cmp_ref[0, 2 * t + 1, :, sl] = jnp.where(low_c, pltpu.roll(a, HEAD_DIM, 1), b)
    ks = _rope(kv[:, 2 * KV_WIDTH:3 * KV_WIDTH], cos, slo, shi)
    kw = _rope(kv[:, 4 * KV_WIDTH:5 * KV_WIDTH], cos, slo, shi)
    vs_t = kv[:, 3 * KV_WIDTH:4 * KV_WIDTH].T
    vw_t = kv[:, 5 * KV_WIDTH:6 * KV_WIDTH].T
    for g in range(N_KV_GROUPS):
        ks_g = ks if g == 0 else pltpu.roll(ks, HEAD_DIM, 1)
        kw_g = kw if g == 0 else pltpu.roll(kw, HEAD_DIM, 1)
        ks_ref[0, g] = jnp.where(low, ks_g, block_onehot).astype(BF16)
        kw_ref[0, g] = jnp.where(low, kw_g, 0.0).astype(BF16)
        vs_g = _value_rows(vs_t[g * HEAD_DIM:(g + 1) * HEAD_DIM, :]).astype(BF16)
        vw_g = _value_rows(vw_t[g * HEAD_DIM:(g + 1) * HEAD_DIM, :]).astype(BF16)
        for c in range(TM_IN // VT_CHUNK):
            sl = slice(c * VT_CHUNK, (c + 1) * VT_CHUNK)
            vst_ref[0, g, c] = vs_g[:, sl]
            vwt_ref[0, g, c] = vw_g[:, sl]
    gm_ref[...] = _dot(h, w_ref[:, C_GM:C_GN])
    gn_ref[0] = jax.nn.sigmoid(_dot(h, w_ref[:, C_GN:N_INP])).T


def _inproj(x2, g, w, layer, cos, slo, shi, b, seq):
    n = x2.shape[0]
    tiles_per_seq = seq // TM_IN
    n_chunks = TM_IN // VT_CHUNK
    row = lambda w_: pl.BlockSpec((TM_IN, w_), lambda i: (i, 0))
    tab = pl.BlockSpec((TM_IN, LANES), lambda i: (i % tiles_per_seq, 0))
    kg = pl.BlockSpec((1, N_KV_GROUPS, TM_IN, LANES),
                      lambda i: (i // tiles_per_seq, 0, i % tiles_per_seq, 0))
    vt = pl.BlockSpec((1, N_KV_GROUPS, n_chunks, V_ROWS, VT_CHUNK),
                      lambda i: (i // tiles_per_seq, 0, i % tiles_per_seq, 0, 0))
    cmp_spec = pl.BlockSpec(
        (1, 2 * N_KV_GROUPS, TM_IN // CMP_STRIDE, CMP_STRIDE * HEAD_DIM),
        lambda i: (i // tiles_per_seq, 0, i % tiles_per_seq, 0))
    qt_spec = pl.BlockSpec((1, N_HEADS, LANES, TM_IN),
                           lambda i: (i // tiles_per_seq, 0, 0, i % tiles_per_seq))
    gt_spec = pl.BlockSpec((1, LANES, TM_IN), lambda i: (i // tiles_per_seq, 0, i % tiles_per_seq))
    k_shape = jax.ShapeDtypeStruct((b, N_KV_GROUPS, seq, LANES), BF16)
    vt_shape = jax.ShapeDtypeStruct((b, N_KV_GROUPS, seq // VT_CHUNK, V_ROWS, VT_CHUNK), BF16)
    return pl.pallas_call(
        functools.partial(_inproj_kernel, tiles_per_seq=tiles_per_seq),
        grid=(n // TM_IN,),
        in_specs=[row(D_MODEL), _resident((1, D_MODEL), lambda i: (0, 0)),
                  _layer_resident(w, layer), tab, tab, tab],
        out_specs=[row(POOL_WIDTH), qt_spec, cmp_spec, kg, vt,
                   kg, vt, row(2 * D_MODEL), gt_spec],
        out_shape=[
            jax.ShapeDtypeStruct((n, POOL_WIDTH), F32),
            jax.ShapeDtypeStruct((b, N_HEADS, LANES, seq), BF16),
            jax.ShapeDtypeStruct(
                (b, 2 * N_KV_GROUPS, seq // CMP_STRIDE, CMP_STRIDE * HEAD_DIM), F32),
            k_shape, vt_shape, k_shape, vt_shape,
            jax.ShapeDtypeStruct((n, 2 * D_MODEL), F32),
            jax.ShapeDtypeStruct((b, LANES, seq), F32),
        ],
        scratch_shapes=[pltpu.VMEM((2, TM_IN, LANES), F32)],
        compiler_params=pltpu.CompilerParams(
            dimension_semantics=("arbitrary",), vmem_limit_bytes=VMEM_LIMIT),
        name="in_proj",
    )(x2, g, w, cos, slo, shi)


def _pool_kernel(u_ref, up_ref, wp_ref, sc_ref, o_ref):
    i = pl.program_id(1)
    cur = u_ref[0]
    prev = jnp.where(i > 0, up_ref[0], 0.0)
    t = i * TS_POOL + lax.broadcasted_iota(jnp.int32, (TS_POOL, POOL_GW), 0)
    for g, w in enumerate(POOL_WINDOWS):
        sl = slice(g * POOL_GW, (g + 1) * POOL_GW)
        cg = cur[:, sl]
        s = jnp.concatenate([prev[:, sl], cg], axis=0)
        sh = 1
        while sh < w:
            s = s + pltpu.roll(s, sh, 0)
            sh *= 2
        cnt = jnp.minimum(t + 1, w).astype(F32)
        d = s[POOL_HALO:] / cnt - cg
        y = _dot(d.astype(BF16), wp_ref[g]) * sc_ref[:, sl]
        o_ref[0, :, sl] = y.astype(BF16)


def _pool(u3, wp, sc):
    b, seq, _ = u3.shape
    halo_per_tile = TS_POOL // POOL_HALO
    return pl.pallas_call(
        _pool_kernel,
        grid=(b, seq // TS_POOL),
        in_specs=[
            pl.BlockSpec((1, TS_POOL, POOL_WIDTH), lambda bi, i: (bi, i, 0)),
            pl.BlockSpec((1, POOL_HALO, POOL_WIDTH),
                         lambda bi, i: (bi, jnp.maximum(i * halo_per_tile - 1, 0), 0)),
            pl.BlockSpec((len(POOL_WINDOWS), POOL_GW, POOL_GW), lambda bi, i: (0, 0, 0)),
            pl.BlockSpec((1, POOL_WIDTH), lambda bi, i: (0, 0)),
        ],
        out_specs=pl.BlockSpec((1, TS_POOL, POOL_WIDTH), lambda bi, i: (bi, i, 0)),
        out_shape=jax.ShapeDtypeStruct((b, seq, POOL_WIDTH), BF16),
        compiler_params=pltpu.CompilerParams(dimension_semantics=("arbitrary", "arbitrary")),
        name="pool_mixer",
    )(u3, u3, wp, sc)


def _compress_kernel(c_ref, pe_ref, w1_ref, w2_ref, cos_ref, slo_ref, shi_ref, o_ref, ot_ref):
    half = CMP_STRIDE * HEAD_DIM
    c = c_ref[...]
    a = _dot((c + pe_ref[:, 0:half]).astype(BF16), w1_ref[0:half, :])
    b = _dot((c + pe_ref[:, half:2 * half]).astype(BF16), w1_ref[half:2 * half, :])
    n_rows = c.shape[0]
    hid = a + pltpu.roll(b, n_rows - 1, 0)
    act = jax.nn.gelu(hid, approximate=True)
    out = _dot(act.astype(BF16), w2_ref[...])
    out = _rope(out, cos_ref[...], slo_ref[...], shi_ref[...])
    row = lax.broadcasted_iota(jnp.int32, out.shape, 0)
    out = jnp.where(row < n_rows - 1, out, 0.0)
    o_ref[...] = out.astype(BF16)
    ot_ref[...] = _value_rows(out.T[0:HEAD_DIM, :]).astype(BF16)


def _compress(c4, pe, w1, w2p, cos, slo, shi):
    b, n_kv, n_chunks, half = c4.shape
    return pl.pallas_call(
        _compress_kernel,
        grid=(b, n_kv),
        in_specs=[
            pl.BlockSpec((None, None, n_chunks, half), lambda bi, j: (bi, j, 0, 0)),
            pl.BlockSpec((None, 1, 2 * half), lambda bi, j: (j // N_KV_GROUPS, 0, 0)),
            pl.BlockSpec((None, 2 * half, CMP_HIDDEN), lambda bi, j: (j // N_KV_GROUPS, 0, 0)),
            pl.BlockSpec((None, CMP_HIDDEN, LANES), lambda bi, j: (j, 0, 0)),
            pl.BlockSpec((None, n_chunks, LANES), lambda bi, j: (j // N_KV_GROUPS, 0, 0)),
            pl.BlockSpec((None, n_chunks, LANES), lambda bi, j: (j // N_KV_GROUPS, 0, 0)),
            pl.BlockSpec((None, n_chunks, LANES), lambda bi, j: (j // N_KV_GROUPS, 0, 0)),
        ],
        out_specs=[
            pl.BlockSpec((None, None, n_chunks, LANES), lambda bi, j: (bi, j, 0, 0)),
            pl.BlockSpec((None, None, V_ROWS, n_chunks), lambda bi, j: (bi, j, 0, 0)),
        ],
        out_shape=[
            jax.ShapeDtypeStruct((b, n_kv, n_chunks, LANES), BF16),
            jax.ShapeDtypeStruct((b, n_kv, V_ROWS, n_chunks), BF16),
        ],
        compiler_params=pltpu.CompilerParams(dimension_semantics=("arbitrary", "arbitrary")),
        name="compress",
    )(c4, pe, w1, w2p, cos, slo, shi)


def _block_rank(score):
    n_slc = score.shape[0]
    sub = 8
    ranks = []
    for v in range(n_slc // sub):
        blk = score[v * sub:(v + 1) * sub, :]
        jb_v = v * sub + lax.broadcasted_iota(jnp.int32, blk.shape, 0)
        r = jnp.zeros(blk.shape, F32)
        for jp in range(n_slc):
            row = score[jp:jp + 1, :]
            ge = jnp.where(row >= blk, 1.0, 0.0)
            gt = jnp.where(row > blk, 1.0, 0.0)
            if jp < v * sub:
                r = r + ge
            elif jp >= (v + 1) * sub:
                r = r + gt
            else:
                r = r + jnp.where(jb_v > jp, ge, gt)
        ranks.append(r)
    return jnp.concatenate(ranks, axis=0)


def _attn_kernel(q_ref, gn_ref, kc_ref, vct_ref, ks_ref, vst_ref, kw_ref, vwt_ref, ovt_ref,
                 pat_ref, o_ref, qsel_ref, s_ref, p_ref, al_ref, mc_ref, mw_ref, ms_ref,
                 accc_ref, accw_ref, accs_ref):
    g = pl.program_id(1)
    i = pl.program_id(2)
    s0 = i * TQ
    n_cmp = kc_ref.shape[0]
    n_slc = ovt_ref.shape[0]
    n_ch = M_ATT // CW
    chunk = lambda ch: slice(ch * CW, (ch + 1) * CW)

    q_plain = lambda ch: q_ref[0, ch]
    q_selected = lambda ch: qsel_ref[:, chunk(ch)]

    g_base = g * HEAD_DIM

    def gate_rows(c, ch):
        return gn_ref[0, pl.ds(g_base + c * HPG + ch, 1), :]

    def key_tile(ref, k0, nk):
        return ref[pl.ds(pl.multiple_of(k0, VT_CHUNK), nk), :]

    def value_tile(ref, k0, nk):
        c0 = k0 // VT_CHUNK
        return jnp.concatenate([ref[c0 + c] for c in range(nk // VT_CHUNK)], axis=1)

    imp_raw = []

    def run_step(soft=None, score=None, value=None, importance=False):
        for ch in range(n_ch):
            cs = chunk(ch)
            if soft is not None:
                nk, m_ref, sbias, first = soft
                s = s_ref[0:nk, cs]
                if sbias is not None:
                    s = s + sbias()
                mx = jnp.max(s, axis=0, keepdims=True)
                if first:
                    m_new, al_new = mx, None
                else:
                    m_prev = m_ref[:, cs]
                    m_new = jnp.maximum(m_prev, mx)
                    al_new = jnp.exp2(m_prev - m_new)
                p_new = jnp.exp2(s - m_new).astype(BF16)
            if score is not None:
                kt, q_chunk, bias = score
                sc = _dot(kt, q_chunk(ch))
                s_ref[0:kt.shape[0], cs] = sc if bias is None else sc + bias()
            if value is not None:
                vt, acc_ref, vfirst = value
                pv = _dot(vt, p_ref[0:vt.shape[1], cs])
                acc_ref[:, cs] = pv if vfirst else al_ref[:, cs] * acc_ref[:, cs] + pv
            if importance:
                imp_raw.append(_dot(ovt_ref[...], p_ref[0:n_cmp, cs]))
            if soft is not None:
                p_ref[0:nk, cs] = p_new
                m_ref[:, cs] = m_new
                if al_new is not None:
                    al_ref[:, cs] = al_new

    n_idx = lax.broadcasted_iota(jnp.int32, (n_cmp, TQ), 0)
    t_cmp = s0 + lax.broadcasted_iota(jnp.int32, (n_cmp, TQ), 1)
    cmp_mask = jnp.where(n_idx * CMP_STRIDE + CMP_LEN - 1 <= t_cmp, 0.0, NEG_INF)
    cmp_bias = lambda: cmp_mask

    w0 = jnp.maximum(s0 - WINDOW, 0)
    win_tiles = [(off, min(NK_SEL, WIN_KEYS - off)) for off in range(0, WIN_KEYS, NK_SEL)]
    n_win = len(win_tiles)

    def win_bias(off, nk):
        d = w0 + off - s0
        pat = jnp.where(d == -WINDOW, PAT_ABOVE,
                        jnp.where(d == 0, PAT_CAUSAL, jnp.where(d < 0, PAT_ALL, PAT_NONE)))
        return lambda: pat_ref[pat]

    def win_step(k):
        args = {}
        if k < n_win:
            off, nk = win_tiles[k]
            args["score"] = (key_tile(kw_ref, w0 + off, nk), q_plain, win_bias(off, nk))
        if 1 <= k <= n_win:
            args["soft"] = (win_tiles[k - 1][1], mw_ref, None, k == 1)
        if 2 <= k <= n_win + 1:
            off, nk = win_tiles[k - 2]
            args["value"] = (value_tile(vwt_ref, w0 + off, nk), accw_ref, k == 2)
        return args

    run_step(score=(kc_ref[...], q_plain, cmp_bias))
    run_step(soft=(n_cmp, mc_ref, None, True), **win_step(0))
    run_step(value=(vct_ref[...], accc_ref, True), importance=True, **win_step(1))
    for k in range(2, n_win):
        run_step(**win_step(k))

    cmp_scale = jnp.where(mc_ref[...] > 0.5 * NEG_INF,
                          1.0 / accc_ref[HEAD_DIM:HEAD_DIM + 1, :], 0.0)

    imp = functools.reduce(
        lambda a, b: a + b,
        [r * cmp_scale[:, hh * TQ:(hh + 1) * TQ] for hh, r in enumerate(imp_raw)])
    jb = lax.broadcasted_iota(jnp.int32, (n_slc, TQ), 0)
    tq = s0 + lax.broadcasted_iota(jnp.int32, (n_slc, TQ), 1)
    causal = jb * SEL_BLOCK <= tq
    near = jnp.logical_or(jb == 0, jb >= tq // SEL_BLOCK - 1)
    score = jnp.where(causal, jnp.where(near, SEL_BONUS, imp), NEG_INF)
    rank = jnp.concatenate(
        [_block_rank(score[:, c0:c0 + LANES]) for c0 in range(0, TQ, LANES)], axis=1)
    sel_bias = jnp.where(causal, jnp.where(rank < float(N_SEL), 0.0, NEG_INF), NEG_INF)
    parts = [jnp.zeros((HEAD_DIM, TQ), F32), sel_bias]
    if n_slc < HEAD_DIM:
        parts.append(jnp.zeros((HEAD_DIM - n_slc, TQ), F32))
    sel_rows = jnp.concatenate(parts, axis=0).astype(BF16)
    for hh in range(HPG):
        qsel_ref[:, chunk(hh)] = q_ref[0, hh] + sel_rows

    ms_ref[...] = jnp.full(ms_ref.shape, NEG_INF, F32)
    accs_ref[...] = jnp.zeros(accs_ref.shape, F32)
    j_diag = s0 // NK_SEL

    def sel_scores(j):
        return (key_tile(ks_ref, j * NK_SEL, NK_SEL), q_selected, None)

    def sel_values(j, live):
        vt = value_tile(vst_ref, jnp.maximum(j, 0) * NK_SEL, NK_SEL)
        return jnp.where(live, vt, jnp.zeros_like(vt))

    run_step(score=sel_scores(0), **win_step(n_win))
    run_step(**win_step(n_win + 1))

    def sel_body(j, carry):
        run_step(soft=(NK_SEL, ms_ref, None, False), score=sel_scores(j),
                 value=(sel_values(j - 2, j >= 2), accs_ref, False))
        return carry

    lax.fori_loop(1, j_diag + 1, sel_body, 0)
    diag_bias = lambda: pat_ref[PAT_CAUSAL]
    run_step(soft=(NK_SEL, ms_ref, diag_bias, False),
             value=(sel_values(j_diag - 1, j_diag >= 1), accs_ref, False))
    run_step(value=(sel_values(j_diag, True), accs_ref, False))

    heads = []
    for ch in range(n_ch):
        cs = chunk(ch)
        coef_c = gate_rows(0, ch) * cmp_scale[:, cs]
        coef_w = gate_rows(2, ch) / accw_ref[HEAD_DIM:HEAD_DIM + 1, cs]
        coef_s = gate_rows(1, ch) / accs_ref[HEAD_DIM:HEAD_DIM + 1, cs]
        tot = (accc_ref[0:HEAD_DIM, cs] * coef_c + accw_ref[0:HEAD_DIM, cs] * coef_w
               + accs_ref[0:HEAD_DIM, cs] * coef_s)
        heads.append(tot)
    for pair in range(HPG // 2):
        both = jnp.concatenate(heads[2 * pair:2 * pair + 2], axis=0)
        o_ref[0, :, pair * LANES:(pair + 1) * LANES] = both.T.astype(BF16)


def _attention(q3, gn3, kcmp, vcmp_t, ks4, vst, kw4, vwt, ovt, pat):
    b, _, _, seq = q3.shape
    n_cmp = kcmp.shape[2]
    n_slc = seq // SEL_BLOCK
    chunks_per_seq = seq // VT_CHUNK
    per_group = lambda bi, g, i: (bi, g, 0, 0)
    return pl.pallas_call(
        _attn_kernel,
        grid=(b, N_KV_GROUPS, seq // TQ),
        in_specs=[
            pl.BlockSpec((1, HPG, LANES, TQ), lambda bi, g, i: (bi, g, 0, i)),
            pl.BlockSpec((1, LANES, TQ), lambda bi, g, i: (bi, 0, i)),
            pl.BlockSpec((None, None, n_cmp, LANES), per_group),
            pl.BlockSpec((None, None, V_ROWS, n_cmp),
                         lambda bi, g, i: (bi, N_KV_GROUPS + g, 0, 0)),
            pl.BlockSpec((None, None, seq, LANES), per_group),
            pl.BlockSpec((None, None, chunks_per_seq, V_ROWS, VT_CHUNK),
                         lambda bi, g, i: (bi, g, 0, 0, 0)),
            pl.BlockSpec((None, None, seq, LANES), per_group),
            pl.BlockSpec((None, None, chunks_per_seq, V_ROWS, VT_CHUNK),
                         lambda bi, g, i: (bi, g, 0, 0, 0)),
            pl.BlockSpec((n_slc, n_cmp), lambda bi, g, i: (0, 0)),
            _resident(pat.shape, lambda bi, g, i: (0, 0, 0)),
        ],
        out_specs=pl.BlockSpec((1, TQ, HPG * HEAD_DIM), lambda bi, g, i: (bi, i, g)),
        out_shape=jax.ShapeDtypeStruct((b, seq, NSA_WIDTH), BF16),
        scratch_shapes=[
            pltpu.VMEM((LANES, M_ATT), BF16),
            pltpu.VMEM((NK_SEL, M_ATT), F32),
            pltpu.VMEM((NK_SEL, M_ATT), BF16),
            pltpu.VMEM((1, M_ATT), F32),
            pltpu.VMEM((1, M_ATT), F32),
            pltpu.VMEM((1, M_ATT), F32),
            pltpu.VMEM((1, M_ATT), F32),
            pltpu.VMEM((V_ROWS, M_ATT), F32),
            pltpu.VMEM((V_ROWS, M_ATT), F32),
            pltpu.VMEM((V_ROWS, M_ATT), F32),
        ],
        compiler_params=pltpu.CompilerParams(
            dimension_semantics=("arbitrary", "arbitrary", "arbitrary"),
            vmem_limit_bytes=VMEM_LIMIT),
        name="nsa_attention",
    )(q3, gn3, kcmp, vcmp_t, ks4, vst, kw4, vwt, ovt, pat)


def _mlp_kernel(x_ref, yp_ref, yn_ref, gm_ref, wpp_ref, wpn_ref, wo_ref, nm_ref, w1_ref, w2_ref,
                nf_ref, o_ref, *, final):
    p1 = _dot(yp_ref[...], wpp_ref[...])
    p2 = _dot(yn_ref[...], wpn_ref[...])
    ga = jax.nn.sigmoid(gm_ref[:, 0:D_MODEL])
    gb = jax.nn.sigmoid(gm_ref[:, D_MODEL:2 * D_MODEL])
    merged = ga * p1 + gb * p2
    x = x_ref[...] + _dot(merged.astype(BF16), wo_ref[...])
    h = _rms(x, nm_ref[...]).astype(BF16)
    acc = jnp.zeros((TM_MLP, D_MODEL), F32)
    for c in range(D_FF // FF_CHUNK):
        sl = slice(c * FF_CHUNK, (c + 1) * FF_CHUNK)
        a = jnp.square(jnp.maximum(_dot(h, w1_ref[:, sl]), 0.0)).astype(BF16)
        acc = acc + _dot(a, w2_ref[sl, :])
    x = x + acc
    if final:
        x = _rms(x, nf_ref[...])
    o_ref[...] = x


def _merge_mlp(x2, yp, yn, gm, wpp, wpn, wo, nm, w1, w2, nf, layer, final):
    n = x2.shape[0]
    row = lambda w_: pl.BlockSpec((TM_MLP, w_), lambda i: (i, 0))
    res = lambda a: _resident(a.shape, lambda i: (0,) * a.ndim)
    lay = lambda a: _layer_resident(a, layer)
    return pl.pallas_call(
        functools.partial(_mlp_kernel, final=final),
        grid=(n // TM_MLP,),
        in_specs=[row(D_MODEL), row(POOL_WIDTH), row(NSA_WIDTH), row(2 * D_MODEL),
                  lay(wpp), lay(wpn), lay(wo), res(nm), lay(w1), lay(w2), res(nf)],
        out_specs=row(D_MODEL),
        out_shape=jax.ShapeDtypeStruct((n, D_MODEL), F32),
        compiler_params=pltpu.CompilerParams(
            dimension_semantics=("arbitrary",), vmem_limit_bytes=VMEM_LIMIT),
        name="merge_mlp",
    )(x2, yp, yn, gm, wpp, wpn, wo, nm, w1, w2, nf)


def _rope_tables(pos):
    inv = ROPE_THETA ** (-jnp.arange(0, HEAD_DIM, 2, dtype=F32) / HEAD_DIM)
    ang = pos.astype(F32)[:, None] * inv[None, :]
    ang = jnp.concatenate([ang, ang, ang, ang], axis=-1)
    first_half = (jnp.arange(LANES) % HEAD_DIM) < HEAD_DIM // 2
    cos, sin = jnp.cos(ang), jnp.sin(ang)
    return cos, jnp.where(first_half, -sin, 0.0), jnp.where(first_half, 0.0, sin)


def _permute_w_in(w_in):
    o_q = POOL_WIDTH
    o_kv = o_q + NSA_WIDTH
    o_gn = o_kv + 6 * KV_WIDTH
    o_gm = o_gn + N_GATE
    depth = w_in.shape[0]
    gn = w_in[:, :, o_gn:o_gm].reshape(depth, D_MODEL, N_KV_GROUPS, HPG, 3)
    gn = gn.transpose(0, 1, 2, 4, 3).reshape(depth, D_MODEL, N_KV_GROUPS, 3 * HPG)
    gn = jnp.pad(gn, ((0, 0), (0, 0), (0, 0), (0, LANES // N_KV_GROUPS - 3 * HPG)))
    gn = gn.reshape(depth, D_MODEL, LANES)
    return jnp.concatenate(
        [w_in[:, :, 0:o_gn], w_in[:, :, o_gm:], gn], axis=-1).astype(BF16)


def kernel(x, norm_mix, w_in, w_pool, pool_scale, pe_k, pe_v, w_ck1, w_ck2, w_cv1, w_cv2,
           w_proj_pool, w_proj_nsa, w_out, norm_mlp, w_ff1, w_ff2, norm_final):
    b, seq, d = x.shape
    depth = w_in.shape[0]
    n = b * seq
    n_chunks = seq // CMP_STRIDE
    n_slc = seq // SEL_BLOCK
    assert n_slc <= HEAD_DIM, "the selection one-hot shares the 64 spare key lanes"
    assert seq >= WIN_KEYS and seq % TM_IN == 0

    w_in_p = _permute_w_in(w_in)
    w_pool_b = w_pool.astype(BF16)
    pe = jnp.stack([pe_k, pe_v], axis=1).reshape(depth, 2, 1, CMP_LEN * HEAD_DIM)
    w_c1 = jnp.stack([w_ck1, w_cv1], axis=1).astype(BF16)
    pad = jnp.zeros_like(w_ck2)
    w_c2 = jnp.stack([jnp.concatenate([w, pad], axis=-1) for w in (w_ck2, w_ck2, w_cv2, w_cv2)],
                     axis=1).astype(BF16)
    wpp, wpn, wo = w_proj_pool.astype(BF16), w_proj_nsa.astype(BF16), w_out.astype(BF16)
    w1, w2 = w_ff1.astype(BF16), w_ff2.astype(BF16)

    cos, slo, shi = _rope_tables(jnp.arange(seq))
    ccos, cslo, cshi = _rope_tables(jnp.arange(n_chunks) * CMP_STRIDE + CMP_LEN - 1)
    ident = (jnp.ones_like(ccos), jnp.zeros_like(cslo), jnp.zeros_like(cshi))
    cmp_tabs = [jnp.stack([t, i_], axis=0) for t, i_ in zip((ccos, cslo, cshi), ident)]
    cmp_start = jnp.arange(n_chunks) * CMP_STRIDE
    slc_start = jnp.arange(n_slc) * SEL_BLOCK
    ovt = ((cmp_start[None, :] <= slc_start[:, None] + SEL_BLOCK - 1)
           & (cmp_start[None, :] + CMP_LEN - 1 >= slc_start[:, None])).astype(BF16)
    k_loc = jnp.arange(NK_SEL)[:, None]
    t_loc = jnp.arange(TQ)[None, :]
    keep = jnp.stack([jnp.ones((NK_SEL, TQ), bool), k_loc <= t_loc, k_loc > t_loc,
                      jnp.zeros((NK_SEL, TQ), bool)])
    pat = jnp.where(keep, 0.0, NEG_INF).astype(F32)

    x2 = x.reshape(n, d)
    for l in range(depth):
        u, q, c4, ks, vst, kw, vwt, gm, gn = _inproj(
            x2, norm_mix[l][None, :], w_in_p, l, cos, slo, shi, b, seq)
        y_pool = _pool(u.reshape(b, seq, POOL_WIDTH), w_pool_b[l], pool_scale[l][None, :])
        cmp_n, cmp_t = _compress(c4, pe[l], w_c1[l], w_c2[l], *cmp_tabs)
        y_nsa = _attention(
            q, gn, cmp_n, cmp_t,
            ks, vst, kw, vwt, ovt, pat)
        x2 = _merge_mlp(
            x2, y_pool.reshape(n, POOL_WIDTH), y_nsa.reshape(n, NSA_WIDTH), gm,
            wpp, wpn, wo, norm_mlp[l][None, :], w1, w2, norm_final[None, :],
            layer=l, final=(l == depth - 1))
    return x2.reshape(b, seq, d)
```

```python
import functools

import jax
import jax.numpy as jnp
import numpy as np
from jax import lax
from jax.experimental import pallas as pl
from jax.experimental.pallas import tpu as pltpu

F32 = jnp.float32
BF16 = jnp.bfloat16

D_MODEL = 1024
POOL_WINDOWS = (2, 4, 8, 16)
POOL_WIDTH = 512
POOL_GW = 128
N_HEADS = 16
HEAD_DIM = 64
N_KV_GROUPS = 2
HPG = 8
NSA_WIDTH = 1024
KV_WIDTH = 128
CMP_LEN = 32
CMP_STRIDE = 16
CMP_HIDDEN = 256
SEL_BLOCK = 64
N_SEL = 16
WINDOW = 512
SEL_BONUS = 1e4
NEG_INF = -1e30
ROPE_THETA = 10000.0
D_FF = 4096
RMS_EPS = 1e-6
N_GATE = 3 * N_HEADS
Q_SCALE = HEAD_DIM ** -0.5 * float(np.log2(np.e))

LANES = 128
VMEM_LIMIT = 56 * 1024 * 1024

C_U = 0
C_Q = C_U + POOL_WIDTH
C_KV = C_Q + NSA_WIDTH
C_GM = C_KV + 6 * KV_WIDTH
C_GN = C_GM + 2 * D_MODEL
N_INP = C_GN + LANES

TM_IN = 512
TS_POOL = 512
POOL_HALO = 16
TM_MLP = 512
FF_CHUNK = 1024
TQ = 256
NK_SEL = 256
WIN_KEYS = WINDOW + TQ
VT_CHUNK = 128
V_ROWS = HEAD_DIM + 16
PAT_ALL, PAT_CAUSAL, PAT_ABOVE, PAT_NONE = 0, 1, 2, 3
assert NK_SEL == TQ and WINDOW == 2 * NK_SEL and WIN_KEYS % NK_SEL == 0


def _dot(a, b):
    return jnp.dot(a, b, preferred_element_type=F32)


def _rms(x, g):
    return x * lax.rsqrt(jnp.mean(x * x, axis=-1, keepdims=True) + RMS_EPS) * g


def _rope(t, cos, sin_lo, sin_hi):
    return t * cos + pltpu.roll(t, LANES - 32, 1) * sin_lo + pltpu.roll(t, 32, 1) * sin_hi


def _resident(shape, index_map):
    return pl.BlockSpec(shape, index_map, pipeline_mode=pl.Buffered(1))


def _layer_resident(stacked, layer):
    nd = stacked.ndim - 1
    return _resident((None,) + stacked.shape[1:], lambda i: (layer,) + (0,) * nd)


def _value_rows(v_t):
    tail_row = lax.broadcasted_iota(jnp.int32, (V_ROWS - HEAD_DIM, v_t.shape[1]), 0)
    return jnp.concatenate([v_t, jnp.where(tail_row == 0, 1.0, 0.0)], axis=0)


def _inproj_kernel(x_ref, g_ref, w_ref, cos_ref, slo_ref, shi_ref,
                   u_ref, q_ref, cmp_ref, ks_ref, vst_ref, kw_ref, vwt_ref, gm_ref, gn_ref,
                   kv_scr, *, tiles_per_seq):
    h = _rms(x_ref[...], g_ref[...]).astype(BF16)
    cos, slo, shi = cos_ref[...], slo_ref[...], shi_ref[...]
    lane = lax.broadcasted_iota(jnp.int32, (TM_IN, LANES), 1)
    low = lane < HEAD_DIM
    pos = (pl.program_id(0) % tiles_per_seq) * TM_IN + lax.broadcasted_iota(
        jnp.int32, (TM_IN, LANES), 0)
    block_onehot = jnp.where(lane - HEAD_DIM == pos // SEL_BLOCK, 1.0, 0.0)

    u_ref[...] = _dot(h, w_ref[:, C_U:C_Q])
    q = _dot(h, w_ref[:, C_Q:C_KV])
    for k in range(NSA_WIDTH // LANES):
        qt = (_rope(q[:, k * LANES:(k + 1) * LANES], cos, slo, shi) * Q_SCALE).T
        spare = jnp.zeros((LANES - HEAD_DIM, TM_IN), F32)
        for par in range(2):
            q_ref[0, 2 * k + par] = jnp.concatenate(
                [qt[par * HEAD_DIM:(par + 1) * HEAD_DIM, :], spare], axis=0).astype(BF16)
    kv = _dot(h, w_ref[:, C_KV:C_GM])
    n_rows = TM_IN // CMP_STRIDE
    low_c = lax.broadcasted_iota(jnp.int32, (n_rows, LANES), 1) < HEAD_DIM
    for t in range(2):
        kv_scr[t] = kv[:, t * LANES:(t + 1) * LANES]
        for pp in range(CMP_STRIDE // 2):
            a = kv_scr[t, pl.ds(2 * pp, n_rows, stride=CMP_STRIDE), :]
            b = kv_scr[t, pl.ds(2 * pp + 1, n_rows, stride=CMP_STRIDE), :]
            sl = slice(pp * LANES, (pp + 1) * LANES)
            cmp_ref[0, 2 * t, :, sl] = jnp.where(low_c, a, pltpu.roll(b, HEAD_DIM, 1))
            cmp_ref[0, 2 * t + 1, :, sl] = jnp.where(low_c, pltpu.roll(a, HEAD_DIM, 1), b)
    ks = _rope(kv[:, 2 * KV_WIDTH:3 * KV_WIDTH], cos, slo, shi)
    kw = _rope(kv[:, 4 * KV_WIDTH:5 * KV_WIDTH], cos, slo, shi)
    vs_t = kv[:, 3 * KV_WIDTH:4 * KV_WIDTH].T
    vw_t = kv[:, 5 * KV_WIDTH:6 * KV_WIDTH].T
    for g in range(N_KV_GROUPS):
        ks_g = ks if g == 0 else pltpu.roll(ks, HEAD_DIM, 1)
        kw_g = kw if g == 0 else pltpu.roll(kw, HEAD_DIM, 1)
        ks_ref[0, g] = jnp.where(low, ks_g, block_onehot).astype(BF16)
        kw_ref[0, g] = jnp.where(low, kw_g, 0.0).astype(BF16)
        vs_g = _value_rows(vs_t[g * HEAD_DIM:(g + 1) * HEAD_DIM, :]).astype(BF16)
        vw_g = _value_rows(vw_t[g * HEAD_DIM:(g + 1) * HEAD_DIM, :]).astype(BF16)
        for c in range(TM_IN // VT_CHUNK):
            sl = slice(c * VT_CHUNK, (c + 1) * VT_CHUNK)
            vst_ref[0, g, c] = vs_g[:, sl]
            vwt_ref[0, g, c] = vw_g[:, sl]
    gm_ref[...] = _dot(h, w_ref[:, C_GM:C_GN])
    gn_ref[0] = jax.nn.sigmoid(_dot(h, w_ref[:, C_GN:N_INP])).T


def _inproj(x2, g, w, layer, cos, slo, shi, b, seq):
    n = x2.shape[0]
    tiles_per_seq = seq // TM_IN
    n_chunks = TM_IN // VT_CHUNK
    row = lambda w_: pl.BlockSpec((TM_IN, w_), lambda i: (i, 0))
    tab = pl.BlockSpec((TM_IN, LANES), lambda i: (i % tiles_per_seq, 0))
    kg = pl.BlockSpec((1, N_KV_GROUPS, TM_IN, LANES),
                      lambda i: (i // tiles_per_seq, 0, i % tiles_per_seq, 0))
    vt = pl.BlockSpec((1, N_KV_GROUPS, n_chunks, V_ROWS, VT_CHUNK),
                      lambda i: (i // tiles_per_seq, 0, i % tiles_per_seq, 0, 0))
    cmp_spec = pl.BlockSpec(
        (1, 2 * N_KV_GROUPS, TM_IN // CMP_STRIDE, CMP_STRIDE * HEAD_DIM),
        lambda i: (i // tiles_per_seq, 0, i % tiles_per_seq, 0))
    qt_spec = pl.BlockSpec((1, N_HEADS, LANES, TM_IN),
                           lambda i: (i // tiles_per_seq, 0, 0, i % tiles_per_seq))
    gt_spec = pl.BlockSpec((1, LANES, TM_IN), lambda i: (i // tiles_per_seq, 0, i % tiles_per_seq))
    k_shape = jax.ShapeDtypeStruct((b, N_KV_GROUPS, seq, LANES), BF16)
    vt_shape = jax.ShapeDtypeStruct((b, N_KV_GROUPS, seq // VT_CHUNK, V_ROWS, VT_CHUNK), BF16)
    return pl.pallas_call(
        functools.partial(_inproj_kernel, tiles_per_seq=tiles_per_seq),
        grid=(n // TM_IN,),
        in_specs=[row(D_MODEL), _resident((1, D_MODEL), lambda i: (0, 0)),
                  _layer_resident(w, layer), tab, tab, tab],
        out_specs=[row(POOL_WIDTH), qt_spec, cmp_spec, kg, vt,
                   kg, vt, row(2 * D_MODEL), gt_spec],
        out_shape=[
            jax.ShapeDtypeStruct((n, POOL_WIDTH), F32),
            jax.ShapeDtypeStruct((b, N_HEADS, LANES, seq), BF16),
            jax.ShapeDtypeStruct(
                (b, 2 * N_KV_GROUPS, seq // CMP_STRIDE, CMP_STRIDE * HEAD_DIM), F32),
            k_shape, vt_shape, k_shape, vt_shape,
            jax.ShapeDtypeStruct((n, 2 * D_MODEL), F32),
            jax.ShapeDtypeStruct((b, LANES, seq), F32),
        ],
        scratch_shapes=[pltpu.VMEM((2, TM_IN, LANES), F32)],
        compiler_params=pltpu.CompilerParams(
            dimension_semantics=("arbitrary",), vmem_limit_bytes=VMEM_LIMIT),
        name="in_proj",
    )(x2, g, w, cos, slo, shi)


def _pool_kernel(u_ref, up_ref, wp_ref, sc_ref, o_ref):
    i = pl.program_id(1)
    cur = u_ref[0]
    prev = jnp.where(i > 0, up_ref[0], 0.0)
    t = i * TS_POOL + lax.broadcasted_iota(jnp.int32, (TS_POOL, POOL_GW), 0)
    for g, w in enumerate(POOL_WINDOWS):
        sl = slice(g * POOL_GW, (g + 1) * POOL_GW)
        cg = cur[:, sl]
        s = jnp.concatenate([prev[:, sl], cg], axis=0)
        sh = 1
        while sh < w:
            s = s + pltpu.roll(s, sh, 0)
            sh *= 2
        cnt = jnp.minimum(t + 1, w).astype(F32)
        d = s[POOL_HALO:] / cnt - cg
        y = _dot(d.astype(BF16), wp_ref[g]) * sc_ref[:, sl]
        o_ref[0, :, sl] = y.astype(BF16)


def _pool(u3, wp, sc):
    b, seq, _ = u3.shape
    halo_per_tile = TS_POOL // POOL_HALO
    return pl.pallas_call(
        _pool_kernel,
        grid=(b, seq // TS_POOL),
        in_specs=[
            pl.BlockSpec((1, TS_POOL, POOL_WIDTH), lambda bi, i: (bi, i, 0)),
            pl.BlockSpec((1, POOL_HALO, POOL_WIDTH),
                         lambda bi, i: (bi, jnp.maximum(i * halo_per_tile - 1, 0), 0)),
            pl.BlockSpec((len(POOL_WINDOWS), POOL_GW, POOL_GW), lambda bi, i: (0, 0, 0)),
            pl.BlockSpec((1, POOL_WIDTH), lambda bi, i: (0, 0)),
        ],
        out_specs=pl.BlockSpec((1, TS_POOL, POOL_WIDTH), lambda bi, i: (bi, i, 0)),
        out_shape=jax.ShapeDtypeStruct((b, seq, POOL_WIDTH), BF16),
        compiler_params=pltpu.CompilerParams(dimension_semantics=("arbitrary", "arbitrary")),
        name="pool_mixer",
    )(u3, u3, wp, sc)


def _compress_kernel(c_ref, pe_ref, w1_ref, w2_ref, cos_ref, slo_ref, shi_ref, o_ref, ot_ref):
    half = CMP_STRIDE * HEAD_DIM
    c = c_ref[...]
    a = _dot((c + pe_ref[:, 0:half]).astype(BF16), w1_ref[0:half, :])
    b = _dot((c + pe_ref[:, half:2 * half]).astype(BF16), w1_ref[half:2 * half, :])
    n_rows = c.shape[0]
    hid = a + pltpu.roll(b, n_rows - 1, 0)
    act = jax.nn.gelu(hid, approximate=True)
    out = _dot(act.astype(BF16), w2_ref[...])
    out = _rope(out, cos_ref[...], slo_ref[...], shi_ref[...])
    row = lax.broadcasted_iota(jnp.int32, out.shape, 0)
    out = jnp.where(row < n_rows - 1, out, 0.0)
    o_ref[...] = out.astype(BF16)
    ot_ref[...] = _value_rows(out.T[0:HEAD_DIM, :]).astype(BF16)


def _compress(c4, pe, w1, w2p, cos, slo, shi):
    b, n_kv, n_chunks, half = c4.shape
    return pl.pallas_call(
        _compress_kernel,
        grid=(b, n_kv),
        in_specs=[
            pl.BlockSpec((None, None, n_chunks, half), lambda bi, j: (bi, j, 0, 0)),
            pl.BlockSpec((None, 1, 2 * half), lambda bi, j: (j // N_KV_GROUPS, 0, 0)),
            pl.BlockSpec((None, 2 * half, CMP_HIDDEN), lambda bi, j: (j // N_KV_GROUPS, 0, 0)),
            pl.BlockSpec((None, CMP_HIDDEN, LANES), lambda bi, j: (j, 0, 0)),
            pl.BlockSpec((None, n_chunks, LANES), lambda bi, j: (j // N_KV_GROUPS, 0, 0)),
            pl.BlockSpec((None, n_chunks, LANES), lambda bi, j: (j // N_KV_GROUPS, 0, 0)),
            pl.BlockSpec((None, n_chunks, LANES), lambda bi, j: (j // N_KV_GROUPS, 0, 0)),
        ],
        out_specs=[
            pl.BlockSpec((None, None, n_chunks, LANES), lambda bi, j: (bi, j, 0, 0)),
            pl.BlockSpec((None, None, V_ROWS, n_chunks), lambda bi, j: (bi, j, 0, 0)),
        ],
        out_shape=[
            jax.ShapeDtypeStruct((b, n_kv, n_chunks, LANES), BF16),
            jax.ShapeDtypeStruct((b, n_kv, V_ROWS, n_chunks), BF16),
        ],
        compiler_params=pltpu.CompilerParams(dimension_semantics=("arbitrary", "arbitrary")),
        name="compress",
    )(c4, pe, w1, w2p, cos, slo, shi)


def _block_rank(score):
    n_slc = score.shape[0]
    sub = 8
    ranks = []
    for v in range(n_slc // sub):
        blk = score[v * sub:(v + 1) * sub, :]
        jb_v = v * sub + lax.broadcasted_iota(jnp.int32, blk.shape, 0)
        r = jnp.zeros(blk.shape, F32)
        for jp in range(n_slc):
            row = score[jp:jp + 1, :]
            ge = jnp.where(row >= blk, 1.0, 0.0)
            gt = jnp.where(row > blk, 1.0, 0.0)
            if jp < v * sub:
                r = r + ge
            elif jp >= (v + 1) * sub:
                r = r + gt
            else:
                r = r + jnp.where(jb_v > jp, ge, gt)
        ranks.append(r)
    return jnp.concatenate(ranks, axis=0)


def _attn_kernel(q_ref, gn_ref, kc_ref, vct_ref, ks_ref, vst_ref, kw_ref, vwt_ref, ovt_ref,
                 pat_ref, o_ref, qsel_ref, s_ref, p_ref, al_ref, mc_ref, mw_ref, ms_ref,
                 accc_ref, accw_ref, accs_ref):
    g = pl.program_id(1)
    i = pl.program_id(2)
    s0 = i * TQ
    n_cmp = kc_ref.shape[0]
    n_slc = ovt_ref.shape[0]
    n_ch = HPG

    q_plain = lambda ch: q_ref[0, ch]
    q_selected = lambda ch: qsel_ref[ch]

    g_base = g * HEAD_DIM

    def gate_rows(c, ch):
        return gn_ref[0, pl.ds(g_base + c * HPG + ch, 1), :]

    def key_tile(ref, k0, nk):
        return ref[pl.ds(pl.multiple_of(k0, VT_CHUNK), nk), :]

    def value_tile(ref, k0, nk):
        c0 = k0 // VT_CHUNK
        return jnp.concatenate([ref[c0 + c] for c in range(nk // VT_CHUNK)], axis=1)

    imp_raw = []

    def run_step(soft=None, score=None, value=None, importance=False):
        for ch in range(n_ch):
            if soft is not None:
                nk, m_ref, first = soft
                s = s_ref[ch, 0:nk, :]
                mx = jnp.max(s, axis=0, keepdims=True)
                if first:
                    m_new, al_new = mx, None
                else:
                    m_prev = m_ref[ch]
                    m_new = jnp.maximum(m_prev, mx)
                    al_new = jnp.exp2(m_prev - m_new)
                p_new = jnp.exp2(s - m_new).astype(BF16)
            if score is not None:
                kt, q_chunk, bias = score
                sc = _dot(kt, q_chunk(ch))
                s_ref[ch, 0:kt.shape[0], :] = sc if bias is None else sc + bias()
            if value is not None:
                vt, acc_ref, vfirst = value
                pv = _dot(vt, p_ref[ch, 0:vt.shape[1], :])
                acc_ref[ch] = pv if vfirst else al_ref[ch] * acc_ref[ch] + pv
            if importance:
                imp_raw.append(_dot(ovt_ref[...], p_ref[ch, 0:n_cmp, :]))
            if soft is not None:
                p_ref[ch, 0:nk, :] = p_new
                m_ref[ch] = m_new
                if al_new is not None:
                    al_ref[ch] = al_new

    n_idx = lax.broadcasted_iota(jnp.int32, (n_cmp, TQ), 0)
    t_cmp = s0 + lax.broadcasted_iota(jnp.int32, (n_cmp, TQ), 1)
    cmp_mask = jnp.where(n_idx * CMP_STRIDE + CMP_LEN - 1 <= t_cmp, 0.0, NEG_INF)
    cmp_bias = lambda: cmp_mask

    w0 = jnp.maximum(s0 - WINDOW, 0)
    win_tiles = [(off, min(NK_SEL, WIN_KEYS - off)) for off in range(0, WIN_KEYS, NK_SEL)]
    n_win = len(win_tiles)

    def win_bias(off, nk):
        d = w0 + off - s0
        pat = jnp.where(d == -WINDOW, PAT_ABOVE,
                        jnp.where(d == 0, PAT_CAUSAL, jnp.where(d < 0, PAT_ALL, PAT_NONE)))
        return lambda: pat_ref[pat]

    def win_step(k):
        args = {}
        if k < n_win:
            off, nk = win_tiles[k]
            args["score"] = (key_tile(kw_ref, w0 + off, nk), q_plain, win_bias(off, nk))
        if 1 <= k <= n_win:
            args["soft"] = (win_tiles[k - 1][1], mw_ref, k == 1)
        if 2 <= k <= n_win + 1:
            off, nk = win_tiles[k - 2]
            args["value"] = (value_tile(vwt_ref, w0 + off, nk), accw_ref, k == 2)
        return args

    run_step(score=(kc_ref[...], q_plain, cmp_bias))
    run_step(soft=(n_cmp, mc_ref, True), **win_step(0))
    run_step(value=(vct_ref[...], accc_ref, True), importance=True, **win_step(1))
    for k in range(2, n_win):
        run_step(**win_step(k))

    cmp_scale = [jnp.where(mc_ref[ch] > 0.5 * NEG_INF,
                           1.0 / accc_ref[ch, HEAD_DIM:HEAD_DIM + 1, :], 0.0)
                 for ch in range(n_ch)]

    imp = functools.reduce(lambda a, b: a + b, [r * sc for r, sc in zip(imp_raw, cmp_scale)])
    jb = lax.broadcasted_iota(jnp.int32, (n_slc, TQ), 0)
    tq = s0 + lax.broadcasted_iota(jnp.int32, (n_slc, TQ), 1)
    causal = jb * SEL_BLOCK <= tq
    near = jnp.logical_or(jb == 0, jb >= tq // SEL_BLOCK - 1)
    score = jnp.where(causal, jnp.where(near, SEL_BONUS, imp), NEG_INF)
    rank = jnp.concatenate(
        [_block_rank(score[:, c0:c0 + LANES]) for c0 in range(0, TQ, LANES)], axis=1)
    sel_bias = jnp.where(causal, jnp.where(rank < float(N_SEL), 0.0, NEG_INF), NEG_INF)
    parts = [jnp.zeros((HEAD_DIM, TQ), F32), sel_bias]
    if n_slc < HEAD_DIM:
        parts.append(jnp.zeros((HEAD_DIM - n_slc, TQ), F32))
    sel_rows = jnp.concatenate(parts, axis=0).astype(BF16)
    for hh in range(HPG):
        qsel_ref[hh] = q_ref[0, hh] + sel_rows

    ms_ref[...] = jnp.full(ms_ref.shape, NEG_INF, F32)
    accs_ref[...] = jnp.zeros(accs_ref.shape, F32)
    n_full = s0 // NK_SEL

    def sel_tile(m):
        return jnp.where(m == 0, n_full, m - 1)

    def sel_scores(m, bias=None):
        return (key_tile(ks_ref, sel_tile(m) * NK_SEL, NK_SEL), q_selected, bias)

    def sel_values(m, live):
        vt = value_tile(vst_ref, sel_tile(jnp.maximum(m, 0)) * NK_SEL, NK_SEL)
        return jnp.where(live, vt, jnp.zeros_like(vt))

    run_step(score=sel_scores(0, lambda: pat_ref[PAT_CAUSAL]), **win_step(n_win))
    run_step(**win_step(n_win + 1))

    def sel_body(m, carry):
        run_step(soft=(NK_SEL, ms_ref, False), score=sel_scores(m),
                 value=(sel_values(m - 2, m >= 2), accs_ref, False))
        return carry

    lax.fori_loop(1, n_full + 1, sel_body, 0)
    run_step(soft=(NK_SEL, ms_ref, False),
             value=(sel_values(n_full - 1, n_full >= 1), accs_ref, False))
    run_step(value=(sel_values(n_full, True), accs_ref, False))

    heads = []
    for ch in range(n_ch):
        coef_c = gate_rows(0, ch) * cmp_scale[ch]
        coef_w = gate_rows(2, ch) / accw_ref[ch, HEAD_DIM:HEAD_DIM + 1, :]
        coef_s = gate_rows(1, ch) / accs_ref[ch, HEAD_DIM:HEAD_DIM + 1, :]
        heads.append(accc_ref[ch, 0:HEAD_DIM, :] * coef_c + accw_ref[ch, 0:HEAD_DIM, :] * coef_w
                     + accs_ref[ch, 0:HEAD_DIM, :] * coef_s)
    for pair in range(HPG // 2):
        both = jnp.concatenate(heads[2 * pair:2 * pair + 2], axis=0)
        o_ref[0, :, pair * LANES:(pair + 1) * LANES] = both.T.astype(BF16)


def _attention(q3, gn3, kcmp, vcmp_t, ks4, vst, kw4, vwt, ovt, pat):
    b, _, _, seq = q3.shape
    n_cmp = kcmp.shape[2]
    n_slc = seq // SEL_BLOCK
    chunks_per_seq = seq // VT_CHUNK
    per_group = lambda bi, g, i: (bi, g, 0, 0)
    return pl.pallas_call(
        _attn_kernel,
        grid=(b, N_KV_GROUPS, seq // TQ),
        in_specs=[
            pl.BlockSpec((1, HPG, LANES, TQ), lambda bi, g, i: (bi, g, 0, i)),
            pl.BlockSpec((1, LANES, TQ), lambda bi, g, i: (bi, 0, i)),
            pl.BlockSpec((None, None, n_cmp, LANES), per_group),
            pl.BlockSpec((None, None, V_ROWS, n_cmp),
                         lambda bi, g, i: (bi, N_KV_GROUPS + g, 0, 0)),
            pl.BlockSpec((None, None, seq, LANES), per_group),
            pl.BlockSpec((None, None, chunks_per_seq, V_ROWS, VT_CHUNK),
                         lambda bi, g, i: (bi, g, 0, 0, 0)),
            pl.BlockSpec((None, None, seq, LANES), per_group),
            pl.BlockSpec((None, None, chunks_per_seq, V_ROWS, VT_CHUNK),
                         lambda bi, g, i: (bi, g, 0, 0, 0)),
            pl.BlockSpec((n_slc, n_cmp), lambda bi, g, i: (0, 0)),
            _resident(pat.shape, lambda bi, g, i: (0, 0, 0)),
        ],
        out_specs=pl.BlockSpec((1, TQ, HPG * HEAD_DIM), lambda bi, g, i: (bi, i, g)),
        out_shape=jax.ShapeDtypeStruct((b, seq, NSA_WIDTH), BF16),
        scratch_shapes=[
            pltpu.VMEM((HPG, LANES, TQ), BF16),
            pltpu.VMEM((HPG, NK_SEL, TQ), F32),
            pltpu.VMEM((HPG, NK_SEL, TQ), BF16),
            pltpu.VMEM((HPG, 1, TQ), F32),
            pltpu.VMEM((HPG, 1, TQ), F32),
            pltpu.VMEM((HPG, 1, TQ), F32),
            pltpu.VMEM((HPG, 1, TQ), F32),
            pltpu.VMEM((HPG, V_ROWS, TQ), F32),
            pltpu.VMEM((HPG, V_ROWS, TQ), F32),
            pltpu.VMEM((HPG, V_ROWS, TQ), F32),
        ],
        compiler_params=pltpu.CompilerParams(
            dimension_semantics=("arbitrary", "arbitrary", "arbitrary"),
            vmem_limit_bytes=VMEM_LIMIT),
        name="nsa_attention",
    )(q3, gn3, kcmp, vcmp_t, ks4, vst, kw4, vwt, ovt, pat)


def _mlp_kernel(x_ref, yp_ref, yn_ref, gm_ref, wpp_ref, wpn_ref, wo_ref, nm_ref, w1_ref, w2_ref,
                nf_ref, o_ref, *, final):
    p1 = _dot(yp_ref[...], wpp_ref[...])
    p2 = _dot(yn_ref[...], wpn_ref[...])
    ga = jax.nn.sigmoid(gm_ref[:, 0:D_MODEL])
    gb = jax.nn.sigmoid(gm_ref[:, D_MODEL:2 * D_MODEL])
    merged = ga * p1 + gb * p2
    x = x_ref[...] + _dot(merged.astype(BF16), wo_ref[...])
    h = _rms(x, nm_ref[...]).astype(BF16)
    acc = jnp.zeros((TM_MLP, D_MODEL), F32)
    for c in range(D_FF // FF_CHUNK):
        sl = slice(c * FF_CHUNK, (c + 1) * FF_CHUNK)
        a = jnp.square(jnp.maximum(_dot(h, w1_ref[:, sl]), 0.0)).astype(BF16)
        acc = acc + _dot(a, w2_ref[sl, :])
    x = x + acc
    if final:
        x = _rms(x, nf_ref[...])
    o_ref[...] = x


def _merge_mlp(x2, yp, yn, gm, wpp, wpn, wo, nm, w1, w2, nf, layer, final):
    n = x2.shape[0]
    row = lambda w_: pl.BlockSpec((TM_MLP, w_), lambda i: (i, 0))
    res = lambda a: _resident(a.shape, lambda i: (0,) * a.ndim)
    lay = lambda a: _layer_resident(a, layer)
    return pl.pallas_call(
        functools.partial(_mlp_kernel, final=final),
        grid=(n // TM_MLP,),
        in_specs=[row(D_MODEL), row(POOL_WIDTH), row(NSA_WIDTH), row(2 * D_MODEL),
                  lay(wpp), lay(wpn), lay(wo), res(nm), lay(w1), lay(w2), res(nf)],
        out_specs=row(D_MODEL),
        out_shape=jax.ShapeDtypeStruct((n, D_MODEL), F32),
        compiler_params=pltpu.CompilerParams(
            dimension_semantics=("arbitrary",), vmem_limit_bytes=VMEM_LIMIT),
        name="merge_mlp",
    )(x2, yp, yn, gm, wpp, wpn, wo, nm, w1, w2, nf)


def _rope_tables(pos):
    inv = ROPE_THETA ** (-jnp.arange(0, HEAD_DIM, 2, dtype=F32) / HEAD_DIM)
    ang = pos.astype(F32)[:, None] * inv[None, :]
    ang = jnp.concatenate([ang, ang, ang, ang], axis=-1)
    first_half = (jnp.arange(LANES) % HEAD_DIM) < HEAD_DIM // 2
    cos, sin = jnp.cos(ang), jnp.sin(ang)
    return cos, jnp.where(first_half, -sin, 0.0), jnp.where(first_half, 0.0, sin)


def _permute_w_in(w_in):
    o_q = POOL_WIDTH
    o_kv = o_q + NSA_WIDTH
    o_gn = o_kv + 6 * KV_WIDTH
    o_gm = o_gn + N_GATE
    depth = w_in.shape[0]
    gn = w_in[:, :, o_gn:o_gm].reshape(depth, D_MODEL, N_KV_GROUPS, HPG, 3)
    gn = gn.transpose(0, 1, 2, 4, 3).reshape(depth, D_MODEL, N_KV_GROUPS, 3 * HPG)
    gn = jnp.pad(gn, ((0, 0), (0, 0), (0, 0), (0, LANES // N_KV_GROUPS - 3 * HPG)))
    gn = gn.reshape(depth, D_MODEL, LANES)
    return jnp.concatenate(
        [w_in[:, :, 0:o_gn], w_in[:, :, o_gm:], gn], axis=-1).astype(BF16)


def kernel(x, norm_mix, w_in, w_pool, pool_scale, pe_k, pe_v, w_ck1, w_ck2, w_cv1, w_cv2,
           w_proj_pool, w_proj_nsa, w_out, norm_mlp, w_ff1, w_ff2, norm_final):
    b, seq, d = x.shape
    depth = w_in.shape[0]
    n = b * seq
    n_chunks = seq // CMP_STRIDE
    n_slc = seq // SEL_BLOCK
    assert n_slc <= HEAD_DIM, "the selection one-hot shares the 64 spare key lanes"
    assert seq >= WIN_KEYS and seq % TM_IN == 0

    w_in_p = _permute_w_in(w_in)
    w_pool_b = w_pool.astype(BF16)
    pe = jnp.stack([pe_k, pe_v], axis=1).reshape(depth, 2, 1, CMP_LEN * HEAD_DIM)
    w_c1 = jnp.stack([w_ck1, w_cv1], axis=1).astype(BF16)
    pad = jnp.zeros_like(w_ck2)
    w_c2 = jnp.stack([jnp.concatenate([w, pad], axis=-1) for w in (w_ck2, w_ck2, w_cv2, w_cv2)],
                     axis=1).astype(BF16)
    wpp, wpn, wo = w_proj_pool.astype(BF16), w_proj_nsa.astype(BF16), w_out.astype(BF16)
    w1, w2 = w_ff1.astype(BF16), w_ff2.astype(BF16)

    cos, slo, shi = _rope_tables(jnp.arange(seq))
    ccos, cslo, cshi = _rope_tables(jnp.arange(n_chunks) * CMP_STRIDE + CMP_LEN - 1)
    ident = (jnp.ones_like(ccos), jnp.zeros_like(cslo), jnp.zeros_like(cshi))
    cmp_tabs = [jnp.stack([t, i_], axis=0) for t, i_ in zip((ccos, cslo, cshi), ident)]
    cmp_start = jnp.arange(n_chunks) * CMP_STRIDE
    slc_start = jnp.arange(n_slc) * SEL_BLOCK
    ovt = ((cmp_start[None, :] <= slc_start[:, None] + SEL_BLOCK - 1)
           & (cmp_start[None, :] + CMP_LEN - 1 >= slc_start[:, None])).astype(BF16)
    k_loc = jnp.arange(NK_SEL)[:, None]
    t_loc = jnp.arange(TQ)[None, :]
    keep = jnp.stack([jnp.ones((NK_SEL, TQ), bool), k_loc <= t_loc, k_loc > t_loc,
                      jnp.zeros((NK_SEL, TQ), bool)])
    pat = jnp.where(keep, 0.0, NEG_INF).astype(F32)

    x2 = x.reshape(n, d)
    for l in range(depth):
        u, q, c4, ks, vst, kw, vwt, gm, gn = _inproj(
            x2, norm_mix[l][None, :], w_in_p, l, cos, slo, shi, b, seq)
        y_pool = _pool(u.reshape(b, seq, POOL_WIDTH), w_pool_b[l], pool_scale[l][None, :])
        cmp_n, cmp_t = _compress(c4, pe[l], w_c1[l], w_c2[l], *cmp_tabs)
        y_nsa = _attention(
            q, gn, cmp_n, cmp_t,
            ks, vst, kw, vwt, ovt, pat)
        x2 = _merge_mlp(
            x2, y_pool.reshape(n, POOL_WIDTH), y_nsa.reshape(n, NSA_WIDTH), gm,
            wpp, wpn, wo, norm_mlp[l][None, :], w1, w2, norm_final[None, :],
            layer=l, final=(l == depth - 1))
    return x2.reshape(b, seq, d)
```

```python
import functools

import jax
import jax.numpy as jnp
import numpy as np
from jax import lax
from jax.experimental import pallas as pl
from jax.experimental.pallas import tpu as pltpu

F32 = jnp.float32
BF16 = jnp.bfloat16

D_MODEL = 1024
POOL_WINDOWS = (2, 4, 8, 16)
POOL_WIDTH = 512
POOL_GW = 128
N_HEADS = 16
HEAD_DIM = 64
N_KV_GROUPS = 2
HPG = 8
NSA_WIDTH = 1024
KV_WIDTH = 128
CMP_LEN = 32
CMP_STRIDE = 16
CMP_HIDDEN = 256
SEL_BLOCK = 64
N_SEL = 16
WINDOW = 512
SEL_BONUS = 1e4
NEG_INF = -1e30
ROPE_THETA = 10000.0
D_FF = 4096
RMS_EPS = 1e-6
N_GATE = 3 * N_HEADS
Q_SCALE = HEAD_DIM ** -0.5 * float(np.log2(np.e))

LANES = 128
VMEM_LIMIT = 56 * 1024 * 1024

C_U = 0
C_Q = C_U + POOL_WIDTH
C_KV = C_Q + NSA_WIDTH
C_GM = C_KV + 6 * KV_WIDTH
C_GN = C_GM + 2 * D_MODEL
N_INP = C_GN + LANES

TM_IN = 512
TS_POOL = 512
POOL_HALO = 16
TM_MLP = 512
FF_CHUNK = 1024
TQ = 256
NK_SEL = 256
WIN_KEYS = WINDOW + TQ
VT_CHUNK = 128
V_ROWS = HEAD_DIM + 16
PAT_ALL, PAT_CAUSAL, PAT_ABOVE, PAT_NONE = 0, 1, 2, 3
assert NK_SEL == TQ and WINDOW == 2 * NK_SEL and WIN_KEYS % NK_SEL == 0


def _dot(a, b):
    return jnp.dot(a, b, preferred_element_type=F32)


def _rms(x, g):
    return x * lax.rsqrt(jnp.mean(x * x, axis=-1, keepdims=True) + RMS_EPS) * g


def _rope(t, cos, sin_lo, sin_hi):
    return t * cos + pltpu.roll(t, LANES - 32, 1) * sin_lo + pltpu.roll(t, 32, 1) * sin_hi


def _resident(shape, index_map):
    return pl.BlockSpec(shape, index_map, pipeline_mode=pl.Buffered(1))


def _layer_resident(stacked, layer):
    nd = stacked.ndim - 1
    return _resident((None,) + stacked.shape[1:], lambda i: (layer,) + (0,) * nd)


def _value_rows(v_t):
    tail_row = lax.broadcasted_iota(jnp.int32, (V_ROWS - HEAD_DIM, v_t.shape[1]), 0)
    return jnp.concatenate([v_t, jnp.where(tail_row == 0, 1.0, 0.0)], axis=0)


def _inproj_kernel(x_ref, g_ref, w_ref, cos_ref, slo_ref, shi_ref,
                   u_ref, q_ref, cmp_ref, ks_ref, vst_ref, kw_ref, vwt_ref, gm_ref, gn_ref,
                   kv_scr, *, tiles_per_seq):
    h = _rms(x_ref[...], g_ref[...]).astype(BF16)
    cos, slo, shi = cos_ref[...], slo_ref[...], shi_ref[...]
    lane = lax.broadcasted_iota(jnp.int32, (TM_IN, LANES), 1)
    low = lane < HEAD_DIM
    pos = (pl.program_id(0) % tiles_per_seq) * TM_IN + lax.broadcasted_iota(
        jnp.int32, (TM_IN, LANES), 0)
    block_onehot = jnp.where(lane - HEAD_DIM == pos // SEL_BLOCK, 1.0, 0.0)

    u_ref[...] = _dot(h, w_ref[:, C_U:C_Q])
    q = _dot(h, w_ref[:, C_Q:C_KV])
    for k in range(NSA_WIDTH // LANES):
        qt = (_rope(q[:, k * LANES:(k + 1) * LANES], cos, slo, shi) * Q_SCALE).T
        spare = jnp.zeros((LANES - HEAD_DIM, TM_IN), F32)
        for par in range(2):
            q_ref[0, 2 * k + par] = jnp.concatenate(
                [qt[par * HEAD_DIM:(par + 1) * HEAD_DIM, :], spare], axis=0).astype(BF16)
    kv = _dot(h, w_ref[:, C_KV:C_GM])
    n_rows = TM_IN // CMP_STRIDE
    low_c = lax.broadcasted_iota(jnp.int32, (n_rows, LANES), 1) < HEAD_DIM
    for t in range(2):
        kv_scr[t] = kv[:, t * LANES:(t + 1) * LANES]
        for pp in range(CMP_STRIDE // 2):
            a = kv_scr[t, pl.ds(2 * pp, n_rows, stride=CMP_STRIDE), :]
            b = kv_scr[t, pl.ds(2 * pp + 1, n_rows, stride=CMP_STRIDE), :]
            sl = slice(pp * LANES, (pp + 1) * LANES)
            cmp_ref[0, 2 * t, :, sl] = jnp.where(low_c, a, pltpu.roll(b, HEAD_DIM, 1))
            cmp_ref[0, 2 * t + 1, :, sl] = jnp.where(low_c, pltpu.roll(a, HEAD_DIM, 1), b)
    ks = _rope(kv[:, 2 * KV_WIDTH:3 * KV_WIDTH], cos, slo, shi)
    kw = _rope(kv[:, 4 * KV_WIDTH:5 * KV_WIDTH], cos, slo, shi)
    vs_t = kv[:, 3 * KV_WIDTH:4 * KV_WIDTH].T
    vw_t = kv[:, 5 * KV_WIDTH:6 * KV_WIDTH].T
    for g in range(N_KV_GROUPS):
        ks_g = ks if g == 0 else pltpu.roll(ks, HEAD_DIM, 1)
        kw_g = kw if g == 0 else pltpu.roll(kw, HEAD_DIM, 1)
        ks_ref[0, g] = jnp.where(low, ks_g, block_onehot).astype(BF16)
        kw_ref[0, g] = jnp.where(low, kw_g, 0.0).astype(BF16)
        vs_g = _value_rows(vs_t[g * HEAD_DIM:(g + 1) * HEAD_DIM, :]).astype(BF16)
        vw_g = _value_rows(vw_t[g * HEAD_DIM:(g + 1) * HEAD_DIM, :]).astype(BF16)
        for c in range(TM_IN // VT_CHUNK):
            sl = slice(c * VT_CHUNK, (c + 1) * VT_CHUNK)
            vst_ref[0, g, c] = vs_g[:, sl]
            vwt_ref[0, g, c] = vw_g[:, sl]
    gm_ref[...] = _dot(h, w_ref[:, C_GM:C_GN])
    gn_ref[0] = jax.nn.sigmoid(_dot(h, w_ref[:, C_GN:N_INP])).T


def _inproj(x2, g, w, layer, cos, slo, shi, b, seq):
    n = x2.shape[0]
    tiles_per_seq = seq // TM_IN
    n_chunks = TM_IN // VT_CHUNK
    row = lambda w_: pl.BlockSpec((TM_IN, w_), lambda i: (i, 0))
    tab = pl.BlockSpec((TM_IN, LANES), lambda i: (i % tiles_per_seq, 0))
    kg = pl.BlockSpec((1, N_KV_GROUPS, TM_IN, LANES),
                      lambda i: (i // tiles_per_seq, 0, i % tiles_per_seq, 0))
    vt = pl.BlockSpec((1, N_KV_GROUPS, n_chunks, V_ROWS, VT_CHUNK),
                      lambda i: (i // tiles_per_seq, 0, i % tiles_per_seq, 0, 0))
    cmp_spec = pl.BlockSpec(
        (1, 2 * N_KV_GROUPS, TM_IN // CMP_STRIDE, CMP_STRIDE * HEAD_DIM),
        lambda i: (i // tiles_per_seq, 0, i % tiles_per_seq, 0))
    qt_spec = pl.BlockSpec((1, N_HEADS, LANES, TM_IN),
                           lambda i: (i // tiles_per_seq, 0, 0, i % tiles_per_seq))
    gt_spec = pl.BlockSpec((1, LANES, TM_IN), lambda i: (i // tiles_per_seq, 0, i % tiles_per_seq))
    k_shape = jax.ShapeDtypeStruct((b, N_KV_GROUPS, seq, LANES), BF16)
    vt_shape = jax.ShapeDtypeStruct((b, N_KV_GROUPS, seq // VT_CHUNK, V_ROWS, VT_CHUNK), BF16)
    return pl.pallas_call(
        functools.partial(_inproj_kernel, tiles_per_seq=tiles_per_seq),
        grid=(n // TM_IN,),
        in_specs=[row(D_MODEL), _resident((1, D_MODEL), lambda i: (0, 0)),
                  _layer_resident(w, layer), tab, tab, tab],
        out_specs=[row(POOL_WIDTH), qt_spec, cmp_spec, kg, vt,
                   kg, vt, row(2 * D_MODEL), gt_spec],
        out_shape=[
            jax.ShapeDtypeStruct((n, POOL_WIDTH), F32),
            jax.ShapeDtypeStruct((b, N_HEADS, LANES, seq), BF16),
            jax.ShapeDtypeStruct(
                (b, 2 * N_KV_GROUPS, seq // CMP_STRIDE, CMP_STRIDE * HEAD_DIM), F32),
            k_shape, vt_shape, k_shape, vt_shape,
            jax.ShapeDtypeStruct((n, 2 * D_MODEL), F32),
            jax.ShapeDtypeStruct((b, LANES, seq), F32),
        ],
        scratch_shapes=[pltpu.VMEM((2, TM_IN, LANES), F32)],
        compiler_params=pltpu.CompilerParams(
            dimension_semantics=("arbitrary",), vmem_limit_bytes=VMEM_LIMIT),
        name="in_proj",
    )(x2, g, w, cos, slo, shi)


def _pool_kernel(u_ref, up_ref, wp_ref, sc_ref, o_ref):
    i = pl.program_id(1)
    cur = u_ref[0]
    prev = jnp.where(i > 0, up_ref[0], 0.0)
    t = i * TS_POOL + lax.broadcasted_iota(jnp.int32, (TS_POOL, POOL_GW), 0)
    for g, w in enumerate(POOL_WINDOWS):
        sl = slice(g * POOL_GW, (g + 1) * POOL_GW)
        cg = cur[:, sl]
        s = jnp.concatenate([prev[:, sl], cg], axis=0)
        sh = 1
        while sh < w:
            s = s + pltpu.roll(s, sh, 0)
            sh *= 2
        cnt = jnp.minimum(t + 1, w).astype(F32)
        d = s[POOL_HALO:] / cnt - cg
        y = _dot(d.astype(BF16), wp_ref[g]) * sc_ref[:, sl]
        o_ref[0, :, sl] = y.astype(BF16)


def _pool(u3, wp, sc):
    b, seq, _ = u3.shape
    halo_per_tile = TS_POOL // POOL_HALO
    return pl.pallas_call(
        _pool_kernel,
        grid=(b, seq // TS_POOL),
        in_specs=[
            pl.BlockSpec((1, TS_POOL, POOL_WIDTH), lambda bi, i: (bi, i, 0)),
            pl.BlockSpec((1, POOL_HALO, POOL_WIDTH),
                         lambda bi, i: (bi, jnp.maximum(i * halo_per_tile - 1, 0), 0)),
            pl.BlockSpec((len(POOL_WINDOWS), POOL_GW, POOL_GW), lambda bi, i: (0, 0, 0)),
            pl.BlockSpec((1, POOL_WIDTH), lambda bi, i: (0, 0)),
        ],
        out_specs=pl.BlockSpec((1, TS_POOL, POOL_WIDTH), lambda bi, i: (bi, i, 0)),
        out_shape=jax.ShapeDtypeStruct((b, seq, POOL_WIDTH), BF16),
        compiler_params=pltpu.CompilerParams(dimension_semantics=("arbitrary", "arbitrary")),
        name="pool_mixer",
    )(u3, u3, wp, sc)


def _compress_kernel(c_ref, pe_ref, w1_ref, w2_ref, cos_ref, slo_ref, shi_ref, o_ref, ot_ref):
    half = CMP_STRIDE * HEAD_DIM
    c = c_ref[...]
    a = _dot((c + pe_ref[:, 0:half]).astype(BF16), w1_ref[0:half, :])
    b = _dot((c + pe_ref[:, half:2 * half]).astype(BF16), w1_ref[half:2 * half, :])
    n_rows = c.shape[0]
    hid = a + pltpu.roll(b, n_rows - 1, 0)
    act = jax.nn.gelu(hid, approximate=True)
    out = _dot(act.astype(BF16), w2_ref[...])
    out = _rope(out, cos_ref[...], slo_ref[...], shi_ref[...])
    row = lax.broadcasted_iota(jnp.int32, out.shape, 0)
    out = jnp.where(row < n_rows - 1, out, 0.0)
    o_ref[...] = out.astype(BF16)
    ot_ref[...] = _value_rows(out.T[0:HEAD_DIM, :]).astype(BF16)


def _compress(c4, pe, w1, w2p, cos, slo, shi):
    b, n_kv, n_chunks, half = c4.shape
    return pl.pallas_call(
        _compress_kernel,
        grid=(b, n_kv),
        in_specs=[
            pl.BlockSpec((None, None, n_chunks, half), lambda bi, j: (bi, j, 0, 0)),
            pl.BlockSpec((None, 1, 2 * half), lambda bi, j: (j // N_KV_GROUPS, 0, 0)),
            pl.BlockSpec((None, 2 * half, CMP_HIDDEN), lambda bi, j: (j // N_KV_GROUPS, 0, 0)),
            pl.BlockSpec((None, CMP_HIDDEN, LANES), lambda bi, j: (j, 0, 0)),
            pl.BlockSpec((None, n_chunks, LANES), lambda bi, j: (j // N_KV_GROUPS, 0, 0)),
            pl.BlockSpec((None, n_chunks, LANES), lambda bi, j: (j // N_KV_GROUPS, 0, 0)),
            pl.BlockSpec((None, n_chunks, LANES), lambda bi, j: (j // N_KV_GROUPS, 0, 0)),
        ],
        out_specs=[
            pl.BlockSpec((None, None, n_chunks, LANES), lambda bi, j: (bi, j, 0, 0)),
            pl.BlockSpec((None, None, V_ROWS, n_chunks), lambda bi, j: (bi, j, 0, 0)),
        ],
        out_shape=[
            jax.ShapeDtypeStruct((b, n_kv, n_chunks, LANES), BF16),
            jax.ShapeDtypeStruct((b, n_kv, V_ROWS, n_chunks), BF16),
        ],
        compiler_params=pltpu.CompilerParams(dimension_semantics=("arbitrary", "arbitrary")),
        name="compress",
    )(c4, pe, w1, w2p, cos, slo, shi)


def _block_rank(score):
    n_slc = score.shape[0]
    sub = 8
    ranks = []
    for v in range(n_slc // sub):
        blk = score[v * sub:(v + 1) * sub, :]
        jb_v = v * sub + lax.broadcasted_iota(jnp.int32, blk.shape, 0)
        r = jnp.zeros(blk.shape, F32)
        for jp in range(n_slc):
            row = score[jp:jp + 1, :]
            ge = jnp.where(row >= blk, 1.0, 0.0)
            gt = jnp.where(row > blk, 1.0, 0.0)
            if jp < v * sub:
                r = r + ge
            elif jp >= (v + 1) * sub:
                r = r + gt
            else:
                r = r + jnp.where(jb_v > jp, ge, gt)
        ranks.append(r)
    return jnp.concatenate(ranks, axis=0)


def _attn_kernel(q_ref, gn_ref, kc_ref, vct_ref, ks_ref, vst_ref, kw_ref, vwt_ref, ovt_ref,
                 pat_ref, o_ref, qsel_ref, s_ref, p_ref, al_ref, mc_ref, mw_ref, ms_ref,
                 accc_ref, accw_ref, accs_ref):
    g = pl.program_id(1)
    i = pl.program_id(2)
    s0 = i * TQ
    n_cmp = kc_ref.shape[0]
    n_slc = ovt_ref.shape[0]
    n_ch = HPG

    q_plain = lambda ch: q_ref[0, ch]
    q_selected = lambda ch: qsel_ref[ch]

    g_base = g * HEAD_DIM

    def gate_rows(c, ch):
        return gn_ref[0, pl.ds(g_base + c * HPG + ch, 1), :]

    def key_tile(ref, k0, nk):
        return ref[pl.ds(pl.multiple_of(k0, VT_CHUNK), nk), :]

    def value_tile(ref, k0, nk):
        c0 = k0 // VT_CHUNK
        return jnp.concatenate([ref[c0 + c] for c in range(nk // VT_CHUNK)], axis=1)

    imp_raw = []

    def run_step(soft=None, score=None, value=None, importance=False):
        for ch in range(n_ch):
            if soft is not None:
                nk, m_ref, first = soft
                s = s_ref[ch, 0:nk, :]
                mx = jnp.max(s, axis=0, keepdims=True)
                if first:
                    m_new, al_new = mx, None
                else:
                    m_prev = m_ref[ch:ch + 1, :]
                    m_new = jnp.maximum(m_prev, mx)
                    al_new = jnp.exp2(m_prev - m_new)
                p_new = jnp.exp2(s - m_new).astype(BF16)
            if score is not None:
                kt, q_chunk, bias = score
                sc = _dot(kt, q_chunk(ch))
                s_ref[ch, 0:kt.shape[0], :] = sc if bias is None else sc + bias()
            if value is not None:
                vt, acc_ref, vfirst = value
                pv = _dot(vt, p_ref[0:vt.shape[1], ch * TQ:(ch + 1) * TQ])
                acc_ref[ch] = pv if vfirst else al_ref[ch:ch + 1, :] * acc_ref[ch] + pv
            if importance:
                imp_raw.append(_dot(ovt_ref[...], p_ref[0:n_cmp, ch * TQ:(ch + 1) * TQ]))
            if soft is not None:
                p_ref[0:nk, ch * TQ:(ch + 1) * TQ] = p_new
                m_ref[ch:ch + 1, :] = m_new
                if al_new is not None:
                    al_ref[ch:ch + 1, :] = al_new

    n_idx = lax.broadcasted_iota(jnp.int32, (n_cmp, TQ), 0)
    t_cmp = s0 + lax.broadcasted_iota(jnp.int32, (n_cmp, TQ), 1)
    cmp_mask = jnp.where(n_idx * CMP_STRIDE + CMP_LEN - 1 <= t_cmp, 0.0, NEG_INF)
    cmp_bias = lambda: cmp_mask

    w0 = jnp.maximum(s0 - WINDOW, 0)
    win_tiles = [(off, min(NK_SEL, WIN_KEYS - off)) for off in range(0, WIN_KEYS, NK_SEL)]
    n_win = len(win_tiles)

    def win_bias(off, nk):
        d = w0 + off - s0
        pat = jnp.where(d == -WINDOW, PAT_ABOVE,
                        jnp.where(d == 0, PAT_CAUSAL, jnp.where(d < 0, PAT_ALL, PAT_NONE)))
        return lambda: pat_ref[pat]

    def win_step(k):
        args = {}
        if k < n_win:
            off, nk = win_tiles[k]
            args["score"] = (key_tile(kw_ref, w0 + off, nk), q_plain, win_bias(off, nk))
        if 1 <= k <= n_win:
            args["soft"] = (win_tiles[k - 1][1], mw_ref, k == 1)
        if 2 <= k <= n_win + 1:
            off, nk = win_tiles[k - 2]
            args["value"] = (value_tile(vwt_ref, w0 + off, nk), accw_ref, k == 2)
        return args

    run_step(score=(kc_ref[...], q_plain, cmp_bias))
    run_step(soft=(n_cmp, mc_ref, True), **win_step(0))
    run_step(value=(vct_ref[...], accc_ref, True), importance=True, **win_step(1))
    for k in range(2, n_win):
        run_step(**win_step(k))

    cmp_scale = [jnp.where(mc_ref[ch:ch + 1, :] > 0.5 * NEG_INF,
                           1.0 / accc_ref[ch, HEAD_DIM:HEAD_DIM + 1, :], 0.0)
                 for ch in range(n_ch)]

    imp = functools.reduce(lambda a, b: a + b, [r * sc for r, sc in zip(imp_raw, cmp_scale)])
    jb = lax.broadcasted_iota(jnp.int32, (n_slc, TQ), 0)
    tq = s0 + lax.broadcasted_iota(jnp.int32, (n_slc, TQ), 1)
    causal = jb * SEL_BLOCK <= tq
    near = jnp.logical_or(jb == 0, jb >= tq // SEL_BLOCK - 1)
    score = jnp.where(causal, jnp.where(near, SEL_BONUS, imp), NEG_INF)
    rank = jnp.concatenate(
        [_block_rank(score[:, c0:c0 + LANES]) for c0 in range(0, TQ, LANES)], axis=1)
    sel_bias = jnp.where(causal, jnp.where(rank < float(N_SEL), 0.0, NEG_INF), NEG_INF)
    parts = [jnp.zeros((HEAD_DIM, TQ), F32), sel_bias]
    if n_slc < HEAD_DIM:
        parts.append(jnp.zeros((HEAD_DIM - n_slc, TQ), F32))
    sel_rows = jnp.concatenate(parts, axis=0).astype(BF16)
    for hh in range(HPG):
        qsel_ref[hh] = q_ref[0, hh] + sel_rows

    ms_ref[...] = jnp.full(ms_ref.shape, NEG_INF, F32)
    accs_ref[...] = jnp.zeros(accs_ref.shape, F32)
    n_full = s0 // NK_SEL

    def sel_tile(m):
        return jnp.where(m == 0, n_full, m - 1)

    def sel_scores(m, bias=None):
        return (key_tile(ks_ref, sel_tile(m) * NK_SEL, NK_SEL), q_selected, bias)

    def sel_values(m, live):
        vt = value_tile(vst_ref, sel_tile(jnp.maximum(m, 0)) * NK_SEL, NK_SEL)
        return jnp.where(live, vt, jnp.zeros_like(vt))

    run_step(score=sel_scores(0, lambda: pat_ref[PAT_CAUSAL]), **win_step(n_win))
    run_step(**win_step(n_win + 1))

    def sel_body(m, carry):
        run_step(soft=(NK_SEL, ms_ref, False), score=sel_scores(m),
                 value=(sel_values(m - 2, m >= 2), accs_ref, False))
        return carry

    lax.fori_loop(1, n_full + 1, sel_body, 0)
    run_step(soft=(NK_SEL, ms_ref, False),
             value=(sel_values(n_full - 1, n_full >= 1), accs_ref, False))
    run_step(value=(sel_values(n_full, True), accs_ref, False))

    heads = []
    for ch in range(n_ch):
        coef_c = gate_rows(0, ch) * cmp_scale[ch]
        coef_w = gate_rows(2, ch) / accw_ref[ch, HEAD_DIM:HEAD_DIM + 1, :]
        coef_s = gate_rows(1, ch) / accs_ref[ch, HEAD_DIM:HEAD_DIM + 1, :]
        heads.append(accc_ref[ch, 0:HEAD_DIM, :] * coef_c + accw_ref[ch, 0:HEAD_DIM, :] * coef_w
                     + accs_ref[ch, 0:HEAD_DIM, :] * coef_s)
    for pair in range(HPG // 2):
        both = jnp.concatenate(heads[2 * pair:2 * pair + 2], axis=0)
        o_ref[0, :, pair * LANES:(pair + 1) * LANES] = both.T.astype(BF16)


def _attention(q3, gn3, kcmp, vcmp_t, ks4, vst, kw4, vwt, ovt, pat):
    b, _, _, seq = q3.shape
    n_cmp = kcmp.shape[2]
    n_slc = seq // SEL_BLOCK
    chunks_per_seq = seq // VT_CHUNK
    per_group = lambda bi, g, i: (bi, g, 0, 0)
    return pl.pallas_call(
        _attn_kernel,
        grid=(b, N_KV_GROUPS, seq // TQ),
        in_specs=[
            pl.BlockSpec((1, HPG, LANES, TQ), lambda bi, g, i: (bi, g, 0, i)),
            pl.BlockSpec((1, LANES, TQ), lambda bi, g, i: (bi, 0, i)),
            pl.BlockSpec((None, None, n_cmp, LANES), per_group),
            pl.BlockSpec((None, None, V_ROWS, n_cmp),
                         lambda bi, g, i: (bi, N_KV_GROUPS + g, 0, 0)),
            pl.BlockSpec((None, None, seq, LANES), per_group),
            pl.BlockSpec((None, None, chunks_per_seq, V_ROWS, VT_CHUNK),
                         lambda bi, g, i: (bi, g, 0, 0, 0)),
            pl.BlockSpec((None, None, seq, LANES), per_group),
            pl.BlockSpec((None, None, chunks_per_seq, V_ROWS, VT_CHUNK),
                         lambda bi, g, i: (bi, g, 0, 0, 0)),
            pl.BlockSpec((n_slc, n_cmp), lambda bi, g, i: (0, 0)),
            _resident(pat.shape, lambda bi, g, i: (0, 0, 0)),
        ],
        out_specs=pl.BlockSpec((1, TQ, HPG * HEAD_DIM), lambda bi, g, i: (bi, i, g)),
        out_shape=jax.ShapeDtypeStruct((b, seq, NSA_WIDTH), BF16),
        scratch_shapes=[
            pltpu.VMEM((HPG, LANES, TQ), BF16),
            pltpu.VMEM((HPG, NK_SEL, TQ), F32),
            pltpu.VMEM((NK_SEL, HPG * TQ), BF16),
            pltpu.VMEM((HPG, TQ), F32),
            pltpu.VMEM((HPG, TQ), F32),
            pltpu.VMEM((HPG, TQ), F32),
            pltpu.VMEM((HPG, TQ), F32),
            pltpu.VMEM((HPG, V_ROWS, TQ), F32),
            pltpu.VMEM((HPG, V_ROWS, TQ), F32),
            pltpu.VMEM((HPG, V_ROWS, TQ), F32),
        ],
        compiler_params=pltpu.CompilerParams(
            dimension_semantics=("arbitrary", "arbitrary", "arbitrary"),
            vmem_limit_bytes=VMEM_LIMIT),
        name="nsa_attention",
    )(q3, gn3, kcmp, vcmp_t, ks4, vst, kw4, vwt, ovt, pat)


def _mlp_kernel(x_ref, yp_ref, yn_ref, gm_ref, wpp_ref, wpn_ref, wo_ref, nm_ref, w1_ref, w2_ref,
                nf_ref, o_ref, *, final):
    p1 = _dot(yp_ref[...], wpp_ref[...])
    p2 = _dot(yn_ref[...], wpn_ref[...])
    ga = jax.nn.sigmoid(gm_ref[:, 0:D_MODEL])
    gb = jax.nn.sigmoid(gm_ref[:, D_MODEL:2 * D_MODEL])
    merged = ga * p1 + gb * p2
    x = x_ref[...] + _dot(merged.astype(BF16), wo_ref[...])
    h = _rms(x, nm_ref[...]).astype(BF16)
    acc = jnp.zeros((TM_MLP, D_MODEL), F32)
    for c in range(D_FF // FF_CHUNK):
        sl = slice(c * FF_CHUNK, (c + 1) * FF_CHUNK)
        a = jnp.square(jnp.maximum(_dot(h, w1_ref[:, sl]), 0.0)).astype(BF16)
        acc = acc + _dot(a, w2_ref[sl, :])
    x = x + acc
    if final:
        x = _rms(x, nf_ref[...])
    o_ref[...] = x


def _merge_mlp(x2, yp, yn, gm, wpp, wpn, wo, nm, w1, w2, nf, layer, final):
    n = x2.shape[0]
    row = lambda w_: pl.BlockSpec((TM_MLP, w_), lambda i: (i, 0))
    res = lambda a: _resident(a.shape, lambda i: (0,) * a.ndim)
    lay = lambda a: _layer_resident(a, layer)
    return pl.pallas_call(
        functools.partial(_mlp_kernel, final=final),
        grid=(n // TM_MLP,),
        in_specs=[row(D_MODEL), row(POOL_WIDTH), row(NSA_WIDTH), row(2 * D_MODEL),
                  lay(wpp), lay(wpn), lay(wo), res(nm), lay(w1), lay(w2), res(nf)],
        out_specs=row(D_MODEL),
        out_shape=jax.ShapeDtypeStruct((n, D_MODEL), F32),
        compiler_params=pltpu.CompilerParams(
            dimension_semantics=("arbitrary",), vmem_limit_bytes=VMEM_LIMIT),
        name="merge_mlp",
    )(x2, yp, yn, gm, wpp, wpn, wo, nm, w1, w2, nf)


def _rope_tables(pos):
    inv = ROPE_THETA ** (-jnp.arange(0, HEAD_DIM, 2, dtype=F32) / HEAD_DIM)
    ang = pos.astype(F32)[:, None] * inv[None, :]
    ang = jnp.concatenate([ang, ang, ang, ang], axis=-1)
    first_half = (jnp.arange(LANES) % HEAD_DIM) < HEAD_DIM // 2
    cos, sin = jnp.cos(ang), jnp.sin(ang)
    return cos, jnp.where(first_half, -sin, 0.0), jnp.where(first_half, 0.0, sin)


def _permute_w_in(w_in):
    o_q = POOL_WIDTH
    o_kv = o_q + NSA_WIDTH
    o_gn = o_kv + 6 * KV_WIDTH
    o_gm = o_gn + N_GATE
    depth = w_in.shape[0]
    gn = w_in[:, :, o_gn:o_gm].reshape(depth, D_MODEL, N_KV_GROUPS, HPG, 3)
    gn = gn.transpose(0, 1, 2, 4, 3).reshape(depth, D_MODEL, N_KV_GROUPS, 3 * HPG)
    gn = jnp.pad(gn, ((0, 0), (0, 0), (0, 0), (0, LANES // N_KV_GROUPS - 3 * HPG)))
    gn = gn.reshape(depth, D_MODEL, LANES)
    return jnp.concatenate(
        [w_in[:, :, 0:o_gn], w_in[:, :, o_gm:], gn], axis=-1).astype(BF16)


def kernel(x, norm_mix, w_in, w_pool, pool_scale, pe_k, pe_v, w_ck1, w_ck2, w_cv1, w_cv2,
           w_proj_pool, w_proj_nsa, w_out, norm_mlp, w_ff1, w_ff2, norm_final):
    b, seq, d = x.shape
    depth = w_in.shape[0]
    n = b * seq
    n_chunks = seq // CMP_STRIDE
    n_slc = seq // SEL_BLOCK
    assert n_slc <= HEAD_DIM, "the selection one-hot shares the 64 spare key lanes"
    assert seq >= WIN_KEYS and seq % TM_IN == 0

    w_in_p = _permute_w_in(w_in)
    w_pool_b = w_pool.astype(BF16)
    pe = jnp.stack([pe_k, pe_v], axis=1).reshape(depth, 2, 1, CMP_LEN * HEAD_DIM)
    w_c1 = jnp.stack([w_ck1, w_cv1], axis=1).astype(BF16)
    pad = jnp.zeros_like(w_ck2)
    w_c2 = jnp.stack([jnp.concatenate([w, pad], axis=-1) for w in (w_ck2, w_ck2, w_cv2, w_cv2)],
                     axis=1).astype(BF16)
    wpp, wpn, wo = w_proj_pool.astype(BF16), w_proj_nsa.astype(BF16), w_out.astype(BF16)
    w1, w2 = w_ff1.astype(BF16), w_ff2.astype(BF16)

    cos, slo, shi = _rope_tables(jnp.arange(seq))
    ccos, cslo, cshi = _rope_tables(jnp.arange(n_chunks) * CMP_STRIDE + CMP_LEN - 1)
    ident = (jnp.ones_like(ccos), jnp.zeros_like(cslo), jnp.zeros_like(cshi))
    cmp_tabs = [jnp.stack([t, i_], axis=0) for t, i_ in zip((ccos, cslo, cshi), ident)]
    cmp_start = jnp.arange(n_chunks) * CMP_STRIDE
    slc_start = jnp.arange(n_slc) * SEL_BLOCK
    ovt = ((cmp_start[None, :] <= slc_start[:, None] + SEL_BLOCK - 1)
           & (cmp_start[None, :] + CMP_LEN - 1 >= slc_start[:, None])).astype(BF16)
    k_loc = jnp.arange(NK_SEL)[:, None]
    t_loc = jnp.arange(TQ)[None, :]
    keep = jnp.stack([jnp.ones((NK_SEL, TQ), bool), k_loc <= t_loc, k_loc > t_loc,
                      jnp.zeros((NK_SEL, TQ), bool)])
    pat = jnp.where(keep, 0.0, NEG_INF).astype(F32)

    x2 = x.reshape(n, d)
    for l in range(depth):
        u, q, c4, ks, vst, kw, vwt, gm, gn = _inproj(
            x2, norm_mix[l][None, :], w_in_p, l, cos, slo, shi, b, seq)
        y_pool = _pool(u.reshape(b, seq, POOL_WIDTH), w_pool_b[l], pool_scale[l][None, :])
        cmp_n, cmp_t = _compress(c4, pe[l], w_c1[l], w_c2[l], *cmp_tabs)
        y_nsa = _attention(
            q, gn, cmp_n, cmp_t,
            ks, vst, kw, vwt, ovt, pat)
        x2 = _merge_mlp(
            x2, y_pool.reshape(n, POOL_WIDTH), y_nsa.reshape(n, NSA_WIDTH), gm,
            wpp, wpn, wo, norm_mlp[l][None, :], w1, w2, norm_final[None, :],
            layer=l, final=(l == depth - 1))
    return x2.reshape(b, seq, d)
```

```python
import functools

import jax
import jax.numpy as jnp
import numpy as np
from jax import lax
from jax.experimental import pallas as pl
from jax.experimental.pallas import tpu as pltpu

F32 = jnp.float32
BF16 = jnp.bfloat16

D_MODEL = 1024
POOL_WINDOWS = (2, 4, 8, 16)
POOL_WIDTH = 512
POOL_GW = 128
N_HEADS = 16
HEAD_DIM = 64
N_KV_GROUPS = 2
HPG = 8
NSA_WIDTH = 1024
KV_WIDTH = 128
CMP_LEN = 32
CMP_STRIDE = 16
CMP_HIDDEN = 256
SEL_BLOCK = 64
N_SEL = 16
WINDOW = 512
SEL_BONUS = 1e4
NEG_INF = -1e30
ROPE_THETA = 10000.0
D_FF = 4096
RMS_EPS = 1e-6
N_GATE = 3 * N_HEADS
Q_SCALE = HEAD_DIM ** -0.5 * float(np.log2(np.e))

LANES = 128
VMEM_LIMIT = 56 * 1024 * 1024

C_U = 0
C_Q = C_U + POOL_WIDTH
C_KV = C_Q + NSA_WIDTH
C_GM = C_KV + 6 * KV_WIDTH
C_GN = C_GM + 2 * D_MODEL
N_INP = C_GN + LANES

TM_IN = 512
POOL_HALO = 16
TM_MLP = 512
FF_CHUNK = 1024
TQ = 256
NK_SEL = 256
WIN_KEYS = WINDOW + TQ
VT_CHUNK = 128
V_ROWS = HEAD_DIM + 16
PAT_ALL, PAT_CAUSAL, PAT_ABOVE, PAT_NONE = 0, 1, 2, 3
assert NK_SEL == TQ and WINDOW == 2 * NK_SEL and WIN_KEYS % NK_SEL == 0


def _dot(a, b):
    return jnp.dot(a, b, preferred_element_type=F32)


def _rms(x, g):
    return x * lax.rsqrt(jnp.mean(x * x, axis=-1, keepdims=True) + RMS_EPS) * g


def _rope(t, cos, sin_lo, sin_hi):
    return t * cos + pltpu.roll(t, LANES - 32, 1) * sin_lo + pltpu.roll(t, 32, 1) * sin_hi


def _resident(shape, index_map):
    return pl.BlockSpec(shape, index_map, pipeline_mode=pl.Buffered(1))


def _layer_resident(stacked, layer):
    nd = stacked.ndim - 1
    return _resident((None,) + stacked.shape[1:], lambda i: (layer,) + (0,) * nd)


def _value_rows(v_t):
    tail_row = lax.broadcasted_iota(jnp.int32, (V_ROWS - HEAD_DIM, v_t.shape[1]), 0)
    return jnp.concatenate([v_t, jnp.where(tail_row == 0, 1.0, 0.0)], axis=0)


def _inproj_kernel(x_ref, g_ref, w_ref, cos_ref, slo_ref, shi_ref,
                   u_ref, q_ref, cmp_ref, ks_ref, vst_ref, kw_ref, vwt_ref, gm_ref, gn_ref,
                   kv_scr, *, tiles_per_seq):
    h = _rms(x_ref[...], g_ref[...]).astype(BF16)
    cos, slo, shi = cos_ref[...], slo_ref[...], shi_ref[...]
    lane = lax.broadcasted_iota(jnp.int32, (TM_IN, LANES), 1)
    low = lane < HEAD_DIM
    pos = (pl.program_id(0) % tiles_per_seq) * TM_IN + lax.broadcasted_iota(
        jnp.int32, (TM_IN, LANES), 0)
    block_onehot = jnp.where(lane - HEAD_DIM == pos // SEL_BLOCK, 1.0, 0.0)

    u_ref[...] = _dot(h, w_ref[:, C_U:C_Q])
    q = _dot(h, w_ref[:, C_Q:C_KV])
    for k in range(NSA_WIDTH // LANES):
        qt = (_rope(q[:, k * LANES:(k + 1) * LANES], cos, slo, shi) * Q_SCALE).T
        spare = jnp.zeros((LANES - HEAD_DIM, TM_IN), F32)
        for par in range(2):
            q_ref[0, 2 * k + par] = jnp.concatenate(
                [qt[par * HEAD_DIM:(par + 1) * HEAD_DIM, :], spare], axis=0).astype(BF16)
    kv = _dot(h, w_ref[:, C_KV:C_GM])
    n_rows = TM_IN // CMP_STRIDE
    low_c = lax.broadcasted_iota(jnp.int32, (n_rows, LANES), 1) < HEAD_DIM
    for t in range(2):
        kv_scr[t] = kv[:, t * LANES:(t + 1) * LANES]
        for pp in range(CMP_STRIDE // 2):
            a = kv_scr[t, pl.ds(2 * pp, n_rows, stride=CMP_STRIDE), :]
            b = kv_scr[t, pl.ds(2 * pp + 1, n_rows, stride=CMP_STRIDE), :]
            sl = slice(pp * LANES, (pp + 1) * LANES)
            cmp_ref[0, 2 * t, :, sl] = jnp.where(low_c, a, pltpu.roll(b, HEAD_DIM, 1))
            cmp_ref[0, 2 * t + 1, :, sl] = jnp.where(low_c, pltpu.roll(a, HEAD_DIM, 1), b)
    ks = _rope(kv[:, 2 * KV_WIDTH:3 * KV_WIDTH], cos, slo, shi)
    kw = _rope(kv[:, 4 * KV_WIDTH:5 * KV_WIDTH], cos, slo, shi)
    vs_t = kv[:, 3 * KV_WIDTH:4 * KV_WIDTH].T
    vw_t = kv[:, 5 * KV_WIDTH:6 * KV_WIDTH].T
    for g in range(N_KV_GROUPS):
        ks_g = ks if g == 0 else pltpu.roll(ks, HEAD_DIM, 1)
        kw_g = kw if g == 0 else pltpu.roll(kw, HEAD_DIM, 1)
        ks_ref[0, g] = jnp.where(low, ks_g, block_onehot).astype(BF16)
        kw_ref[0, g] = jnp.where(low, kw_g, 0.0).astype(BF16)
        vs_g = _value_rows(vs_t[g * HEAD_DIM:(g + 1) * HEAD_DIM, :]).astype(BF16)
        vw_g = _value_rows(vw_t[g * HEAD_DIM:(g + 1) * HEAD_DIM, :]).astype(BF16)
        for c in range(TM_IN // VT_CHUNK):
            sl = slice(c * VT_CHUNK, (c + 1) * VT_CHUNK)
            vst_ref[0, g, c] = vs_g[:, sl]
            vwt_ref[0, g, c] = vw_g[:, sl]
    gm_ref[...] = _dot(h, w_ref[:, C_GM:C_GN])
    gn_ref[0] = jax.nn.sigmoid(_dot(h, w_ref[:, C_GN:N_INP])).T


def _inproj(x2, g, w, layer, cos, slo, shi, b, seq):
    n = x2.shape[0]
    tiles_per_seq = seq // TM_IN
    n_chunks = TM_IN // VT_CHUNK
    row = lambda w_: pl.BlockSpec((TM_IN, w_), lambda i: (i, 0))
    tab = pl.BlockSpec((TM_IN, LANES), lambda i: (i % tiles_per_seq, 0))
    kg = pl.BlockSpec((1, N_KV_GROUPS, TM_IN, LANES),
                      lambda i: (i // tiles_per_seq, 0, i % tiles_per_seq, 0))
    vt = pl.BlockSpec((1, N_KV_GROUPS, n_chunks, V_ROWS, VT_CHUNK),
                      lambda i: (i // tiles_per_seq, 0, i % tiles_per_seq, 0, 0))
    cmp_spec = pl.BlockSpec(
        (1, 2 * N_KV_GROUPS, TM_IN // CMP_STRIDE, CMP_STRIDE * HEAD_DIM),
        lambda i: (i // tiles_per_seq, 0, i % tiles_per_seq, 0))
    qt_spec = pl.BlockSpec((1, N_HEADS, LANES, TM_IN),
                           lambda i: (i // tiles_per_seq, 0, 0, i % tiles_per_seq))
    gt_spec = pl.BlockSpec((1, LANES, TM_IN), lambda i: (i // tiles_per_seq, 0, i % tiles_per_seq))
    k_shape = jax.ShapeDtypeStruct((b, N_KV_GROUPS, seq, LANES), BF16)
    vt_shape = jax.ShapeDtypeStruct((b, N_KV_GROUPS, seq // VT_CHUNK, V_ROWS, VT_CHUNK), BF16)
    return pl.pallas_call(
        functools.partial(_inproj_kernel, tiles_per_seq=tiles_per_seq),
        grid=(n // TM_IN,),
        in_specs=[row(D_MODEL), _resident((1, D_MODEL), lambda i: (0, 0)),
                  _layer_resident(w, layer), tab, tab, tab],
        out_specs=[row(POOL_WIDTH), qt_spec, cmp_spec, kg, vt,
                   kg, vt, row(2 * D_MODEL), gt_spec],
        out_shape=[
            jax.ShapeDtypeStruct((n, POOL_WIDTH), F32),
            jax.ShapeDtypeStruct((b, N_HEADS, LANES, seq), BF16),
            jax.ShapeDtypeStruct(
                (b, 2 * N_KV_GROUPS, seq // CMP_STRIDE, CMP_STRIDE * HEAD_DIM), F32),
            k_shape, vt_shape, k_shape, vt_shape,
            jax.ShapeDtypeStruct((n, 2 * D_MODEL), F32),
            jax.ShapeDtypeStruct((b, LANES, seq), F32),
        ],
        scratch_shapes=[pltpu.VMEM((2, TM_IN, LANES), F32)],
        compiler_params=pltpu.CompilerParams(
            dimension_semantics=("arbitrary",), vmem_limit_bytes=VMEM_LIMIT),
        name="in_proj",
    )(x2, g, w, cos, slo, shi)


def _pool_mix(cur, prev, t0, wp_ref, sc_ref):
    t = t0 + lax.broadcasted_iota(jnp.int32, (cur.shape[0], POOL_GW), 0)
    out = []
    for g, w in enumerate(POOL_WINDOWS):
        sl = slice(g * POOL_GW, (g + 1) * POOL_GW)
        cg = cur[:, sl]
        s = jnp.concatenate([prev[:, sl], cg], axis=0)
        sh = 1
        while sh < w:
            s = s + pltpu.roll(s, sh, 0)
            sh *= 2
        cnt = jnp.minimum(t + 1, w).astype(F32)
        d = s[POOL_HALO:] / cnt - cg
        out.append((_dot(d.astype(BF16), wp_ref[g]) * sc_ref[:, sl]).astype(BF16))
    return jnp.concatenate(out, axis=1)


def _compress_kernel(c_ref, pe_ref, w1_ref, w2_ref, cos_ref, slo_ref, shi_ref, o_ref, ot_ref):
    half = CMP_STRIDE * HEAD_DIM
    c = c_ref[...]
    a = _dot((c + pe_ref[:, 0:half]).astype(BF16), w1_ref[0:half, :])
    b = _dot((c + pe_ref[:, half:2 * half]).astype(BF16), w1_ref[half:2 * half, :])
    n_rows = c.shape[0]
    hid = a + pltpu.roll(b, n_rows - 1, 0)
    act = jax.nn.gelu(hid, approximate=True)
    out = _dot(act.astype(BF16), w2_ref[...])
    out = _rope(out, cos_ref[...], slo_ref[...], shi_ref[...])
    row = lax.broadcasted_iota(jnp.int32, out.shape, 0)
    out = jnp.where(row < n_rows - 1, out, 0.0)
    o_ref[...] = out.astype(BF16)
    ot_ref[...] = _value_rows(out.T[0:HEAD_DIM, :]).astype(BF16)


def _compress(c4, pe, w1, w2p, cos, slo, shi):
    b, n_kv, n_chunks, half = c4.shape
    return pl.pallas_call(
        _compress_kernel,
        grid=(b, n_kv),
        in_specs=[
            pl.BlockSpec((None, None, n_chunks, half), lambda bi, j: (bi, j, 0, 0)),
            pl.BlockSpec((None, 1, 2 * half), lambda bi, j: (j // N_KV_GROUPS, 0, 0)),
            pl.BlockSpec((None, 2 * half, CMP_HIDDEN), lambda bi, j: (j // N_KV_GROUPS, 0, 0)),
            pl.BlockSpec((None, CMP_HIDDEN, LANES), lambda bi, j: (j, 0, 0)),
            pl.BlockSpec((None, n_chunks, LANES), lambda bi, j: (j // N_KV_GROUPS, 0, 0)),
            pl.BlockSpec((None, n_chunks, LANES), lambda bi, j: (j // N_KV_GROUPS, 0, 0)),
            pl.BlockSpec((None, n_chunks, LANES), lambda bi, j: (j // N_KV_GROUPS, 0, 0)),
        ],
        out_specs=[
            pl.BlockSpec((None, None, n_chunks, LANES), lambda bi, j: (bi, j, 0, 0)),
            pl.BlockSpec((None, None, V_ROWS, n_chunks), lambda bi, j: (bi, j, 0, 0)),
        ],
        out_shape=[
            jax.ShapeDtypeStruct((b, n_kv, n_chunks, LANES), BF16),
            jax.ShapeDtypeStruct((b, n_kv, V_ROWS, n_chunks), BF16),
        ],
        compiler_params=pltpu.CompilerParams(dimension_semantics=("arbitrary", "arbitrary")),
        name="compress",
    )(c4, pe, w1, w2p, cos, slo, shi)


def _block_rank(score):
    n_slc = score.shape[0]
    sub = 8
    ranks = []
    for v in range(n_slc // sub):
        blk = score[v * sub:(v + 1) * sub, :]
        jb_v = v * sub + lax.broadcasted_iota(jnp.int32, blk.shape, 0)
        r = jnp.zeros(blk.shape, F32)
        for jp in range(n_slc):
            row = score[jp:jp + 1, :]
            ge = jnp.where(row >= blk, 1.0, 0.0)
            gt = jnp.where(row > blk, 1.0, 0.0)
            if jp < v * sub:
                r = r + ge
            elif jp >= (v + 1) * sub:
                r = r + gt
            else:
                r = r + jnp.where(jb_v > jp, ge, gt)
        ranks.append(r)
    return jnp.concatenate(ranks, axis=0)


def _attn_kernel(q_ref, gn_ref, kc_ref, vct_ref, ks_ref, vst_ref, kw_ref, vwt_ref, ovt_ref,
                 pat_ref, o_ref, qsel_ref, s_ref, p_ref, al_ref, mc_ref, mw_ref, ms_ref,
                 accc_ref, accw_ref, accs_ref):
    g = pl.program_id(1)
    i = pl.program_id(2)
    s0 = i * TQ
    n_cmp = kc_ref.shape[0]
    n_slc = ovt_ref.shape[0]
    n_ch = HPG

    q_plain = lambda ch: q_ref[0, ch]
    q_selected = lambda ch: qsel_ref[ch]

    g_base = g * HEAD_DIM

    def gate_rows(c, ch):
        return gn_ref[0, pl.ds(g_base + c * HPG + ch, 1), :]

    def key_tile(ref, k0, nk):
        return ref[pl.ds(pl.multiple_of(k0, VT_CHUNK), nk), :]

    def value_tile(ref, k0, nk):
        c0 = k0 // VT_CHUNK
        return jnp.concatenate([ref[c0 + c] for c in range(nk // VT_CHUNK)], axis=1)

    imp_raw = []

    def run_step(soft=None, score=None, value=None, importance=False):
        for ch in range(n_ch):
            if soft is not None:
                nk, m_ref, first = soft
                s = s_ref[ch, 0:nk, :]
                mx = jnp.max(s, axis=0, keepdims=True)
                if first:
                    m_new, al_new = mx, None
                else:
                    m_prev = m_ref[ch:ch + 1, :]
                    m_new = jnp.maximum(m_prev, mx)
                    al_new = jnp.exp2(m_prev - m_new)
                p_new = jnp.exp2(s - m_new).astype(BF16)
            if score is not None:
                kt, q_chunk, bias = score
                sc = _dot(kt, q_chunk(ch))
                s_ref[ch, 0:kt.shape[0], :] = sc if bias is None else sc + bias()
            if value is not None:
                vt, acc_ref, vfirst = value
                pv = _dot(vt, p_ref[0:vt.shape[1], ch * TQ:(ch + 1) * TQ])
                acc_ref[ch] = pv if vfirst else al_ref[ch:ch + 1, :] * acc_ref[ch] + pv
            if importance:
                imp_raw.append(_dot(ovt_ref[...], p_ref[0:n_cmp, ch * TQ:(ch + 1) * TQ]))
            if soft is not None:
                p_ref[0:nk, ch * TQ:(ch + 1) * TQ] = p_new
                m_ref[ch:ch + 1, :] = m_new
                if al_new is not None:
                    al_ref[ch:ch + 1, :] = al_new

    n_idx = lax.broadcasted_iota(jnp.int32, (n_cmp, TQ), 0)
    t_cmp = s0 + lax.broadcasted_iota(jnp.int32, (n_cmp, TQ), 1)
    cmp_mask = jnp.where(n_idx * CMP_STRIDE + CMP_LEN - 1 <= t_cmp, 0.0, NEG_INF)
    cmp_bias = lambda: cmp_mask

    w0 = jnp.maximum(s0 - WINDOW, 0)
    win_tiles = [(off, min(NK_SEL, WIN_KEYS - off)) for off in range(0, WIN_KEYS, NK_SEL)]
    n_win = len(win_tiles)

    def win_bias(off, nk):
        d = w0 + off - s0
        pat = jnp.where(d == -WINDOW, PAT_ABOVE,
                        jnp.where(d == 0, PAT_CAUSAL, jnp.where(d < 0, PAT_ALL, PAT_NONE)))
        return lambda: pat_ref[pat]

    def win_step(k):
        args = {}
        if k < n_win:
            off, nk = win_tiles[k]
            args["score"] = (key_tile(kw_ref, w0 + off, nk), q_plain, win_bias(off, nk))
        if 1 <= k <= n_win:
            args["soft"] = (win_tiles[k - 1][1], mw_ref, k == 1)
        if 2 <= k <= n_win + 1:
            off, nk = win_tiles[k - 2]
            args["value"] = (value_tile(vwt_ref, w0 + off, nk), accw_ref, k == 2)
        return args

    run_step(score=(kc_ref[...], q_plain, cmp_bias))
    run_step(soft=(n_cmp, mc_ref, True), **win_step(0))
    run_step(value=(vct_ref[...], accc_ref, True), importance=True, **win_step(1))
    for k in range(2, n_win):
        run_step(**win_step(k))

    cmp_scale = [jnp.where(mc_ref[ch:ch + 1, :] > 0.5 * NEG_INF,
                           1.0 / accc_ref[ch, HEAD_DIM:HEAD_DIM + 1, :], 0.0)
                 for ch in range(n_ch)]

    imp = functools.reduce(lambda a, b: a + b, [r * sc for r, sc in zip(imp_raw, cmp_scale)])
    jb = lax.broadcasted_iota(jnp.int32, (n_slc, TQ), 0)
    tq = s0 + lax.broadcasted_iota(jnp.int32, (n_slc, TQ), 1)
    causal = jb * SEL_BLOCK <= tq
    near = jnp.logical_or(jb == 0, jb >= tq // SEL_BLOCK - 1)
    score = jnp.where(causal, jnp.where(near, SEL_BONUS, imp), NEG_INF)
    rank = jnp.concatenate(
        [_block_rank(score[:, c0:c0 + LANES]) for c0 in range(0, TQ, LANES)], axis=1)
    sel_bias = jnp.where(causal, jnp.where(rank < float(N_SEL), 0.0, NEG_INF), NEG_INF)
    parts = [jnp.zeros((HEAD_DIM, TQ), F32), sel_bias]
    if n_slc < HEAD_DIM:
        parts.append(jnp.zeros((HEAD_DIM - n_slc, TQ), F32))
    sel_rows = jnp.concatenate(parts, axis=0).astype(BF16)
    for hh in range(HPG):
        qsel_ref[hh] = q_ref[0, hh] + sel_rows

    ms_ref[...] = jnp.full(ms_ref.shape, NEG_INF, F32)
    accs_ref[...] = jnp.zeros(accs_ref.shape, F32)
    n_full = s0 // NK_SEL

    def sel_tile(m):
        return jnp.where(m == 0, n_full, m - 1)

    def sel_scores(m, bias=None):
        return (key_tile(ks_ref, sel_tile(m) * NK_SEL, NK_SEL), q_selected, bias)

    def sel_values(m, live):
        vt = value_tile(vst_ref, sel_tile(jnp.maximum(m, 0)) * NK_SEL, NK_SEL)
        return jnp.where(live, vt, jnp.zeros_like(vt))

    run_step(score=sel_scores(0, lambda: pat_ref[PAT_CAUSAL]), **win_step(n_win))
    run_step(**win_step(n_win + 1))

    def sel_body(m, carry):
        run_step(soft=(NK_SEL, ms_ref, False), score=sel_scores(m),
                 value=(sel_values(m - 2, m >= 2), accs_ref, False))
        return carry

    lax.fori_loop(1, n_full + 1, sel_body, 0)
    run_step(soft=(NK_SEL, ms_ref, False),
             value=(sel_values(n_full - 1, n_full >= 1), accs_ref, False))
    run_step(value=(sel_values(n_full, True), accs_ref, False))

    heads = []
    for ch in range(n_ch):
        coef_c = gate_rows(0, ch) * cmp_scale[ch]
        coef_w = gate_rows(2, ch) / accw_ref[ch, HEAD_DIM:HEAD_DIM + 1, :]
        coef_s = gate_rows(1, ch) / accs_ref[ch, HEAD_DIM:HEAD_DIM + 1, :]
        heads.append(accc_ref[ch, 0:HEAD_DIM, :] * coef_c + accw_ref[ch, 0:HEAD_DIM, :] * coef_w
                     + accs_ref[ch, 0:HEAD_DIM, :] * coef_s)
    for pair in range(HPG // 2):
        both = jnp.concatenate(heads[2 * pair:2 * pair + 2], axis=0)
        o_ref[0, :, pair * LANES:(pair + 1) * LANES] = both.T.astype(BF16)


def _attention(q3, gn3, kcmp, vcmp_t, ks4, vst, kw4, vwt, ovt, pat):
    b, _, _, seq = q3.shape
    n_cmp = kcmp.shape[2]
    n_slc = seq // SEL_BLOCK
    chunks_per_seq = seq // VT_CHUNK
    per_group = lambda bi, g, i: (bi, g, 0, 0)
    return pl.pallas_call(
        _attn_kernel,
        grid=(b, N_KV_GROUPS, seq // TQ),
        in_specs=[
            pl.BlockSpec((1, HPG, LANES, TQ), lambda bi, g, i: (bi, g, 0, i)),
            pl.BlockSpec((1, LANES, TQ), lambda bi, g, i: (bi, 0, i)),
            pl.BlockSpec((None, None, n_cmp, LANES), per_group),
            pl.BlockSpec((None, None, V_ROWS, n_cmp),
                         lambda bi, g, i: (bi, N_KV_GROUPS + g, 0, 0)),
            pl.BlockSpec((None, None, seq, LANES), per_group),
            pl.BlockSpec((None, None, chunks_per_seq, V_ROWS, VT_CHUNK),
                         lambda bi, g, i: (bi, g, 0, 0, 0)),
            pl.BlockSpec((None, None, seq, LANES), per_group),
            pl.BlockSpec((None, None, chunks_per_seq, V_ROWS, VT_CHUNK),
                         lambda bi, g, i: (bi, g, 0, 0, 0)),
            pl.BlockSpec((n_slc, n_cmp), lambda bi, g, i: (0, 0)),
            _resident(pat.shape, lambda bi, g, i: (0, 0, 0)),
        ],
        out_specs=pl.BlockSpec((1, TQ, HPG * HEAD_DIM), lambda bi, g, i: (bi, i, g)),
        out_shape=jax.ShapeDtypeStruct((b, seq, NSA_WIDTH), BF16),
        scratch_shapes=[
            pltpu.VMEM((HPG, LANES, TQ), BF16),
            pltpu.VMEM((HPG, NK_SEL, TQ), F32),
            pltpu.VMEM((NK_SEL, HPG * TQ), BF16),
            pltpu.VMEM((HPG, TQ), F32),
            pltpu.VMEM((HPG, TQ), F32),
            pltpu.VMEM((HPG, TQ), F32),
            pltpu.VMEM((HPG, TQ), F32),
            pltpu.VMEM((HPG, V_ROWS, TQ), F32),
            pltpu.VMEM((HPG, V_ROWS, TQ), F32),
            pltpu.VMEM((HPG, V_ROWS, TQ), F32),
        ],
        compiler_params=pltpu.CompilerParams(
            dimension_semantics=("arbitrary", "arbitrary", "arbitrary"),
            vmem_limit_bytes=VMEM_LIMIT),
        name="nsa_attention",
    )(q3, gn3, kcmp, vcmp_t, ks4, vst, kw4, vwt, ovt, pat)


def _mlp_kernel(x_ref, u_ref, up_ref, yn_ref, gm_ref, wp_ref, sc_ref, wpp_ref, wpn_ref, wo_ref,
                nm_ref, w1_ref, w2_ref, nf_ref, o_ref, *, final, tiles_per_seq):
    st = pl.program_id(0) % tiles_per_seq
    prev = jnp.where(st > 0, up_ref[...], 0.0)
    y_pool = _pool_mix(u_ref[...], prev, st * TM_MLP, wp_ref, sc_ref)
    p1 = _dot(y_pool, wpp_ref[...])
    p2 = _dot(yn_ref[...], wpn_ref[...])
    ga = jax.nn.sigmoid(gm_ref[:, 0:D_MODEL])
    gb = jax.nn.sigmoid(gm_ref[:, D_MODEL:2 * D_MODEL])
    merged = ga * p1 + gb * p2
    x = x_ref[...] + _dot(merged.astype(BF16), wo_ref[...])
    h = _rms(x, nm_ref[...]).astype(BF16)
    acc = jnp.zeros((TM_MLP, D_MODEL), F32)
    for c in range(D_FF // FF_CHUNK):
        sl = slice(c * FF_CHUNK, (c + 1) * FF_CHUNK)
        a = jnp.square(jnp.maximum(_dot(h, w1_ref[:, sl]), 0.0)).astype(BF16)
        acc = acc + _dot(a, w2_ref[sl, :])
    x = x + acc
    if final:
        x = _rms(x, nf_ref[...])
    o_ref[...] = x


def _merge_mlp(x2, u, yn, gm, wp, sc, wpp, wpn, wo, nm, w1, w2, nf, layer, final, seq):
    n = x2.shape[0]
    halo_per_tile = TM_MLP // POOL_HALO
    row = lambda w_: pl.BlockSpec((TM_MLP, w_), lambda i: (i, 0))
    halo = pl.BlockSpec((POOL_HALO, POOL_WIDTH),
                        lambda i: (jnp.maximum(i * halo_per_tile - 1, 0), 0))
    res = lambda a: _resident(a.shape, lambda i: (0,) * a.ndim)
    lay = lambda a: _layer_resident(a, layer)
    return pl.pallas_call(
        functools.partial(_mlp_kernel, final=final, tiles_per_seq=seq // TM_MLP),
        grid=(n // TM_MLP,),
        in_specs=[row(D_MODEL), row(POOL_WIDTH), halo, row(NSA_WIDTH), row(2 * D_MODEL),
                  lay(wp), res(sc), lay(wpp), lay(wpn), lay(wo), res(nm), lay(w1), lay(w2),
                  res(nf)],
        out_specs=row(D_MODEL),
        out_shape=jax.ShapeDtypeStruct((n, D_MODEL), F32),
        compiler_params=pltpu.CompilerParams(
            dimension_semantics=("arbitrary",), vmem_limit_bytes=VMEM_LIMIT),
        name="merge_mlp",
    )(x2, u, u, yn, gm, wp, sc, wpp, wpn, wo, nm, w1, w2, nf)


def _rope_tables(pos):
    inv = ROPE_THETA ** (-jnp.arange(0, HEAD_DIM, 2, dtype=F32) / HEAD_DIM)
    ang = pos.astype(F32)[:, None] * inv[None, :]
    ang = jnp.concatenate([ang, ang, ang, ang], axis=-1)
    first_half = (jnp.arange(LANES) % HEAD_DIM) < HEAD_DIM // 2
    cos, sin = jnp.cos(ang), jnp.sin(ang)
    return cos, jnp.where(first_half, -sin, 0.0), jnp.where(first_half, 0.0, sin)


def _permute_w_in(w_in):
    o_q = POOL_WIDTH
    o_kv = o_q + NSA_WIDTH
    o_gn = o_kv + 6 * KV_WIDTH
    o_gm = o_gn + N_GATE
    depth = w_in.shape[0]
    gn = w_in[:, :, o_gn:o_gm].reshape(depth, D_MODEL, N_KV_GROUPS, HPG, 3)
    gn = gn.transpose(0, 1, 2, 4, 3).reshape(depth, D_MODEL, N_KV_GROUPS, 3 * HPG)
    gn = jnp.pad(gn, ((0, 0), (0, 0), (0, 0), (0, LANES // N_KV_GROUPS - 3 * HPG)))
    gn = gn.reshape(depth, D_MODEL, LANES)
    return jnp.concatenate(
        [w_in[:, :, 0:o_gn], w_in[:, :, o_gm:], gn], axis=-1).astype(BF16)


def kernel(x, norm_mix, w_in, w_pool, pool_scale, pe_k, pe_v, w_ck1, w_ck2, w_cv1, w_cv2,
           w_proj_pool, w_proj_nsa, w_out, norm_mlp, w_ff1, w_ff2, norm_final):
    b, seq, d = x.shape
    depth = w_in.shape[0]
    n = b * seq
    n_chunks = seq // CMP_STRIDE
    n_slc = seq // SEL_BLOCK
    assert n_slc <= HEAD_DIM, "the selection one-hot shares the 64 spare key lanes"
    assert seq >= WIN_KEYS and seq % TM_IN == 0

    w_in_p = _permute_w_in(w_in)
    w_pool_b = w_pool.astype(BF16)
    pe = jnp.stack([pe_k, pe_v], axis=1).reshape(depth, 2, 1, CMP_LEN * HEAD_DIM)
    w_c1 = jnp.stack([w_ck1, w_cv1], axis=1).astype(BF16)
    pad = jnp.zeros_like(w_ck2)
    w_c2 = jnp.stack([jnp.concatenate([w, pad], axis=-1) for w in (w_ck2, w_ck2, w_cv2, w_cv2)],
                     axis=1).astype(BF16)
    wpp, wpn, wo = w_proj_pool.astype(BF16), w_proj_nsa.astype(BF16), w_out.astype(BF16)
    w1, w2 = w_ff1.astype(BF16), w_ff2.astype(BF16)

    cos, slo, shi = _rope_tables(jnp.arange(seq))
    ccos, cslo, cshi = _rope_tables(jnp.arange(n_chunks) * CMP_STRIDE + CMP_LEN - 1)
    ident = (jnp.ones_like(ccos), jnp.zeros_like(cslo), jnp.zeros_like(cshi))
    cmp_tabs = [jnp.stack([t, i_], axis=0) for t, i_ in zip((ccos, cslo, cshi), ident)]
    cmp_start = jnp.arange(n_chunks) * CMP_STRIDE
    slc_start = jnp.arange(n_slc) * SEL_BLOCK
    ovt = ((cmp_start[None, :] <= slc_start[:, None] + SEL_BLOCK - 1)
           & (cmp_start[None, :] + CMP_LEN - 1 >= slc_start[:, None])).astype(BF16)
    k_loc = jnp.arange(NK_SEL)[:, None]
    t_loc = jnp.arange(TQ)[None, :]
    keep = jnp.stack([jnp.ones((NK_SEL, TQ), bool), k_loc <= t_loc, k_loc > t_loc,
                      jnp.zeros((NK_SEL, TQ), bool)])
    pat = jnp.where(keep, 0.0, NEG_INF).astype(F32)

    x2 = x.reshape(n, d)
    for l in range(depth):
        u, q, c4, ks, vst, kw, vwt, gm, gn = _inproj(
            x2, norm_mix[l][None, :], w_in_p, l, cos, slo, shi, b, seq)
        cmp_n, cmp_t = _compress(c4, pe[l], w_c1[l], w_c2[l], *cmp_tabs)
        y_nsa = _attention(
            q, gn, cmp_n, cmp_t,
            ks, vst, kw, vwt, ovt, pat)
        x2 = _merge_mlp(
            x2, u, y_nsa.reshape(n, NSA_WIDTH), gm, w_pool_b, pool_scale[l][None, :],
            wpp, wpn, wo, norm_mlp[l][None, :], w1, w2, norm_final[None, :],
            layer=l, final=(l == depth - 1), seq=seq)
    return x2.reshape(b, seq, d)
```

```python
import functools

import jax
import jax.numpy as jnp
import numpy as np
from jax import lax
from jax.experimental import pallas as pl
from jax.experimental.pallas import tpu as pltpu

F32 = jnp.float32
BF16 = jnp.bfloat16

D_MODEL = 1024
POOL_WINDOWS = (2, 4, 8, 16)
POOL_WIDTH = 512
POOL_GW = 128
N_HEADS = 16
HEAD_DIM = 64
N_KV_GROUPS = 2
HPG = 8
NSA_WIDTH = 1024
KV_WIDTH = 128
CMP_LEN = 32
CMP_STRIDE = 16
CMP_HIDDEN = 256
SEL_BLOCK = 64
N_SEL = 16
WINDOW = 512
SEL_BONUS = 1e4
NEG_INF = -1e30
ROPE_THETA = 10000.0
D_FF = 4096
RMS_EPS = 1e-6
N_GATE = 3 * N_HEADS
Q_SCALE = HEAD_DIM ** -0.5 * float(np.log2(np.e))

LANES = 128
VMEM_LIMIT = 56 * 1024 * 1024

C_U = 0
C_Q = C_U + POOL_WIDTH
C_KV = C_Q + NSA_WIDTH
C_GM = C_KV + 6 * KV_WIDTH
C_GN = C_GM + 2 * D_MODEL
N_INP = C_GN + LANES

TM_IN = 512
POOL_HALO = 16
TM_MLP = 512
FF_CHUNK = 1024
TQ = 256
NK_SEL = 256
WIN_KEYS = WINDOW + TQ
VT_CHUNK = 128
V_ROWS = HEAD_DIM + 16
PAT_ALL, PAT_CAUSAL, PAT_ABOVE, PAT_NONE = 0, 1, 2, 3
assert NK_SEL == TQ and WINDOW == 2 * NK_SEL and WIN_KEYS % NK_SEL == 0


def _dot(a, b):
    return jnp.dot(a, b, preferred_element_type=F32)


def _rms(x, g):
    return x * lax.rsqrt(jnp.mean(x * x, axis=-1, keepdims=True) + RMS_EPS) * g


def _rope(t, cos, sin_lo, sin_hi):
    return t * cos + pltpu.roll(t, LANES - 32, 1) * sin_lo + pltpu.roll(t, 32, 1) * sin_hi


def _resident(shape, index_map):
    return pl.BlockSpec(shape, index_map, pipeline_mode=pl.Buffered(1))


def _layer_resident(stacked, layer):
    nd = stacked.ndim - 1
    return _resident((None,) + stacked.shape[1:], lambda i: (layer,) + (0,) * nd)


def _value_rows(v_t):
    tail_row = lax.broadcasted_iota(jnp.int32, (V_ROWS - HEAD_DIM, v_t.shape[1]), 0)
    return jnp.concatenate([v_t, jnp.where(tail_row == 0, 1.0, 0.0)], axis=0)


def _inproj_kernel(x_ref, g_ref, w_ref, cos_ref, slo_ref, shi_ref,
                   u_ref, q_ref, cmp_ref, ks_ref, vst_ref, kw_ref, vwt_ref, gm_ref, gn_ref,
                   kv_scr, *, tiles_per_seq):
    h = _rms(x_ref[...], g_ref[...]).astype(BF16)
    cos, slo, shi = cos_ref[...], slo_ref[...], shi_ref[...]
    lane = lax.broadcasted_iota(jnp.int32, (TM_IN, LANES), 1)
    low = lane < HEAD_DIM
    pos = (pl.program_id(0) % tiles_per_seq) * TM_IN + lax.broadcasted_iota(
        jnp.int32, (TM_IN, LANES), 0)
    block_onehot = jnp.where(lane - HEAD_DIM == pos // SEL_BLOCK, 1.0, 0.0)

    u_ref[...] = _dot(h, w_ref[:, C_U:C_Q])
    q = _dot(h, w_ref[:, C_Q:C_KV])
    for k in range(NSA_WIDTH // LANES):
        qt = (_rope(q[:, k * LANES:(k + 1) * LANES], cos, slo, shi) * Q_SCALE).T
        spare = jnp.zeros((LANES - HEAD_DIM, TM_IN), F32)
        for par in range(2):
            q_ref[0, 2 * k + par] = jnp.concatenate(
                [qt[par * HEAD_DIM:(par + 1) * HEAD_DIM, :], spare], axis=0).astype(BF16)
    kv = _dot(h, w_ref[:, C_KV:C_GM])
    n_rows = TM_IN // CMP_STRIDE
    low_c = lax.broadcasted_iota(jnp.int32, (n_rows, LANES), 1) < HEAD_DIM
    for t in range(2):
        kv_scr[t] = kv[:, t * LANES:(t + 1) * LANES]
        for pp in range(CMP_STRIDE // 2):
            a = kv_scr[t, pl.ds(2 * pp, n_rows, stride=CMP_STRIDE), :]
            b = kv_scr[t, pl.ds(2 * pp + 1, n_rows, stride=CMP_STRIDE), :]
            sl = slice(pp * LANES, (pp + 1) * LANES)
            cmp_ref[0, 2 * t, :, sl] = jnp.where(low_c, a, pltpu.roll(b, HEAD_DIM, 1))
            cmp_ref[0, 2 * t + 1, :, sl] = jnp.where(low_c, pltpu.roll(a, HEAD_DIM, 1), b)
    ks = _rope(kv[:, 2 * KV_WIDTH:3 * KV_WIDTH], cos, slo, shi)
    kw = _rope(kv[:, 4 * KV_WIDTH:5 * KV_WIDTH], cos, slo, shi)
    vs_t = kv[:, 3 * KV_WIDTH:4 * KV_WIDTH].T
    vw_t = kv[:, 5 * KV_WIDTH:6 * KV_WIDTH].T
    for g in range(N_KV_GROUPS):
        ks_g = ks if g == 0 else pltpu.roll(ks, HEAD_DIM, 1)
        kw_g = kw if g == 0 else pltpu.roll(kw, HEAD_DIM, 1)
        ks_ref[0, g] = jnp.where(low, ks_g, block_onehot).astype(BF16)
        kw_ref[0, g] = jnp.where(low, kw_g, 0.0).astype(BF16)
        vs_g = _value_rows(vs_t[g * HEAD_DIM:(g + 1) * HEAD_DIM, :]).astype(BF16)
        vw_g = _value_rows(vw_t[g * HEAD_DIM:(g + 1) * HEAD_DIM, :]).astype(BF16)
        for c in range(TM_IN // VT_CHUNK):
            sl = slice(c * VT_CHUNK, (c + 1) * VT_CHUNK)
            vst_ref[0, g, c] = vs_g[:, sl]
            vwt_ref[0, g, c] = vw_g[:, sl]
    gm_ref[...] = _dot(h, w_ref[:, C_GM:C_GN])
    gn_ref[0] = jax.nn.sigmoid(_dot(h, w_ref[:, C_GN:N_INP])).T


def _inproj(x2, g, w, layer, cos, slo, shi, b, seq):
    n = x2.shape[0]
    tiles_per_seq = seq // TM_IN
    n_chunks = TM_IN // VT_CHUNK
    row = lambda w_: pl.BlockSpec((TM_IN, w_), lambda i: (i, 0))
    tab = pl.BlockSpec((TM_IN, LANES), lambda i: (i % tiles_per_seq, 0))
    kg = pl.BlockSpec((1, N_KV_GROUPS, TM_IN, LANES),
                      lambda i: (i // tiles_per_seq, 0, i % tiles_per_seq, 0))
    vt = pl.BlockSpec((1, N_KV_GROUPS, n_chunks, V_ROWS, VT_CHUNK),
                      lambda i: (i // tiles_per_seq, 0, i % tiles_per_seq, 0, 0))
    cmp_spec = pl.BlockSpec(
        (1, 2 * N_KV_GROUPS, TM_IN // CMP_STRIDE, CMP_STRIDE * HEAD_DIM),
        lambda i: (i // tiles_per_seq, 0, i % tiles_per_seq, 0))
    qt_spec = pl.BlockSpec((1, N_HEADS, LANES, TM_IN),
                           lambda i: (i // tiles_per_seq, 0, 0, i % tiles_per_seq))
    gt_spec = pl.BlockSpec((1, LANES, TM_IN), lambda i: (i // tiles_per_seq, 0, i % tiles_per_seq))
    k_shape = jax.ShapeDtypeStruct((b, N_KV_GROUPS, seq, LANES), BF16)
    vt_shape = jax.ShapeDtypeStruct((b, N_KV_GROUPS, seq // VT_CHUNK, V_ROWS, VT_CHUNK), BF16)
    return pl.pallas_call(
        functools.partial(_inproj_kernel, tiles_per_seq=tiles_per_seq),
        grid=(n // TM_IN,),
        in_specs=[row(D_MODEL), _resident((1, D_MODEL), lambda i: (0, 0)),
                  _layer_resident(w, layer), tab, tab, tab],
        out_specs=[row(POOL_WIDTH), qt_spec, cmp_spec, kg, vt,
                   kg, vt, row(2 * D_MODEL), gt_spec],
        out_shape=[
            jax.ShapeDtypeStruct((n, POOL_WIDTH), F32),
            jax.ShapeDtypeStruct((b, N_HEADS, LANES, seq), BF16),
            jax.ShapeDtypeStruct(
                (b, 2 * N_KV_GROUPS, seq // CMP_STRIDE, CMP_STRIDE * HEAD_DIM), F32),
            k_shape, vt_shape, k_shape, vt_shape,
            jax.ShapeDtypeStruct((n, 2 * D_MODEL), F32),
            jax.ShapeDtypeStruct((b, LANES, seq), F32),
        ],
        scratch_shapes=[pltpu.VMEM((2, TM_IN, LANES), F32)],
        compiler_params=pltpu.CompilerParams(
            dimension_semantics=("arbitrary",), vmem_limit_bytes=VMEM_LIMIT),
        name="in_proj",
    )(x2, g, w, cos, slo, shi)


def _pool_mix(cur, prev, t0, wp_ref, sc_ref):
    t = t0 + lax.broadcasted_iota(jnp.int32, (cur.shape[0], POOL_GW), 0)
    out = []
    for g, w in enumerate(POOL_WINDOWS):
        sl = slice(g * POOL_GW, (g + 1) * POOL_GW)
        cg = cur[:, sl]
        s = jnp.concatenate([prev[:, sl], cg], axis=0)
        sh = 1
        while sh < w:
            s = s + pltpu.roll(s, sh, 0)
            sh *= 2
        cnt = jnp.minimum(t + 1, w).astype(F32)
        d = s[POOL_HALO:] / cnt - cg
        out.append((_dot(d.astype(BF16), wp_ref[g]) * sc_ref[:, sl]).astype(BF16))
    return jnp.concatenate(out, axis=1)


def _compress_kernel(c_ref, pe_ref, w1_ref, w2_ref, cos_ref, slo_ref, shi_ref, o_ref, ot_ref):
    half = CMP_STRIDE * HEAD_DIM
    c = c_ref[...]
    a = _dot((c + pe_ref[:, 0:half]).astype(BF16), w1_ref[0:half, :])
    b = _dot((c + pe_ref[:, half:2 * half]).astype(BF16), w1_ref[half:2 * half, :])
    n_rows = c.shape[0]
    hid = a + pltpu.roll(b, n_rows - 1, 0)
    act = jax.nn.gelu(hid, approximate=True)
    out = _dot(act.astype(BF16), w2_ref[...])
    out = _rope(out, cos_ref[...], slo_ref[...], shi_ref[...])
    row = lax.broadcasted_iota(jnp.int32, out.shape, 0)
    out = jnp.where(row < n_rows - 1, out, 0.0)
    o_ref[...] = out.astype(BF16)
    ot_ref[...] = _value_rows(out.T[0:HEAD_DIM, :]).astype(BF16)


def _compress(c4, pe, w1, w2p, cos, slo, shi):
    b, n_kv, n_chunks, half = c4.shape
    return pl.pallas_call(
        _compress_kernel,
        grid=(b, n_kv),
        in_specs=[
            pl.BlockSpec((None, None, n_chunks, half), lambda bi, j: (bi, j, 0, 0)),
            pl.BlockSpec((None, 1, 2 * half), lambda bi, j: (j // N_KV_GROUPS, 0, 0)),
            pl.BlockSpec((None, 2 * half, CMP_HIDDEN), lambda bi, j: (j // N_KV_GROUPS, 0, 0)),
            pl.BlockSpec((None, CMP_HIDDEN, LANES), lambda bi, j: (j, 0, 0)),
            pl.BlockSpec((None, n_chunks, LANES), lambda bi, j: (j // N_KV_GROUPS, 0, 0)),
            pl.BlockSpec((None, n_chunks, LANES), lambda bi, j: (j // N_KV_GROUPS, 0, 0)),
            pl.BlockSpec((None, n_chunks, LANES), lambda bi, j: (j // N_KV_GROUPS, 0, 0)),
        ],
        out_specs=[
            pl.BlockSpec((None, None, n_chunks, LANES), lambda bi, j: (bi, j, 0, 0)),
            pl.BlockSpec((None, None, V_ROWS, n_chunks), lambda bi, j: (bi, j, 0, 0)),
        ],
        out_shape=[
            jax.ShapeDtypeStruct((b, n_kv, n_chunks, LANES), BF16),
            jax.ShapeDtypeStruct((b, n_kv, V_ROWS, n_chunks), BF16),
        ],
        compiler_params=pltpu.CompilerParams(dimension_semantics=("arbitrary", "arbitrary")),
        name="compress",
    )(c4, pe, w1, w2p, cos, slo, shi)


def _block_rank(score):
    n_slc = score.shape[0]
    sub = 8
    ranks = []
    for v in range(n_slc // sub):
        blk = score[v * sub:(v + 1) * sub, :]
        jb_v = v * sub + lax.broadcasted_iota(jnp.int32, blk.shape, 0)
        r = jnp.zeros(blk.shape, F32)
        for jp in range(n_slc):
            row = score[jp:jp + 1, :]
            ge = jnp.where(row >= blk, 1.0, 0.0)
            gt = jnp.where(row > blk, 1.0, 0.0)
            if jp < v * sub:
                r = r + ge
            elif jp >= (v + 1) * sub:
                r = r + gt
            else:
                r = r + jnp.where(jb_v > jp, ge, gt)
        ranks.append(r)
    return jnp.concatenate(ranks, axis=0)


def _attn_kernel(q_ref, gn_ref, kc_ref, vct_ref, ks_ref, vst_ref, kw_ref, vwt_ref, ovt_ref,
                 pat_ref, o_ref, qsel_ref, s_ref, p_ref, al_ref, mc_ref, mw_ref, ms_ref,
                 accc_ref, accw_ref, accs_ref):
    g = pl.program_id(1)
    i = pl.program_id(2)
    s0 = i * TQ
    n_cmp = kc_ref.shape[0]
    n_slc = ovt_ref.shape[0]
    n_ch = HPG

    q_plain = lambda ch: q_ref[0, ch]
    q_selected = lambda ch: qsel_ref[ch]

    g_base = g * HEAD_DIM

    def gate_rows(c, ch):
        return gn_ref[0, pl.ds(g_base + c * HPG + ch, 1), :]

    def key_tile(ref, k0, nk):
        return ref[pl.ds(pl.multiple_of(k0, VT_CHUNK), nk), :]

    def value_tile(ref, k0, nk):
        c0 = k0 // VT_CHUNK
        return jnp.concatenate([ref[c0 + c] for c in range(nk // VT_CHUNK)], axis=1)

    imp_raw = []

    def run_step(soft=None, score=None, value=None, importance=False):
        for ch in range(n_ch):
            if soft is not None:
                nk, m_ref, first = soft
                s = s_ref[ch, 0:nk, :]
                mx = jnp.max(s, axis=0, keepdims=True)
                if first:
                    m_new, al_new = mx, None
                else:
                    m_prev = m_ref[ch:ch + 1, :]
                    m_new = jnp.maximum(m_prev, mx)
                    al_new = jnp.exp2(m_prev - m_new)
                p_new = jnp.exp2(s - m_new).astype(BF16)
            if score is not None:
                kt, q_chunk, bias = score
                sc = _dot(kt, q_chunk(ch))
                s_ref[ch, 0:kt.shape[0], :] = sc if bias is None else sc + bias()
            consumed = []
            if value is not None:
                vt, acc_ref, vfirst = value
                pv = _dot(vt, p_ref[ch, 0:vt.shape[1], :])
                acc_ref[ch] = pv if vfirst else al_ref[ch:ch + 1, :] * acc_ref[ch] + pv
                consumed.append(pv)
            if importance:
                imp_raw.append(_dot(ovt_ref[...], p_ref[ch, 0:n_cmp, :]))
                consumed.append(imp_raw[-1])
            if soft is not None:
                for r in consumed:
                    p_new = p_new + (r[0:1, :] * 0.0).astype(BF16)
                p_ref[ch, 0:nk, :] = p_new
                m_ref[ch:ch + 1, :] = m_new
                if al_new is not None:
                    al_ref[ch:ch + 1, :] = al_new

    n_idx = lax.broadcasted_iota(jnp.int32, (n_cmp, TQ), 0)
    t_cmp = s0 + lax.broadcasted_iota(jnp.int32, (n_cmp, TQ), 1)
    cmp_mask = jnp.where(n_idx * CMP_STRIDE + CMP_LEN - 1 <= t_cmp, 0.0, NEG_INF)
    cmp_bias = lambda: cmp_mask

    w0 = jnp.maximum(s0 - WINDOW, 0)
    win_tiles = [(off, min(NK_SEL, WIN_KEYS - off)) for off in range(0, WIN_KEYS, NK_SEL)]
    n_win = len(win_tiles)

    def win_bias(off, nk):
        d = w0 + off - s0
        pat = jnp.where(d == -WINDOW, PAT_ABOVE,
                        jnp.where(d == 0, PAT_CAUSAL, jnp.where(d < 0, PAT_ALL, PAT_NONE)))
        return lambda: pat_ref[pat]

    def win_step(k):
        args = {}
        if k < n_win:
            off, nk = win_tiles[k]
            args["score"] = (key_tile(kw_ref, w0 + off, nk), q_plain, win_bias(off, nk))
        if 1 <= k <= n_win:
            args["soft"] = (win_tiles[k - 1][1], mw_ref, k == 1)
        if 2 <= k <= n_win + 1:
            off, nk = win_tiles[k - 2]
            args["value"] = (value_tile(vwt_ref, w0 + off, nk), accw_ref, k == 2)
        return args

    run_step(score=(kc_ref[...], q_plain, cmp_bias))
    run_step(soft=(n_cmp, mc_ref, True), **win_step(0))
    run_step(value=(vct_ref[...], accc_ref, True), importance=True, **win_step(1))
    for k in range(2, n_win):
        run_step(**win_step(k))

    cmp_scale = [jnp.where(mc_ref[ch:ch + 1, :] > 0.5 * NEG_INF,
                           1.0 / accc_ref[ch, HEAD_DIM:HEAD_DIM + 1, :], 0.0)
                 for ch in range(n_ch)]

    imp = functools.reduce(lambda a, b: a + b, [r * sc for r, sc in zip(imp_raw, cmp_scale)])
    jb = lax.broadcasted_iota(jnp.int32, (n_slc, TQ), 0)
    tq = s0 + lax.broadcasted_iota(jnp.int32, (n_slc, TQ), 1)
    causal = jb * SEL_BLOCK <= tq
    near = jnp.logical_or(jb == 0, jb >= tq // SEL_BLOCK - 1)
    score = jnp.where(causal, jnp.where(near, SEL_BONUS, imp), NEG_INF)
    rank = jnp.concatenate(
        [_block_rank(score[:, c0:c0 + LANES]) for c0 in range(0, TQ, LANES)], axis=1)
    sel_bias = jnp.where(causal, jnp.where(rank < float(N_SEL), 0.0, NEG_INF), NEG_INF)
    parts = [jnp.zeros((HEAD_DIM, TQ), F32), sel_bias]
    if n_slc < HEAD_DIM:
        parts.append(jnp.zeros((HEAD_DIM - n_slc, TQ), F32))
    sel_rows = jnp.concatenate(parts, axis=0).astype(BF16)
    for hh in range(HPG):
        qsel_ref[hh] = q_ref[0, hh] + sel_rows

    ms_ref[...] = jnp.full(ms_ref.shape, NEG_INF, F32)
    accs_ref[...] = jnp.zeros(accs_ref.shape, F32)
    n_full = s0 // NK_SEL

    def sel_tile(m):
        return jnp.where(m == 0, n_full, m - 1)

    def sel_scores(m, bias=None):
        return (key_tile(ks_ref, sel_tile(m) * NK_SEL, NK_SEL), q_selected, bias)

    def sel_values(m, live):
        vt = value_tile(vst_ref, sel_tile(jnp.maximum(m, 0)) * NK_SEL, NK_SEL)
        return jnp.where(live, vt, jnp.zeros_like(vt))

    run_step(score=sel_scores(0, lambda: pat_ref[PAT_CAUSAL]), **win_step(n_win))
    run_step(**win_step(n_win + 1))

    def sel_body(m, carry):
        run_step(soft=(NK_SEL, ms_ref, False), score=sel_scores(m),
                 value=(sel_values(m - 2, m >= 2), accs_ref, False))
        return carry

    lax.fori_loop(1, n_full + 1, sel_body, 0)
    run_step(soft=(NK_SEL, ms_ref, False),
             value=(sel_values(n_full - 1, n_full >= 1), accs_ref, False))
    run_step(value=(sel_values(n_full, True), accs_ref, False))

    heads = []
    for ch in range(n_ch):
        coef_c = gate_rows(0, ch) * cmp_scale[ch]
        coef_w = gate_rows(2, ch) / accw_ref[ch, HEAD_DIM:HEAD_DIM + 1, :]
        coef_s = gate_rows(1, ch) / accs_ref[ch, HEAD_DIM:HEAD_DIM + 1, :]
        heads.append(accc_ref[ch, 0:HEAD_DIM, :] * coef_c + accw_ref[ch, 0:HEAD_DIM, :] * coef_w
                     + accs_ref[ch, 0:HEAD_DIM, :] * coef_s)
    for pair in range(HPG // 2):
        both = jnp.concatenate(heads[2 * pair:2 * pair + 2], axis=0)
        o_ref[0, :, pair * LANES:(pair + 1) * LANES] = both.T.astype(BF16)


def _attention(q3, gn3, kcmp, vcmp_t, ks4, vst, kw4, vwt, ovt, pat):
    b, _, _, seq = q3.shape
    n_cmp = kcmp.shape[2]
    n_slc = seq // SEL_BLOCK
    chunks_per_seq = seq // VT_CHUNK
    per_group = lambda bi, g, i: (bi, g, 0, 0)
    return pl.pallas_call(
        _attn_kernel,
        grid=(b, N_KV_GROUPS, seq // TQ),
        in_specs=[
            pl.BlockSpec((1, HPG, LANES, TQ), lambda bi, g, i: (bi, g, 0, i)),
            pl.BlockSpec((1, LANES, TQ), lambda bi, g, i: (bi, 0, i)),
            pl.BlockSpec((None, None, n_cmp, LANES), per_group),
            pl.BlockSpec((None, None, V_ROWS, n_cmp),
                         lambda bi, g, i: (bi, N_KV_GROUPS + g, 0, 0)),
            pl.BlockSpec((None, None, seq, LANES), per_group),
            pl.BlockSpec((None, None, chunks_per_seq, V_ROWS, VT_CHUNK),
                         lambda bi, g, i: (bi, g, 0, 0, 0)),
            pl.BlockSpec((None, None, seq, LANES), per_group),
            pl.BlockSpec((None, None, chunks_per_seq, V_ROWS, VT_CHUNK),
                         lambda bi, g, i: (bi, g, 0, 0, 0)),
            pl.BlockSpec((n_slc, n_cmp), lambda bi, g, i: (0, 0)),
            _resident(pat.shape, lambda bi, g, i: (0, 0, 0)),
        ],
        out_specs=pl.BlockSpec((1, TQ, HPG * HEAD_DIM), lambda bi, g, i: (bi, i, g)),
        out_shape=jax.ShapeDtypeStruct((b, seq, NSA_WIDTH), BF16),
        scratch_shapes=[
            pltpu.VMEM((HPG, LANES, TQ), BF16),
            pltpu.VMEM((HPG, NK_SEL, TQ), F32),
            pltpu.VMEM((HPG, NK_SEL, TQ), BF16),
            pltpu.VMEM((HPG, TQ), F32),
            pltpu.VMEM((HPG, TQ), F32),
            pltpu.VMEM((HPG, TQ), F32),
            pltpu.VMEM((HPG, TQ), F32),
            pltpu.VMEM((HPG, V_ROWS, TQ), F32),
            pltpu.VMEM((HPG, V_ROWS, TQ), F32),
            pltpu.VMEM((HPG, V_ROWS, TQ), F32),
        ],
        compiler_params=pltpu.CompilerParams(
            dimension_semantics=("arbitrary", "arbitrary", "arbitrary"),
            vmem_limit_bytes=VMEM_LIMIT),
        name="nsa_attention",
    )(q3, gn3, kcmp, vcmp_t, ks4, vst, kw4, vwt, ovt, pat)


def _mlp_kernel(x_ref, u_ref, up_ref, yn_ref, gm_ref, wp_ref, sc_ref, wpp_ref, wpn_ref, wo_ref,
                nm_ref, w1_ref, w2_ref, nf_ref, o_ref, *, final, tiles_per_seq):
    st = pl.program_id(0) % tiles_per_seq
    prev = jnp.where(st > 0, up_ref[...], 0.0)
    y_pool = _pool_mix(u_ref[...], prev, st * TM_MLP, wp_ref, sc_ref)
    p1 = _dot(y_pool, wpp_ref[...])
    p2 = _dot(yn_ref[...], wpn_ref[...])
    ga = jax.nn.sigmoid(gm_ref[:, 0:D_MODEL])
    gb = jax.nn.sigmoid(gm_ref[:, D_MODEL:2 * D_MODEL])
    merged = ga * p1 + gb * p2
    x = x_ref[...] + _dot(merged.astype(BF16), wo_ref[...])
    h = _rms(x, nm_ref[...]).astype(BF16)
    acc = jnp.zeros((TM_MLP, D_MODEL), F32)
    for c in range(D_FF // FF_CHUNK):
        sl = slice(c * FF_CHUNK, (c + 1) * FF_CHUNK)
        a = jnp.square(jnp.maximum(_dot(h, w1_ref[:, sl]), 0.0)).astype(BF16)
        acc = acc + _dot(a, w2_ref[sl, :])
    x = x + acc
    if final:
        x = _rms(x, nf_ref[...])
    o_ref[...] = x


def _merge_mlp(x2, u, yn, gm, wp, sc, wpp, wpn, wo, nm, w1, w2, nf, layer, final, seq):
    n = x2.shape[0]
    halo_per_tile = TM_MLP // POOL_HALO
    row = lambda w_: pl.BlockSpec((TM_MLP, w_), lambda i: (i, 0))
    halo = pl.BlockSpec((POOL_HALO, POOL_WIDTH),
                        lambda i: (jnp.maximum(i * halo_per_tile - 1, 0), 0))
    res = lambda a: _resident(a.shape, lambda i: (0,) * a.ndim)
    lay = lambda a: _layer_resident(a, layer)
    return pl.pallas_call(
        functools.partial(_mlp_kernel, final=final, tiles_per_seq=seq // TM_MLP),
        grid=(n // TM_MLP,),
        in_specs=[row(D_MODEL), row(POOL_WIDTH), halo, row(NSA_WIDTH), row(2 * D_MODEL),
                  lay(wp), res(sc), lay(wpp), lay(wpn), lay(wo), res(nm), lay(w1), lay(w2),
                  res(nf)],
        out_specs=row(D_MODEL),
        out_shape=jax.ShapeDtypeStruct((n, D_MODEL), F32),
        compiler_params=pltpu.CompilerParams(
            dimension_semantics=("arbitrary",), vmem_limit_bytes=VMEM_LIMIT),
        name="merge_mlp",
    )(x2, u, u, yn, gm, wp, sc, wpp, wpn, wo, nm, w1, w2, nf)


def _rope_tables(pos):
    inv = ROPE_THETA ** (-jnp.arange(0, HEAD_DIM, 2, dtype=F32) / HEAD_DIM)
    ang = pos.astype(F32)[:, None] * inv[None, :]
    ang = jnp.concatenate([ang, ang, ang, ang], axis=-1)
    first_half = (jnp.arange(LANES) % HEAD_DIM) < HEAD_DIM // 2
    cos, sin = jnp.cos(ang), jnp.sin(ang)
    return cos, jnp.where(first_half, -sin, 0.0), jnp.where(first_half, 0.0, sin)


def _permute_w_in(w_in):
    o_q = POOL_WIDTH
    o_kv = o_q + NSA_WIDTH
    o_gn = o_kv + 6 * KV_WIDTH
    o_gm = o_gn + N_GATE
    depth = w_in.shape[0]
    gn = w_in[:, :, o_gn:o_gm].reshape(depth, D_MODEL, N_KV_GROUPS, HPG, 3)
    gn = gn.transpose(0, 1, 2, 4, 3).reshape(depth, D_MODEL, N_KV_GROUPS, 3 * HPG)
    gn = jnp.pad(gn, ((0, 0), (0, 0), (0, 0), (0, LANES // N_KV_GROUPS - 3 * HPG)))
    gn = gn.reshape(depth, D_MODEL, LANES)
    return jnp.concatenate(
        [w_in[:, :, 0:o_gn], w_in[:, :, o_gm:], gn], axis=-1).astype(BF16)


def kernel(x, norm_mix, w_in, w_pool, pool_scale, pe_k, pe_v, w_ck1, w_ck2, w_cv1, w_cv2,
           w_proj_pool, w_proj_nsa, w_out, norm_mlp, w_ff1, w_ff2, norm_final):
    b, seq, d = x.shape
    depth = w_in.shape[0]
    n = b * seq
    n_chunks = seq // CMP_STRIDE
    n_slc = seq // SEL_BLOCK
    assert n_slc <= HEAD_DIM, "the selection one-hot shares the 64 spare key lanes"
    assert seq >= WIN_KEYS and seq % TM_IN == 0

    w_in_p = _permute_w_in(w_in)
    w_pool_b = w_pool.astype(BF16)
    pe = jnp.stack([pe_k, pe_v], axis=1).reshape(depth, 2, 1, CMP_LEN * HEAD_DIM)
    w_c1 = jnp.stack([w_ck1, w_cv1], axis=1).astype(BF16)
    pad = jnp.zeros_like(w_ck2)
    w_c2 = jnp.stack([jnp.concatenate([w, pad], axis=-1) for w in (w_ck2, w_ck2, w_cv2, w_cv2)],
                     axis=1).astype(BF16)
    wpp, wpn, wo = w_proj_pool.astype(BF16), w_proj_nsa.astype(BF16), w_out.astype(BF16)
    w1, w2 = w_ff1.astype(BF16), w_ff2.astype(BF16)

    cos, slo, shi = _rope_tables(jnp.arange(seq))
    ccos, cslo, cshi = _rope_tables(jnp.arange(n_chunks) * CMP_STRIDE + CMP_LEN - 1)
    ident = (jnp.ones_like(ccos), jnp.zeros_like(cslo), jnp.zeros_like(cshi))
    cmp_tabs = [jnp.stack([t, i_], axis=0) for t, i_ in zip((ccos, cslo, cshi), ident)]
    cmp_start = jnp.arange(n_chunks) * CMP_STRIDE
    slc_start = jnp.arange(n_slc) * SEL_BLOCK
    ovt = ((cmp_start[None, :] <= slc_start[:, None] + SEL_BLOCK - 1)
           & (cmp_start[None, :] + CMP_LEN - 1 >= slc_start[:, None])).astype(BF16)
    k_loc = jnp.arange(NK_SEL)[:, None]
    t_loc = jnp.arange(TQ)[None, :]
    keep = jnp.stack([jnp.ones((NK_SEL, TQ), bool), k_loc <= t_loc, k_loc > t_loc,
                      jnp.zeros((NK_SEL, TQ), bool)])
    pat = jnp.where(keep, 0.0, NEG_INF).astype(F32)

    x2 = x.reshape(n, d)
    for l in range(depth):
        u, q, c4, ks, vst, kw, vwt, gm, gn = _inproj(
            x2, norm_mix[l][None, :], w_in_p, l, cos, slo, shi, b, seq)
        cmp_n, cmp_t = _compress(c4, pe[l], w_c1[l], w_c2[l], *cmp_tabs)
        y_nsa = _attention(
            q, gn, cmp_n, cmp_t,
            ks, vst, kw, vwt, ovt, pat)
        x2 = _merge_mlp(
            x2, u, y_nsa.reshape(n, NSA_WIDTH), gm, w_pool_b, pool_scale[l][None, :],
            wpp, wpn, wo, norm_mlp[l][None, :], w1, w2, norm_final[None, :],
            layer=l, final=(l == depth - 1), seq=seq)
    return x2.reshape(b, seq, d)
```

```python
import functools

import jax
import jax.numpy as jnp
import numpy as np
from jax import lax
from jax.experimental import pallas as pl
from jax.experimental.pallas import tpu as pltpu

F32 = jnp.float32
BF16 = jnp.bfloat16

D_MODEL = 1024
POOL_WINDOWS = (2, 4, 8, 16)
POOL_WIDTH = 512
POOL_GW = 128
N_HEADS = 16
HEAD_DIM = 64
N_KV_GROUPS = 2
HPG = 8
NSA_WIDTH = 1024
KV_WIDTH = 128
CMP_LEN = 32
CMP_STRIDE = 16
CMP_HIDDEN = 256
SEL_BLOCK = 64
N_SEL = 16
WINDOW = 512
SEL_BONUS = 1e4
NEG_INF = -1e30
ROPE_THETA = 10000.0
D_FF = 4096
RMS_EPS = 1e-6
N_GATE = 3 * N_HEADS
Q_SCALE = HEAD_DIM ** -0.5 * float(np.log2(np.e))

LANES = 128
VMEM_LIMIT = 56 * 1024 * 1024

C_U = 0
C_Q = C_U + POOL_WIDTH
C_KV = C_Q + NSA_WIDTH
C_GM = C_KV + 6 * KV_WIDTH
C_GN = C_GM + 2 * D_MODEL
N_INP = C_GN + LANES

TM_IN = 512
POOL_HALO = 16
TM_MLP = 512
FF_CHUNK = 1024
TQ = 256
NK_SEL = 256
WIN_KEYS = WINDOW + TQ
VT_CHUNK = 128
V_ROWS = HEAD_DIM + 16
PAT_ALL, PAT_CAUSAL, PAT_ABOVE, PAT_NONE = 0, 1, 2, 3
assert NK_SEL == TQ and WINDOW == 2 * NK_SEL and WIN_KEYS % NK_SEL == 0


def _dot(a, b):
    return jnp.dot(a, b, preferred_element_type=F32)


def _rms(x, g):
    return x * lax.rsqrt(jnp.mean(x * x, axis=-1, keepdims=True) + RMS_EPS) * g


def _rope(t, cos, sin_lo, sin_hi):
    return t * cos + pltpu.roll(t, LANES - 32, 1) * sin_lo + pltpu.roll(t, 32, 1) * sin_hi


def _resident(shape, index_map):
    return pl.BlockSpec(shape, index_map, pipeline_mode=pl.Buffered(1))


def _layer_resident(stacked, layer):
    nd = stacked.ndim - 1
    return _resident((None,) + stacked.shape[1:], lambda i: (layer,) + (0,) * nd)


def _value_rows(v_t):
    tail_row = lax.broadcasted_iota(jnp.int32, (V_ROWS - HEAD_DIM, v_t.shape[1]), 0)
    return jnp.concatenate([v_t, jnp.where(tail_row == 0, 1.0, 0.0)], axis=0)


def _inproj_kernel(x_ref, g_ref, w_ref, cos_ref, slo_ref, shi_ref,
                   u_ref, q_ref, cmp_ref, ks_ref, vst_ref, kw_ref, vwt_ref, gm_ref, gn_ref,
                   kv_scr, *, tiles_per_seq):
    h = _rms(x_ref[...], g_ref[...]).astype(BF16)
    cos, slo, shi = cos_ref[...], slo_ref[...], shi_ref[...]
    lane = lax.broadcasted_iota(jnp.int32, (TM_IN, LANES), 1)
    low = lane < HEAD_DIM
    pos = (pl.program_id(0) % tiles_per_seq) * TM_IN + lax.broadcasted_iota(
        jnp.int32, (TM_IN, LANES), 0)
    block_onehot = jnp.where(lane - HEAD_DIM == pos // SEL_BLOCK, 1.0, 0.0)

    u_ref[...] = _dot(h, w_ref[:, C_U:C_Q])
    q = _dot(h, w_ref[:, C_Q:C_KV])
    for k in range(NSA_WIDTH // LANES):
        qt = (_rope(q[:, k * LANES:(k + 1) * LANES], cos, slo, shi) * Q_SCALE).T
        spare = jnp.zeros((LANES - HEAD_DIM, TM_IN), F32)
        for par in range(2):
            q_ref[0, 2 * k + par] = jnp.concatenate(
                [qt[par * HEAD_DIM:(par + 1) * HEAD_DIM, :], spare], axis=0).astype(BF16)
    kv = _dot(h, w_ref[:, C_KV:C_GM])
    n_rows = TM_IN // CMP_STRIDE
    low_c = lax.broadcasted_iota(jnp.int32, (n_rows, LANES), 1) < HEAD_DIM
    for t in range(2):
        kv_scr[t] = kv[:, t * LANES:(t + 1) * LANES]
        for pp in range(CMP_STRIDE // 2):
            a = kv_scr[t, pl.ds(2 * pp, n_rows, stride=CMP_STRIDE), :]
            b = kv_scr[t, pl.ds(2 * pp + 1, n_rows, stride=CMP_STRIDE), :]
            sl = slice(pp * LANES, (pp + 1) * LANES)
            cmp_ref[0, 2 * t, :, sl] = jnp.where(low_c, a, pltpu.roll(b, HEAD_DIM, 1))
            cmp_ref[0, 2 * t + 1, :, sl] = jnp.where(low_c, pltpu.roll(a, HEAD_DIM, 1), b)
    ks = _rope(kv[:, 2 * KV_WIDTH:3 * KV_WIDTH], cos, slo, shi)
    kw = _rope(kv[:, 4 * KV_WIDTH:5 * KV_WIDTH], cos, slo, shi)
    vs_t = kv[:, 3 * KV_WIDTH:4 * KV_WIDTH].T
    vw_t = kv[:, 5 * KV_WIDTH:6 * KV_WIDTH].T
    for g in range(N_KV_GROUPS):
        ks_g = ks if g == 0 else pltpu.roll(ks, HEAD_DIM, 1)
        kw_g = kw if g == 0 else pltpu.roll(kw, HEAD_DIM, 1)
        ks_ref[0, g] = jnp.where(low, ks_g, block_onehot).astype(BF16)
        kw_ref[0, g] = jnp.where(low, kw_g, 0.0).astype(BF16)
        vs_g = _value_rows(vs_t[g * HEAD_DIM:(g + 1) * HEAD_DIM, :]).astype(BF16)
        vw_g = _value_rows(vw_t[g * HEAD_DIM:(g + 1) * HEAD_DIM, :]).astype(BF16)
        for c in range(TM_IN // VT_CHUNK):
            sl = slice(c * VT_CHUNK, (c + 1) * VT_CHUNK)
            vst_ref[0, g, c] = vs_g[:, sl]
            vwt_ref[0, g, c] = vw_g[:, sl]
    gm_ref[...] = _dot(h, w_ref[:, C_GM:C_GN])
    gn_ref[0] = jax.nn.sigmoid(_dot(h, w_ref[:, C_GN:N_INP])).T


def _inproj(x2, g, w, layer, cos, slo, shi, b, seq):
    n = x2.shape[0]
    tiles_per_seq = seq // TM_IN
    n_chunks = TM_IN // VT_CHUNK
    row = lambda w_: pl.BlockSpec((TM_IN, w_), lambda i: (i, 0))
    tab = pl.BlockSpec((TM_IN, LANES), lambda i: (i % tiles_per_seq, 0))
    kg = pl.BlockSpec((1, N_KV_GROUPS, TM_IN, LANES),
                      lambda i: (i // tiles_per_seq, 0, i % tiles_per_seq, 0))
    vt = pl.BlockSpec((1, N_KV_GROUPS, n_chunks, V_ROWS, VT_CHUNK),
                      lambda i: (i // tiles_per_seq, 0, i % tiles_per_seq, 0, 0))
    cmp_spec = pl.BlockSpec(
        (1, 2 * N_KV_GROUPS, TM_IN // CMP_STRIDE, CMP_STRIDE * HEAD_DIM),
        lambda i: (i // tiles_per_seq, 0, i % tiles_per_seq, 0))
    qt_spec = pl.BlockSpec((1, N_HEADS, LANES, TM_IN),
                           lambda i: (i // tiles_per_seq, 0, 0, i % tiles_per_seq))
    gt_spec = pl.BlockSpec((1, LANES, TM_IN), lambda i: (i // tiles_per_seq, 0, i % tiles_per_seq))
    k_shape = jax.ShapeDtypeStruct((b, N_KV_GROUPS, seq, LANES), BF16)
    vt_shape = jax.ShapeDtypeStruct((b, N_KV_GROUPS, seq // VT_CHUNK, V_ROWS, VT_CHUNK), BF16)
    return pl.pallas_call(
        functools.partial(_inproj_kernel, tiles_per_seq=tiles_per_seq),
        grid=(n // TM_IN,),
        in_specs=[row(D_MODEL), _resident((1, D_MODEL), lambda i: (0, 0)),
                  _layer_resident(w, layer), tab, tab, tab],
        out_specs=[row(POOL_WIDTH), qt_spec, cmp_spec, kg, vt,
                   kg, vt, row(2 * D_MODEL), gt_spec],
        out_shape=[
            jax.ShapeDtypeStruct((n, POOL_WIDTH), F32),
            jax.ShapeDtypeStruct((b, N_HEADS, LANES, seq), BF16),
            jax.ShapeDtypeStruct(
                (b, 2 * N_KV_GROUPS, seq // CMP_STRIDE, CMP_STRIDE * HEAD_DIM), F32),
            k_shape, vt_shape, k_shape, vt_shape,
            jax.ShapeDtypeStruct((n, 2 * D_MODEL), F32),
            jax.ShapeDtypeStruct((b, LANES, seq), F32),
        ],
        scratch_shapes=[pltpu.VMEM((2, TM_IN, LANES), F32)],
        compiler_params=pltpu.CompilerParams(
            dimension_semantics=("arbitrary",), vmem_limit_bytes=VMEM_LIMIT),
        name="in_proj",
    )(x2, g, w, cos, slo, shi)


def _pool_mix(cur, prev, t0, wp_ref, sc_ref):
    t = t0 + lax.broadcasted_iota(jnp.int32, (cur.shape[0], POOL_GW), 0)
    out = []
    for g, w in enumerate(POOL_WINDOWS):
        sl = slice(g * POOL_GW, (g + 1) * POOL_GW)
        cg = cur[:, sl]
        s = jnp.concatenate([prev[:, sl], cg], axis=0)
        sh = 1
        while sh < w:
            s = s + pltpu.roll(s, sh, 0)
            sh *= 2
        cnt = jnp.minimum(t + 1, w).astype(F32)
        d = s[POOL_HALO:] / cnt - cg
        out.append((_dot(d.astype(BF16), wp_ref[g]) * sc_ref[:, sl]).astype(BF16))
    return jnp.concatenate(out, axis=1)


def _compress_kernel(c_ref, pe_ref, w1_ref, w2_ref, cos_ref, slo_ref, shi_ref, o_ref, ot_ref):
    half = CMP_STRIDE * HEAD_DIM
    c = c_ref[...]
    a = _dot((c + pe_ref[:, 0:half]).astype(BF16), w1_ref[0:half, :])
    b = _dot((c + pe_ref[:, half:2 * half]).astype(BF16), w1_ref[half:2 * half, :])
    n_rows = c.shape[0]
    hid = a + pltpu.roll(b, n_rows - 1, 0)
    act = jax.nn.gelu(hid, approximate=True)
    out = _dot(act.astype(BF16), w2_ref[...])
    out = _rope(out, cos_ref[...], slo_ref[...], shi_ref[...])
    row = lax.broadcasted_iota(jnp.int32, out.shape, 0)
    out = jnp.where(row < n_rows - 1, out, 0.0)
    o_ref[...] = out.astype(BF16)
    ot_ref[...] = _value_rows(out.T[0:HEAD_DIM, :]).astype(BF16)


def _compress(c4, pe, w1, w2p, cos, slo, shi):
    b, n_kv, n_chunks, half = c4.shape
    return pl.pallas_call(
        _compress_kernel,
        grid=(b, n_kv),
        in_specs=[
            pl.BlockSpec((None, None, n_chunks, half), lambda bi, j: (bi, j, 0, 0)),
            pl.BlockSpec((None, 1, 2 * half), lambda bi, j: (j // N_KV_GROUPS, 0, 0)),
            pl.BlockSpec((None, 2 * half, CMP_HIDDEN), lambda bi, j: (j // N_KV_GROUPS, 0, 0)),
            pl.BlockSpec((None, CMP_HIDDEN, LANES), lambda bi, j: (j, 0, 0)),
            pl.BlockSpec((None, n_chunks, LANES), lambda bi, j: (j // N_KV_GROUPS, 0, 0)),
            pl.BlockSpec((None, n_chunks, LANES), lambda bi, j: (j // N_KV_GROUPS, 0, 0)),
            pl.BlockSpec((None, n_chunks, LANES), lambda bi, j: (j // N_KV_GROUPS, 0, 0)),
        ],
        out_specs=[
            pl.BlockSpec((None, None, n_chunks, LANES), lambda bi, j: (bi, j, 0, 0)),
            pl.BlockSpec((None, None, V_ROWS, n_chunks), lambda bi, j: (bi, j, 0, 0)),
        ],
        out_shape=[
            jax.ShapeDtypeStruct((b, n_kv, n_chunks, LANES), BF16),
            jax.ShapeDtypeStruct((b, n_kv, V_ROWS, n_chunks), BF16),
        ],
        compiler_params=pltpu.CompilerParams(dimension_semantics=("arbitrary", "arbitrary")),
        name="compress",
    )(c4, pe, w1, w2p, cos, slo, shi)


def _block_rank(score):
    n_slc = score.shape[0]
    sub = 8
    ranks = []
    for v in range(n_slc // sub):
        blk = score[v * sub:(v + 1) * sub, :]
        jb_v = v * sub + lax.broadcasted_iota(jnp.int32, blk.shape, 0)
        r = jnp.zeros(blk.shape, F32)
        for jp in range(n_slc):
            row = score[jp:jp + 1, :]
            ge = jnp.where(row >= blk, 1.0, 0.0)
            gt = jnp.where(row > blk, 1.0, 0.0)
            if jp < v * sub:
                r = r + ge
            elif jp >= (v + 1) * sub:
                r = r + gt
            else:
                r = r + jnp.where(jb_v > jp, ge, gt)
        ranks.append(r)
    return jnp.concatenate(ranks, axis=0)


def _attn_kernel(q_ref, gn_ref, kc_ref, vct_ref, ks_ref, vst_ref, kw_ref, vwt_ref, ovt_ref,
                 pat_ref, o_ref, qsel_ref, s_ref, p_ref, al_ref, mc_ref, mw_ref, ms_ref,
                 accc_ref, accw_ref, accs_ref, *, i0):
    g = pl.program_id(1)
    i = pl.program_id(2) + i0
    s0 = i * TQ
    n_cmp = kc_ref.shape[0]
    n_slc = ovt_ref.shape[0]
    n_ch = HPG

    q_plain = lambda ch: q_ref[0, ch]
    q_selected = lambda ch: qsel_ref[ch]

    g_base = g * HEAD_DIM

    def gate_rows(c, ch):
        return gn_ref[0, pl.ds(g_base + c * HPG + ch, 1), :]

    def key_tile(ref, k0, nk):
        return ref[pl.ds(pl.multiple_of(k0, VT_CHUNK), nk), :]

    def value_tile(ref, k0, nk):
        c0 = k0 // VT_CHUNK
        return jnp.concatenate([ref[c0 + c] for c in range(nk // VT_CHUNK)], axis=1)

    imp_raw = []

    def run_step(soft=None, score=None, value=None, importance=False):
        for ch in range(n_ch):
            if soft is not None:
                nk, m_ref, first = soft
                s = s_ref[ch, 0:nk, :]
                mx = jnp.max(s, axis=0, keepdims=True)
                if first:
                    m_new, al_new = mx, None
                else:
                    m_prev = m_ref[ch:ch + 1, :]
                    m_new = jnp.maximum(m_prev, mx)
                    al_new = jnp.exp2(m_prev - m_new)
                p_new = jnp.exp2(s - m_new).astype(BF16)
            if score is not None:
                kt, q_chunk, bias = score
                sc = _dot(kt, q_chunk(ch))
                s_ref[ch, 0:kt.shape[0], :] = sc if bias is None else sc + bias()
            if value is not None:
                vt, acc_ref, vfirst = value
                pv = _dot(vt, p_ref[0:vt.shape[1], ch * TQ:(ch + 1) * TQ])
                acc_ref[ch] = pv if vfirst else al_ref[ch:ch + 1, :] * acc_ref[ch] + pv
            if importance:
                imp_raw.append(_dot(ovt_ref[...], p_ref[0:n_cmp, ch * TQ:(ch + 1) * TQ]))
            if soft is not None:
                p_ref[0:nk, ch * TQ:(ch + 1) * TQ] = p_new
                m_ref[ch:ch + 1, :] = m_new
                if al_new is not None:
                    al_ref[ch:ch + 1, :] = al_new

    n_idx = lax.broadcasted_iota(jnp.int32, (n_cmp, TQ), 0)
    t_cmp = s0 + lax.broadcasted_iota(jnp.int32, (n_cmp, TQ), 1)
    cmp_mask = jnp.where(n_idx * CMP_STRIDE + CMP_LEN - 1 <= t_cmp, 0.0, NEG_INF)
    cmp_bias = lambda: cmp_mask

    w0 = jnp.maximum(s0 - WINDOW, 0)
    win_tiles = [(off, min(NK_SEL, WIN_KEYS - off)) for off in range(0, WIN_KEYS, NK_SEL)]
    n_win = len(win_tiles)

    def win_bias(off, nk):
        d = w0 + off - s0
        pat = jnp.where(d == -WINDOW, PAT_ABOVE,
                        jnp.where(d == 0, PAT_CAUSAL, jnp.where(d < 0, PAT_ALL, PAT_NONE)))
        return lambda: pat_ref[pat]

    def win_step(k):
        args = {}
        if k < n_win:
            off, nk = win_tiles[k]
            args["score"] = (key_tile(kw_ref, w0 + off, nk), q_plain, win_bias(off, nk))
        if 1 <= k <= n_win:
            args["soft"] = (win_tiles[k - 1][1], mw_ref, k == 1)
        if 2 <= k <= n_win + 1:
            off, nk = win_tiles[k - 2]
            args["value"] = (value_tile(vwt_ref, w0 + off, nk), accw_ref, k == 2)
        return args

    run_step(score=(kc_ref[...], q_plain, cmp_bias))
    run_step(soft=(n_cmp, mc_ref, True), **win_step(0))
    run_step(value=(vct_ref[...], accc_ref, True), importance=True, **win_step(1))
    for k in range(2, n_win):
        run_step(**win_step(k))

    cmp_scale = [jnp.where(mc_ref[ch:ch + 1, :] > 0.5 * NEG_INF,
                           1.0 / accc_ref[ch, HEAD_DIM:HEAD_DIM + 1, :], 0.0)
                 for ch in range(n_ch)]

    imp = functools.reduce(lambda a, b: a + b, [r * sc for r, sc in zip(imp_raw, cmp_scale)])
    jb = lax.broadcasted_iota(jnp.int32, (n_slc, TQ), 0)
    tq = s0 + lax.broadcasted_iota(jnp.int32, (n_slc, TQ), 1)
    causal = jb * SEL_BLOCK <= tq
    near = jnp.logical_or(jb == 0, jb >= tq // SEL_BLOCK - 1)
    score = jnp.where(causal, jnp.where(near, SEL_BONUS, imp), NEG_INF)
    rank = jnp.concatenate(
        [_block_rank(score[:, c0:c0 + LANES]) for c0 in range(0, TQ, LANES)], axis=1)
    sel_bias = jnp.where(causal, jnp.where(rank < float(N_SEL), 0.0, NEG_INF), NEG_INF)
    parts = [jnp.zeros((HEAD_DIM, TQ), F32), sel_bias]
    if n_slc < HEAD_DIM:
        parts.append(jnp.zeros((HEAD_DIM - n_slc, TQ), F32))
    sel_rows = jnp.concatenate(parts, axis=0).astype(BF16)
    for hh in range(HPG):
        qsel_ref[hh] = q_ref[0, hh] + sel_rows

    ms_ref[...] = jnp.full(ms_ref.shape, NEG_INF, F32)
    accs_ref[...] = jnp.zeros(accs_ref.shape, F32)
    n_full = s0 // NK_SEL

    def sel_tile(m):
        return jnp.where(m == 0, n_full, m - 1)

    def sel_scores(m, bias=None):
        return (key_tile(ks_ref, sel_tile(m) * NK_SEL, NK_SEL), q_selected, bias)

    def sel_values(m, live):
        vt = value_tile(vst_ref, sel_tile(jnp.maximum(m, 0)) * NK_SEL, NK_SEL)
        return jnp.where(live, vt, jnp.zeros_like(vt))

    run_step(score=sel_scores(0, lambda: pat_ref[PAT_CAUSAL]), **win_step(n_win))
    run_step(**win_step(n_win + 1))

    def sel_body(m, carry):
        run_step(soft=(NK_SEL, ms_ref, False), score=sel_scores(m),
                 value=(sel_values(m - 2, m >= 2), accs_ref, False))
        return carry

    lax.fori_loop(1, n_full + 1, sel_body, 0)
    run_step(soft=(NK_SEL, ms_ref, False),
             value=(sel_values(n_full - 1, n_full >= 1), accs_ref, False))
    run_step(value=(sel_values(n_full, True), accs_ref, False))

    heads = []
    for ch in range(n_ch):
        coef_c = gate_rows(0, ch) * cmp_scale[ch]
        coef_w = gate_rows(2, ch) / accw_ref[ch, HEAD_DIM:HEAD_DIM + 1, :]
        coef_s = gate_rows(1, ch) / accs_ref[ch, HEAD_DIM:HEAD_DIM + 1, :]
        heads.append(accc_ref[ch, 0:HEAD_DIM, :] * coef_c + accw_ref[ch, 0:HEAD_DIM, :] * coef_w
                     + accs_ref[ch, 0:HEAD_DIM, :] * coef_s)
    for pair in range(HPG // 2):
        both = jnp.concatenate(heads[2 * pair:2 * pair + 2], axis=0)
        o_ref[0, :, pair * LANES:(pair + 1) * LANES] = both.T.astype(BF16)


def _attention(q3, gn3, kcmp, vcmp_t, ks4, vst, kw4, vwt, ovt, pat, i0, n_i):
    b, _, _, seq = q3.shape
    last = (i0 + n_i) * TQ - 1
    n_cmp = min(kcmp.shape[2], -(-((last - CMP_LEN + 1) // CMP_STRIDE + 1) // LANES) * LANES)
    n_slc = min(seq // SEL_BLOCK, -(-(last // SEL_BLOCK + 1) // 16) * 16)
    chunks_per_seq = seq // VT_CHUNK
    per_group = lambda bi, g, i: (bi, g, 0, 0)
    return pl.pallas_call(
        functools.partial(_attn_kernel, i0=i0),
        grid=(b, N_KV_GROUPS, n_i),
        in_specs=[
            pl.BlockSpec((1, HPG, LANES, TQ), lambda bi, g, i: (bi, g, 0, i + i0)),
            pl.BlockSpec((1, LANES, TQ), lambda bi, g, i: (bi, 0, i + i0)),
            pl.BlockSpec((None, None, n_cmp, LANES), per_group),
            pl.BlockSpec((None, None, V_ROWS, n_cmp),
                         lambda bi, g, i: (bi, N_KV_GROUPS + g, 0, 0)),
            pl.BlockSpec((None, None, seq, LANES), per_group),
            pl.BlockSpec((None, None, chunks_per_seq, V_ROWS, VT_CHUNK),
                         lambda bi, g, i: (bi, g, 0, 0, 0)),
            pl.BlockSpec((None, None, seq, LANES), per_group),
            pl.BlockSpec((None, None, chunks_per_seq, V_ROWS, VT_CHUNK),
                         lambda bi, g, i: (bi, g, 0, 0, 0)),
            pl.BlockSpec((n_slc, n_cmp), lambda bi, g, i: (0, 0)),
            _resident(pat.shape, lambda bi, g, i: (0, 0, 0)),
        ],
        out_specs=pl.BlockSpec((1, TQ, HPG * HEAD_DIM), lambda bi, g, i: (bi, i, g)),
        out_shape=jax.ShapeDtypeStruct((b, n_i * TQ, NSA_WIDTH), BF16),
        scratch_shapes=[
            pltpu.VMEM((HPG, LANES, TQ), BF16),
            pltpu.VMEM((HPG, NK_SEL, TQ), F32),
            pltpu.VMEM((NK_SEL, HPG * TQ), BF16),
            pltpu.VMEM((HPG, TQ), F32),
            pltpu.VMEM((HPG, TQ), F32),
            pltpu.VMEM((HPG, TQ), F32),
            pltpu.VMEM((HPG, TQ), F32),
            pltpu.VMEM((HPG, V_ROWS, TQ), F32),
            pltpu.VMEM((HPG, V_ROWS, TQ), F32),
            pltpu.VMEM((HPG, V_ROWS, TQ), F32),
        ],
        compiler_params=pltpu.CompilerParams(
            dimension_semantics=("arbitrary", "arbitrary", "arbitrary"),
            vmem_limit_bytes=VMEM_LIMIT),
        name="nsa_attention",
    )(q3, gn3, kcmp, vcmp_t, ks4, vst, kw4, vwt, ovt, pat)


def _mlp_kernel(x_ref, u_ref, up_ref, ylo_ref, yhi_ref, gm_ref, wp_ref, sc_ref, wpp_ref, wpn_ref, wo_ref,
                nm_ref, w1_ref, w2_ref, nf_ref, o_ref, *, final, tiles_per_seq):
    st = pl.program_id(0) % tiles_per_seq
    prev = jnp.where(st > 0, up_ref[...], 0.0)
    y_pool = _pool_mix(u_ref[...], prev, st * TM_MLP, wp_ref, sc_ref)
    p1 = _dot(y_pool, wpp_ref[...])
    y_nsa = jnp.where(2 * st < tiles_per_seq, ylo_ref[...], yhi_ref[...])
    p2 = _dot(y_nsa, wpn_ref[...])
    ga = jax.nn.sigmoid(gm_ref[:, 0:D_MODEL])
    gb = jax.nn.sigmoid(gm_ref[:, D_MODEL:2 * D_MODEL])
    merged = ga * p1 + gb * p2
    x = x_ref[...] + _dot(merged.astype(BF16), wo_ref[...])
    h = _rms(x, nm_ref[...]).astype(BF16)
    acc = jnp.zeros((TM_MLP, D_MODEL), F32)
    for c in range(D_FF // FF_CHUNK):
        sl = slice(c * FF_CHUNK, (c + 1) * FF_CHUNK)
        a = jnp.square(jnp.maximum(_dot(h, w1_ref[:, sl]), 0.0)).astype(BF16)
        acc = acc + _dot(a, w2_ref[sl, :])
    x = x + acc
    if final:
        x = _rms(x, nf_ref[...])
    o_ref[...] = x


def _merge_mlp(x2, u, y_lo, y_hi, gm, wp, sc, wpp, wpn, wo, nm, w1, w2, nf, layer, final, seq):
    n = x2.shape[0]
    halo_per_tile = TM_MLP // POOL_HALO
    tps = seq // TM_MLP
    half = tps // 2
    row = lambda w_: pl.BlockSpec((TM_MLP, w_), lambda i: (i, 0))
    lo = pl.BlockSpec((TM_MLP, NSA_WIDTH),
                      lambda i: ((i // tps) * half + jnp.minimum(i % tps, half - 1), 0))
    hi = pl.BlockSpec((TM_MLP, NSA_WIDTH),
                      lambda i: ((i // tps) * half + jnp.maximum(i % tps - half, 0), 0))
    halo = pl.BlockSpec((POOL_HALO, POOL_WIDTH),
                        lambda i: (jnp.maximum(i * halo_per_tile - 1, 0), 0))
    res = lambda a: _resident(a.shape, lambda i: (0,) * a.ndim)
    lay = lambda a: _layer_resident(a, layer)
    return pl.pallas_call(
        functools.partial(_mlp_kernel, final=final, tiles_per_seq=seq // TM_MLP),
        grid=(n // TM_MLP,),
        in_specs=[row(D_MODEL), row(POOL_WIDTH), halo, lo, hi, row(2 * D_MODEL),
                  lay(wp), res(sc), lay(wpp), lay(wpn), lay(wo), res(nm), lay(w1), lay(w2),
                  res(nf)],
        out_specs=row(D_MODEL),
        out_shape=jax.ShapeDtypeStruct((n, D_MODEL), F32),
        compiler_params=pltpu.CompilerParams(
            dimension_semantics=("arbitrary",), vmem_limit_bytes=VMEM_LIMIT),
        name="merge_mlp",
    )(x2, u, u, y_lo, y_hi, gm, wp, sc, wpp, wpn, wo, nm, w1, w2, nf)


def _rope_tables(pos):
    inv = ROPE_THETA ** (-jnp.arange(0, HEAD_DIM, 2, dtype=F32) / HEAD_DIM)
    ang = pos.astype(F32)[:, None] * inv[None, :]
    ang = jnp.concatenate([ang, ang, ang, ang], axis=-1)
    first_half = (jnp.arange(LANES) % HEAD_DIM) < HEAD_DIM // 2
    cos, sin = jnp.cos(ang), jnp.sin(ang)
    return cos, jnp.where(first_half, -sin, 0.0), jnp.where(first_half, 0.0, sin)


def _permute_w_in(w_in):
    o_q = POOL_WIDTH
    o_kv = o_q + NSA_WIDTH
    o_gn = o_kv + 6 * KV_WIDTH
    o_gm = o_gn + N_GATE
    depth = w_in.shape[0]
    gn = w_in[:, :, o_gn:o_gm].reshape(depth, D_MODEL, N_KV_GROUPS, HPG, 3)
    gn = gn.transpose(0, 1, 2, 4, 3).reshape(depth, D_MODEL, N_KV_GROUPS, 3 * HPG)
    gn = jnp.pad(gn, ((0, 0), (0, 0), (0, 0), (0, LANES // N_KV_GROUPS - 3 * HPG)))
    gn = gn.reshape(depth, D_MODEL, LANES)
    return jnp.concatenate(
        [w_in[:, :, 0:o_gn], w_in[:, :, o_gm:], gn], axis=-1).astype(BF16)


def kernel(x, norm_mix, w_in, w_pool, pool_scale, pe_k, pe_v, w_ck1, w_ck2, w_cv1, w_cv2,
           w_proj_pool, w_proj_nsa, w_out, norm_mlp, w_ff1, w_ff2, norm_final):
    b, seq, d = x.shape
    depth = w_in.shape[0]
    n = b * seq
    n_chunks = seq // CMP_STRIDE
    n_slc = seq // SEL_BLOCK
    assert n_slc <= HEAD_DIM, "the selection one-hot shares the 64 spare key lanes"
    assert seq >= WIN_KEYS and seq % TM_IN == 0

    w_in_p = _permute_w_in(w_in)
    w_pool_b = w_pool.astype(BF16)
    pe = jnp.stack([pe_k, pe_v], axis=1).reshape(depth, 2, 1, CMP_LEN * HEAD_DIM)
    w_c1 = jnp.stack([w_ck1, w_cv1], axis=1).astype(BF16)
    pad = jnp.zeros_like(w_ck2)
    w_c2 = jnp.stack([jnp.concatenate([w, pad], axis=-1) for w in (w_ck2, w_ck2, w_cv2, w_cv2)],
                     axis=1).astype(BF16)
    wpp, wpn, wo = w_proj_pool.astype(BF16), w_proj_nsa.astype(BF16), w_out.astype(BF16)
    w1, w2 = w_ff1.astype(BF16), w_ff2.astype(BF16)

    cos, slo, shi = _rope_tables(jnp.arange(seq))
    ccos, cslo, cshi = _rope_tables(jnp.arange(n_chunks) * CMP_STRIDE + CMP_LEN - 1)
    ident = (jnp.ones_like(ccos), jnp.zeros_like(cslo), jnp.zeros_like(cshi))
    cmp_tabs = [jnp.stack([t, i_], axis=0) for t, i_ in zip((ccos, cslo, cshi), ident)]
    cmp_start = jnp.arange(n_chunks) * CMP_STRIDE
    slc_start = jnp.arange(n_slc) * SEL_BLOCK
    ovt = ((cmp_start[None, :] <= slc_start[:, None] + SEL_BLOCK - 1)
           & (cmp_start[None, :] + CMP_LEN - 1 >= slc_start[:, None])).astype(BF16)
    k_loc = jnp.arange(NK_SEL)[:, None]
    t_loc = jnp.arange(TQ)[None, :]
    keep = jnp.stack([jnp.ones((NK_SEL, TQ), bool), k_loc <= t_loc, k_loc > t_loc,
                      jnp.zeros((NK_SEL, TQ), bool)])
    pat = jnp.where(keep, 0.0, NEG_INF).astype(F32)

    x2 = x.reshape(n, d)
    for l in range(depth):
        u, q, c4, ks, vst, kw, vwt, gm, gn = _inproj(
            x2, norm_mix[l][None, :], w_in_p, l, cos, slo, shi, b, seq)
        cmp_n, cmp_t = _compress(c4, pe[l], w_c1[l], w_c2[l], *cmp_tabs)
        n_i = seq // TQ // 2
        y_lo, y_hi = [
            _attention(q, gn, cmp_n, cmp_t, ks, vst, kw, vwt, ovt, pat, i0, n_i).reshape(
                n // 2, NSA_WIDTH) for i0 in (0, n_i)]
        x2 = _merge_mlp(
            x2, u, y_lo, y_hi, gm, w_pool_b, pool_scale[l][None, :],
            wpp, wpn, wo, norm_mlp[l][None, :], w1, w2, norm_final[None, :],
            layer=l, final=(l == depth - 1), seq=seq)
    return x2.reshape(b, seq, d)
```

```python
import functools

import jax
import jax.numpy as jnp
import numpy as np
from jax import lax
from jax.experimental import pallas as pl
from jax.experimental.pallas import tpu as pltpu

F32 = jnp.float32
BF16 = jnp.bfloat16

D_MODEL = 1024
POOL_WINDOWS = (2, 4, 8, 16)
POOL_WIDTH = 512
POOL_GW = 128
N_HEADS = 16
HEAD_DIM = 64
N_KV_GROUPS = 2
HPG = 8
NSA_WIDTH = 1024
KV_WIDTH = 128
CMP_LEN = 32
CMP_STRIDE = 16
CMP_HIDDEN = 256
SEL_BLOCK = 64
N_SEL = 16
WINDOW = 512
SEL_BONUS = 1e4
NEG_INF = -1e30
ROPE_THETA = 10000.0
D_FF = 4096
RMS_EPS = 1e-6
N_GATE = 3 * N_HEADS
Q_SCALE = HEAD_DIM ** -0.5 * float(np.log2(np.e))

LANES = 128
VMEM_LIMIT = 56 * 1024 * 1024

C_U = 0
C_Q = C_U + POOL_WIDTH
C_KV = C_Q + NSA_WIDTH
C_GM = C_KV + 6 * KV_WIDTH
C_GN = C_GM + 2 * D_MODEL
N_INP = C_GN + LANES

TM_IN = 512
POOL_HALO = 16
TM_MLP = 512
FF_CHUNK = 1024
TQ = 256
NK_SEL = 256
WIN_KEYS = WINDOW + TQ
VT_CHUNK = 128
V_ROWS = HEAD_DIM + 16
PAT_ALL, PAT_CAUSAL, PAT_ABOVE, PAT_NONE = 0, 1, 2, 3
assert NK_SEL == TQ and WINDOW == 2 * NK_SEL and WIN_KEYS % NK_SEL == 0


def _dot(a, b):
    return jnp.dot(a, b, preferred_element_type=F32)


def _rms(x, g):
    return x * lax.rsqrt(jnp.mean(x * x, axis=-1, keepdims=True) + RMS_EPS) * g


def _rope(t, cos, sin_lo, sin_hi):
    return t * cos + pltpu.roll(t, LANES - 32, 1) * sin_lo + pltpu.roll(t, 32, 1) * sin_hi


def _resident(shape, index_map):
    return pl.BlockSpec(shape, index_map, pipeline_mode=pl.Buffered(1))


def _layer_resident(stacked, layer):
    nd = stacked.ndim - 1
    return _resident((None,) + stacked.shape[1:], lambda i: (layer,) + (0,) * nd)


def _value_rows(v_t):
    tail_row = lax.broadcasted_iota(jnp.int32, (V_ROWS - HEAD_DIM, v_t.shape[1]), 0)
    return jnp.concatenate([v_t, jnp.where(tail_row == 0, 1.0, 0.0)], axis=0)


def _inproj_kernel(x_ref, g_ref, w_ref, cos_ref, slo_ref, shi_ref,
                   u_ref, q_ref, cmp_ref, ks_ref, vst_ref, kw_ref, vwt_ref, gm_ref, gn_ref,
                   kv_scr, *, tiles_per_seq):
    h = _rms(x_ref[...], g_ref[...]).astype(BF16)
    cos, slo, shi = cos_ref[...], slo_ref[...], shi_ref[...]
    lane = lax.broadcasted_iota(jnp.int32, (TM_IN, LANES), 1)
    low = lane < HEAD_DIM
    pos = (pl.program_id(0) % tiles_per_seq) * TM_IN + lax.broadcasted_iota(
        jnp.int32, (TM_IN, LANES), 0)
    block_onehot = jnp.where(lane - HEAD_DIM == pos // SEL_BLOCK, 1.0, 0.0)

    u_ref[...] = _dot(h, w_ref[:, C_U:C_Q])
    q = _dot(h, w_ref[:, C_Q:C_KV])
    for k in range(NSA_WIDTH // LANES):
        qt = (_rope(q[:, k * LANES:(k + 1) * LANES], cos, slo, shi) * Q_SCALE).T
        spare = jnp.zeros((LANES - HEAD_DIM, TM_IN), F32)
        for par in range(2):
            q_ref[0, 2 * k + par] = jnp.concatenate(
                [qt[par * HEAD_DIM:(par + 1) * HEAD_DIM, :], spare], axis=0).astype(BF16)
    kv = _dot(h, w_ref[:, C_KV:C_GM])
    n_rows = TM_IN // CMP_STRIDE
    low_c = lax.broadcasted_iota(jnp.int32, (n_rows, LANES), 1) < HEAD_DIM
    for t in range(2):
        kv_scr[t] = kv[:, t * LANES:(t + 1) * LANES]
        for pp in range(CMP_STRIDE // 2):
            a = kv_scr[t, pl.ds(2 * pp, n_rows, stride=CMP_STRIDE), :]
            b = kv_scr[t, pl.ds(2 * pp + 1, n_rows, stride=CMP_STRIDE), :]
            sl = slice(pp * LANES, (pp + 1) * LANES)
            cmp_ref[0, 2 * t, :, sl] = jnp.where(low_c, a, pltpu.roll(b, HEAD_DIM, 1))
            cmp_ref[0, 2 * t + 1, :, sl] = jnp.where(low_c, pltpu.roll(a, HEAD_DIM, 1), b)
    ks = _rope(kv[:, 2 * KV_WIDTH:3 * KV_WIDTH], cos, slo, shi)
    kw = _rope(kv[:, 4 * KV_WIDTH:5 * KV_WIDTH], cos, slo, shi)
    vs_t = kv[:, 3 * KV_WIDTH:4 * KV_WIDTH].T
    vw_t = kv[:, 5 * KV_WIDTH:6 * KV_WIDTH].T
    for g in range(N_KV_GROUPS):
        ks_g = ks if g == 0 else pltpu.roll(ks, HEAD_DIM, 1)
        kw_g = kw if g == 0 else pltpu.roll(kw, HEAD_DIM, 1)
        ks_ref[0, g] = jnp.where(low, ks_g, block_onehot).astype(BF16)
        kw_ref[0, g] = jnp.where(low, kw_g, 0.0).astype(BF16)
        vs_g = _value_rows(vs_t[g * HEAD_DIM:(g + 1) * HEAD_DIM, :]).astype(BF16)
        vw_g = _value_rows(vw_t[g * HEAD_DIM:(g + 1) * HEAD_DIM, :]).astype(BF16)
        for c in range(TM_IN // VT_CHUNK):
            sl = slice(c * VT_CHUNK, (c + 1) * VT_CHUNK)
            vst_ref[0, g, c] = vs_g[:, sl]
            vwt_ref[0, g, c] = vw_g[:, sl]
    gm_ref[...] = _dot(h, w_ref[:, C_GM:C_GN])
    gn_ref[0] = jax.nn.sigmoid(_dot(h, w_ref[:, C_GN:N_INP])).T


def _inproj(x2, g, w, layer, cos, slo, shi, b, seq):
    n = x2.shape[0]
    tiles_per_seq = seq // TM_IN
    n_chunks = TM_IN // VT_CHUNK
    row = lambda w_: pl.BlockSpec((TM_IN, w_), lambda i: (i, 0))
    tab = pl.BlockSpec((TM_IN, LANES), lambda i: (i % tiles_per_seq, 0))
    kg = pl.BlockSpec((1, N_KV_GROUPS, TM_IN, LANES),
                      lambda i: (i // tiles_per_seq, 0, i % tiles_per_seq, 0))
    vt = pl.BlockSpec((1, N_KV_GROUPS, n_chunks, V_ROWS, VT_CHUNK),
                      lambda i: (i // tiles_per_seq, 0, i % tiles_per_seq, 0, 0))
    cmp_spec = pl.BlockSpec(
        (1, 2 * N_KV_GROUPS, TM_IN // CMP_STRIDE, CMP_STRIDE * HEAD_DIM),
        lambda i: (i // tiles_per_seq, 0, i % tiles_per_seq, 0))
    qt_spec = pl.BlockSpec((1, N_HEADS, LANES, TM_IN),
                           lambda i: (i // tiles_per_seq, 0, 0, i % tiles_per_seq))
    gt_spec = pl.BlockSpec((1, LANES, TM_IN), lambda i: (i // tiles_per_seq, 0, i % tiles_per_seq))
    k_shape = jax.ShapeDtypeStruct((b, N_KV_GROUPS, seq, LANES), BF16)
    vt_shape = jax.ShapeDtypeStruct((b, N_KV_GROUPS, seq // VT_CHUNK, V_ROWS, VT_CHUNK), BF16)
    return pl.pallas_call(
        functools.partial(_inproj_kernel, tiles_per_seq=tiles_per_seq),
        grid=(n // TM_IN,),
        in_specs=[row(D_MODEL), _resident((1, D_MODEL), lambda i: (0, 0)),
                  _layer_resident(w, layer), tab, tab, tab],
        out_specs=[row(POOL_WIDTH), qt_spec, cmp_spec, kg, vt,
                   kg, vt, row(2 * D_MODEL), gt_spec],
        out_shape=[
            jax.ShapeDtypeStruct((n, POOL_WIDTH), F32),
            jax.ShapeDtypeStruct((b, N_HEADS, LANES, seq), BF16),
            jax.ShapeDtypeStruct(
                (b, 2 * N_KV_GROUPS, seq // CMP_STRIDE, CMP_STRIDE * HEAD_DIM), F32),
            k_shape, vt_shape, k_shape, vt_shape,
            jax.ShapeDtypeStruct((n, 2 * D_MODEL), F32),
            jax.ShapeDtypeStruct((b, LANES, seq), F32),
        ],
        scratch_shapes=[pltpu.VMEM((2, TM_IN, LANES), F32)],
        compiler_params=pltpu.CompilerParams(
            dimension_semantics=("arbitrary",), vmem_limit_bytes=VMEM_LIMIT),
        name="in_proj",
    )(x2, g, w, cos, slo, shi)


def _pool_mix(cur, prev, t0, wp_ref, sc_ref):
    t = t0 + lax.broadcasted_iota(jnp.int32, (cur.shape[0], POOL_GW), 0)
    out = []
    for g, w in enumerate(POOL_WINDOWS):
        sl = slice(g * POOL_GW, (g + 1) * POOL_GW)
        cg = cur[:, sl]
        s = jnp.concatenate([prev[:, sl], cg], axis=0)
        sh = 1
        while sh < w:
            s = s + pltpu.roll(s, sh, 0)
            sh *= 2
        cnt = jnp.minimum(t + 1, w).astype(F32)
        d = s[POOL_HALO:] / cnt - cg
        out.append((_dot(d.astype(BF16), wp_ref[g]) * sc_ref[:, sl]).astype(BF16))
    return jnp.concatenate(out, axis=1)


def _compress_kernel(c_ref, pe_ref, w1_ref, w2_ref, cos_ref, slo_ref, shi_ref, o_ref, ot_ref):
    half = CMP_STRIDE * HEAD_DIM
    c = c_ref[...]
    a = _dot((c + pe_ref[:, 0:half]).astype(BF16), w1_ref[0:half, :])
    b = _dot((c + pe_ref[:, half:2 * half]).astype(BF16), w1_ref[half:2 * half, :])
    n_rows = c.shape[0]
    hid = a + pltpu.roll(b, n_rows - 1, 0)
    act = jax.nn.gelu(hid, approximate=True)
    out = _dot(act.astype(BF16), w2_ref[...])
    out = _rope(out, cos_ref[...], slo_ref[...], shi_ref[...])
    row = lax.broadcasted_iota(jnp.int32, out.shape, 0)
    out = jnp.where(row < n_rows - 1, out, 0.0)
    o_ref[...] = out.astype(BF16)
    ot_ref[...] = _value_rows(out.T[0:HEAD_DIM, :]).astype(BF16)


def _compress(c4, pe, w1, w2p, cos, slo, shi):
    b, n_kv, n_chunks, half = c4.shape
    return pl.pallas_call(
        _compress_kernel,
        grid=(b, n_kv),
        in_specs=[
            pl.BlockSpec((None, None, n_chunks, half), lambda bi, j: (bi, j, 0, 0)),
            pl.BlockSpec((None, 1, 2 * half), lambda bi, j: (j // N_KV_GROUPS, 0, 0)),
            pl.BlockSpec((None, 2 * half, CMP_HIDDEN), lambda bi, j: (j // N_KV_GROUPS, 0, 0)),
            pl.BlockSpec((None, CMP_HIDDEN, LANES), lambda bi, j: (j, 0, 0)),
            pl.BlockSpec((None, n_chunks, LANES), lambda bi, j: (j // N_KV_GROUPS, 0, 0)),
            pl.BlockSpec((None, n_chunks, LANES), lambda bi, j: (j // N_KV_GROUPS, 0, 0)),
            pl.BlockSpec((None, n_chunks, LANES), lambda bi, j: (j // N_KV_GROUPS, 0, 0)),
        ],
        out_specs=[
            pl.BlockSpec((None, None, n_chunks, LANES), lambda bi, j: (bi, j, 0, 0)),
            pl.BlockSpec((None, None, V_ROWS, n_chunks), lambda bi, j: (bi, j, 0, 0)),
        ],
        out_shape=[
            jax.ShapeDtypeStruct((b, n_kv, n_chunks, LANES), BF16),
            jax.ShapeDtypeStruct((b, n_kv, V_ROWS, n_chunks), BF16),
        ],
        compiler_params=pltpu.CompilerParams(dimension_semantics=("arbitrary", "arbitrary")),
        name="compress",
    )(c4, pe, w1, w2p, cos, slo, shi)


def _block_rank(score):
    n_slc = score.shape[0]
    sub = 8
    ranks = []
    for v in range(n_slc // sub):
        blk = score[v * sub:(v + 1) * sub, :]
        jb_v = v * sub + lax.broadcasted_iota(jnp.int32, blk.shape, 0)
        r = jnp.zeros(blk.shape, F32)
        for jp in range(n_slc):
            row = score[jp:jp + 1, :]
            ge = jnp.where(row >= blk, 1.0, 0.0)
            gt = jnp.where(row > blk, 1.0, 0.0)
            if jp < v * sub:
                r = r + ge
            elif jp >= (v + 1) * sub:
                r = r + gt
            else:
                r = r + jnp.where(jb_v > jp, ge, gt)
        ranks.append(r)
    return jnp.concatenate(ranks, axis=0)


def _attn_kernel(q_ref, gn_ref, kc_ref, vct_ref, ks_ref, vst_ref, kw_ref, vwt_ref, ovt_ref,
                 pat_ref, o_ref, qsel_ref, s_ref, p_ref, al_ref, mc_ref, mw_ref, ms_ref,
                 accc_ref, accw_ref, accs_ref, *, i0):
    i = pl.program_id(1) + i0
    s0 = i * TQ
    n_cmp = kc_ref.shape[1]
    n_slc = ovt_ref.shape[0]
    n_ch = N_HEADS
    groups = range(N_KV_GROUPS)
    group_of = lambda ch: ch // HPG

    q_plain = lambda ch: q_ref[0, ch]
    q_selected = lambda ch: qsel_ref[ch]

    def gate_rows(c, ch):
        row = group_of(ch) * HEAD_DIM + c * HPG + ch % HPG
        return gn_ref[0, row:row + 1, :]

    def key_tile(ref, k0, nk):
        k0 = pl.multiple_of(k0, VT_CHUNK)
        return [ref[g, pl.ds(k0, nk), :] for g in groups]

    def value_tile(ref, k0, nk):
        c0 = k0 // VT_CHUNK
        return [jnp.concatenate([ref[g, c0 + c] for c in range(nk // VT_CHUNK)], axis=1)
                for g in groups]

    imp_raw = []

    def run_step(soft=None, score=None, value=None, importance=False):
        for ch in range(n_ch):
            if soft is not None:
                nk, m_ref, first = soft
                s = s_ref[ch, 0:nk, :]
                mx = jnp.max(s, axis=0, keepdims=True)
                if first:
                    m_new, al_new = mx, None
                else:
                    m_prev = m_ref[ch:ch + 1, :]
                    m_new = jnp.maximum(m_prev, mx)
                    al_new = jnp.exp2(m_prev - m_new)
                p_new = jnp.exp2(s - m_new).astype(BF16)
            if score is not None:
                kts, q_chunk, bias = score
                kt = kts[group_of(ch)]
                sc = _dot(kt, q_chunk(ch))
                s_ref[ch, 0:kt.shape[0], :] = sc if bias is None else sc + bias()
            if value is not None:
                vts, acc_ref, vfirst = value
                vt = vts[group_of(ch)]
                pv = _dot(vt, p_ref[0:vt.shape[1], ch * TQ:(ch + 1) * TQ])
                acc_ref[ch] = pv if vfirst else al_ref[ch:ch + 1, :] * acc_ref[ch] + pv
            if importance:
                imp_raw.append(_dot(ovt_ref[...], p_ref[0:n_cmp, ch * TQ:(ch + 1) * TQ]))
            if soft is not None:
                p_ref[0:nk, ch * TQ:(ch + 1) * TQ] = p_new
                m_ref[ch:ch + 1, :] = m_new
                if al_new is not None:
                    al_ref[ch:ch + 1, :] = al_new

    n_idx = lax.broadcasted_iota(jnp.int32, (n_cmp, TQ), 0)
    t_cmp = s0 + lax.broadcasted_iota(jnp.int32, (n_cmp, TQ), 1)
    cmp_mask = jnp.where(n_idx * CMP_STRIDE + CMP_LEN - 1 <= t_cmp, 0.0, NEG_INF)
    cmp_bias = lambda: cmp_mask

    w0 = jnp.maximum(s0 - WINDOW, 0)
    win_tiles = [(off, min(NK_SEL, WIN_KEYS - off)) for off in range(0, WIN_KEYS, NK_SEL)]
    n_win = len(win_tiles)

    def win_bias(off, nk):
        d = w0 + off - s0
        pat = jnp.where(d == -WINDOW, PAT_ABOVE,
                        jnp.where(d == 0, PAT_CAUSAL, jnp.where(d < 0, PAT_ALL, PAT_NONE)))
        return lambda: pat_ref[pat]

    def win_step(k):
        args = {}
        if k < n_win:
            off, nk = win_tiles[k]
            args["score"] = (key_tile(kw_ref, w0 + off, nk), q_plain, win_bias(off, nk))
        if 1 <= k <= n_win:
            args["soft"] = (win_tiles[k - 1][1], mw_ref, k == 1)
        if 2 <= k <= n_win + 1:
            off, nk = win_tiles[k - 2]
            args["value"] = (value_tile(vwt_ref, w0 + off, nk), accw_ref, k == 2)
        return args

    run_step(score=([kc_ref[g] for g in groups], q_plain, cmp_bias))
    run_step(soft=(n_cmp, mc_ref, True), **win_step(0))
    run_step(value=([vct_ref[N_KV_GROUPS + g] for g in groups], accc_ref, True), importance=True,
             **win_step(1))
    for k in range(2, n_win):
        run_step(**win_step(k))

    cmp_scale = [jnp.where(mc_ref[ch:ch + 1, :] > 0.5 * NEG_INF,
                           1.0 / accc_ref[ch, HEAD_DIM:HEAD_DIM + 1, :], 0.0)
                 for ch in range(n_ch)]

    jb = lax.broadcasted_iota(jnp.int32, (n_slc, TQ), 0)
    tq = s0 + lax.broadcasted_iota(jnp.int32, (n_slc, TQ), 1)
    causal = jb * SEL_BLOCK <= tq
    near = jnp.logical_or(jb == 0, jb >= tq // SEL_BLOCK - 1)
    for g in groups:
        heads_g = range(g * HPG, (g + 1) * HPG)
        imp = functools.reduce(lambda a, b: a + b, [imp_raw[ch] * cmp_scale[ch] for ch in heads_g])
        score = jnp.where(causal, jnp.where(near, SEL_BONUS, imp), NEG_INF)
        rank = jnp.concatenate(
            [_block_rank(score[:, c0:c0 + LANES]) for c0 in range(0, TQ, LANES)], axis=1)
        sel_bias = jnp.where(causal, jnp.where(rank < float(N_SEL), 0.0, NEG_INF), NEG_INF)
        parts = [jnp.zeros((HEAD_DIM, TQ), F32), sel_bias]
        if n_slc < HEAD_DIM:
            parts.append(jnp.zeros((HEAD_DIM - n_slc, TQ), F32))
        sel_rows = jnp.concatenate(parts, axis=0).astype(BF16)
        for hh in heads_g:
            qsel_ref[hh] = q_ref[0, hh] + sel_rows

    ms_ref[...] = jnp.full(ms_ref.shape, NEG_INF, F32)
    accs_ref[...] = jnp.zeros(accs_ref.shape, F32)
    n_full = s0 // NK_SEL

    def sel_tile(m):
        return jnp.where(m == 0, n_full, m - 1)

    def sel_scores(m, bias=None):
        return (key_tile(ks_ref, sel_tile(m) * NK_SEL, NK_SEL), q_selected, bias)

    def sel_values(m, live):
        vt = value_tile(vst_ref, sel_tile(jnp.maximum(m, 0)) * NK_SEL, NK_SEL)
        return [jnp.where(live, v, jnp.zeros_like(v)) for v in vt]

    run_step(score=sel_scores(0, lambda: pat_ref[PAT_CAUSAL]), **win_step(n_win))
    run_step(**win_step(n_win + 1))

    def sel_body(m, carry):
        run_step(soft=(NK_SEL, ms_ref, False), score=sel_scores(m),
                 value=(sel_values(m - 2, m >= 2), accs_ref, False))
        return carry

    lax.fori_loop(1, n_full + 1, sel_body, 0)
    run_step(soft=(NK_SEL, ms_ref, False),
             value=(sel_values(n_full - 1, n_full >= 1), accs_ref, False))
    run_step(value=(sel_values(n_full, True), accs_ref, False))

    heads = []
    for ch in range(n_ch):
        coef_c = gate_rows(0, ch) * cmp_scale[ch]
        coef_w = gate_rows(2, ch) / accw_ref[ch, HEAD_DIM:HEAD_DIM + 1, :]
        coef_s = gate_rows(1, ch) / accs_ref[ch, HEAD_DIM:HEAD_DIM + 1, :]
        heads.append(accc_ref[ch, 0:HEAD_DIM, :] * coef_c + accw_ref[ch, 0:HEAD_DIM, :] * coef_w
                     + accs_ref[ch, 0:HEAD_DIM, :] * coef_s)
    for pair in range(N_HEADS // 2):
        both = jnp.concatenate(heads[2 * pair:2 * pair + 2], axis=0)
        o_ref[0, :, pair * LANES:(pair + 1) * LANES] = both.T.astype(BF16)


def _attention(q3, gn3, kcmp, vcmp_t, ks4, vst, kw4, vwt, ovt, pat, i0, n_i):
    b, _, _, seq = q3.shape
    last = (i0 + n_i) * TQ - 1
    n_cmp = min(kcmp.shape[2], -(-((last - CMP_LEN + 1) // CMP_STRIDE + 1) // LANES) * LANES)
    n_slc = min(seq // SEL_BLOCK, -(-(last // SEL_BLOCK + 1) // 16) * 16)
    chunks_per_seq = seq // VT_CHUNK
    whole = lambda bi, i: (bi, 0, 0, 0)
    return pl.pallas_call(
        functools.partial(_attn_kernel, i0=i0),
        grid=(b, n_i),
        in_specs=[
            pl.BlockSpec((1, N_HEADS, LANES, TQ), lambda bi, i: (bi, 0, 0, i + i0)),
            pl.BlockSpec((1, LANES, TQ), lambda bi, i: (bi, 0, i + i0)),
            pl.BlockSpec((None, 2 * N_KV_GROUPS, n_cmp, LANES), whole),
            pl.BlockSpec((None, 2 * N_KV_GROUPS, V_ROWS, n_cmp), whole),
            pl.BlockSpec((None, N_KV_GROUPS, seq, LANES), whole),
            pl.BlockSpec((None, N_KV_GROUPS, chunks_per_seq, V_ROWS, VT_CHUNK),
                         lambda bi, i: (bi, 0, 0, 0, 0)),
            pl.BlockSpec((None, N_KV_GROUPS, seq, LANES), whole),
            pl.BlockSpec((None, N_KV_GROUPS, chunks_per_seq, V_ROWS, VT_CHUNK),
                         lambda bi, i: (bi, 0, 0, 0, 0)),
            pl.BlockSpec((n_slc, n_cmp), lambda bi, i: (0, 0)),
            _resident(pat.shape, lambda bi, i: (0, 0, 0)),
        ],
        out_specs=pl.BlockSpec((1, TQ, NSA_WIDTH), lambda bi, i: (bi, i, 0)),
        out_shape=jax.ShapeDtypeStruct((b, n_i * TQ, NSA_WIDTH), BF16),
        scratch_shapes=[
            pltpu.VMEM((N_HEADS, LANES, TQ), BF16),
            pltpu.VMEM((N_HEADS, NK_SEL, TQ), F32),
            pltpu.VMEM((NK_SEL, N_HEADS * TQ), BF16),
            pltpu.VMEM((N_HEADS, TQ), F32),
            pltpu.VMEM((N_HEADS, TQ), F32),
            pltpu.VMEM((N_HEADS, TQ), F32),
            pltpu.VMEM((N_HEADS, TQ), F32),
            pltpu.VMEM((N_HEADS, V_ROWS, TQ), F32),
            pltpu.VMEM((N_HEADS, V_ROWS, TQ), F32),
            pltpu.VMEM((N_HEADS, V_ROWS, TQ), F32),
        ],
        compiler_params=pltpu.CompilerParams(
            dimension_semantics=("arbitrary", "arbitrary"),
            vmem_limit_bytes=VMEM_LIMIT),
        name="nsa_attention",
    )(q3, gn3, kcmp, vcmp_t, ks4, vst, kw4, vwt, ovt, pat)


def _mlp_kernel(x_ref, u_ref, up_ref, ylo_ref, yhi_ref, gm_ref, wp_ref, sc_ref, wpp_ref, wpn_ref, wo_ref,
                nm_ref, w1_ref, w2_ref, nf_ref, o_ref, *, final, tiles_per_seq):
    st = pl.program_id(0) % tiles_per_seq
    prev = jnp.where(st > 0, up_ref[...], 0.0)
    y_pool = _pool_mix(u_ref[...], prev, st * TM_MLP, wp_ref, sc_ref)
    p1 = _dot(y_pool, wpp_ref[...])
    y_nsa = jnp.where(2 * st < tiles_per_seq, ylo_ref[...], yhi_ref[...])
    p2 = _dot(y_nsa, wpn_ref[...])
    ga = jax.nn.sigmoid(gm_ref[:, 0:D_MODEL])
    gb = jax.nn.sigmoid(gm_ref[:, D_MODEL:2 * D_MODEL])
    merged = ga * p1 + gb * p2
    x = x_ref[...] + _dot(merged.astype(BF16), wo_ref[...])
    h = _rms(x, nm_ref[...]).astype(BF16)
    acc = jnp.zeros((TM_MLP, D_MODEL), F32)
    for c in range(D_FF // FF_CHUNK):
        sl = slice(c * FF_CHUNK, (c + 1) * FF_CHUNK)
        a = jnp.square(jnp.maximum(_dot(h, w1_ref[:, sl]), 0.0)).astype(BF16)
        acc = acc + _dot(a, w2_ref[sl, :])
    x = x + acc
    if final:
        x = _rms(x, nf_ref[...])
    o_ref[...] = x


def _merge_mlp(x2, u, y_lo, y_hi, gm, wp, sc, wpp, wpn, wo, nm, w1, w2, nf, layer, final, seq):
    n = x2.shape[0]
    halo_per_tile = TM_MLP // POOL_HALO
    tps = seq // TM_MLP
    half = tps // 2
    row = lambda w_: pl.BlockSpec((TM_MLP, w_), lambda i: (i, 0))
    lo = pl.BlockSpec((TM_MLP, NSA_WIDTH),
                      lambda i: ((i // tps) * half + jnp.minimum(i % tps, half - 1), 0))
    hi = pl.BlockSpec((TM_MLP, NSA_WIDTH),
                      lambda i: ((i // tps) * half + jnp.maximum(i % tps - half, 0), 0))
    halo = pl.BlockSpec((POOL_HALO, POOL_WIDTH),
                        lambda i: (jnp.maximum(i * halo_per_tile - 1, 0), 0))
    res = lambda a: _resident(a.shape, lambda i: (0,) * a.ndim)
    lay = lambda a: _layer_resident(a, layer)
    return pl.pallas_call(
        functools.partial(_mlp_kernel, final=final, tiles_per_seq=seq // TM_MLP),
        grid=(n // TM_MLP,),
        in_specs=[row(D_MODEL), row(POOL_WIDTH), halo, lo, hi, row(2 * D_MODEL),
                  lay(wp), res(sc), lay(wpp), lay(wpn), lay(wo), res(nm), lay(w1), lay(w2),
                  res(nf)],
        out_specs=row(D_MODEL),
        out_shape=jax.ShapeDtypeStruct((n, D_MODEL), F32),
        compiler_params=pltpu.CompilerParams(
            dimension_semantics=("arbitrary",), vmem_limit_bytes=VMEM_LIMIT),
        name="merge_mlp",
    )(x2, u, u, y_lo, y_hi, gm, wp, sc, wpp, wpn, wo, nm, w1, w2, nf)


def _rope_tables(pos):
    inv = ROPE_THETA ** (-jnp.arange(0, HEAD_DIM, 2, dtype=F32) / HEAD_DIM)
    ang = pos.astype(F32)[:, None] * inv[None, :]
    ang = jnp.concatenate([ang, ang, ang, ang], axis=-1)
    first_half = (jnp.arange(LANES) % HEAD_DIM) < HEAD_DIM // 2
    cos, sin = jnp.cos(ang), jnp.sin(ang)
    return cos, jnp.where(first_half, -sin, 0.0), jnp.where(first_half, 0.0, sin)


def _permute_w_in(w_in):
    o_q = POOL_WIDTH
    o_kv = o_q + NSA_WIDTH
    o_gn = o_kv + 6 * KV_WIDTH
    o_gm = o_gn + N_GATE
    depth = w_in.shape[0]
    gn = w_in[:, :, o_gn:o_gm].reshape(depth, D_MODEL, N_KV_GROUPS, HPG, 3)
    gn = gn.transpose(0, 1, 2, 4, 3).reshape(depth, D_MODEL, N_KV_GROUPS, 3 * HPG)
    gn = jnp.pad(gn, ((0, 0), (0, 0), (0, 0), (0, LANES // N_KV_GROUPS - 3 * HPG)))
    gn = gn.reshape(depth, D_MODEL, LANES)
    return jnp.concatenate(
        [w_in[:, :, 0:o_gn], w_in[:, :, o_gm:], gn], axis=-1).astype(BF16)


def kernel(x, norm_mix, w_in, w_pool, pool_scale, pe_k, pe_v, w_ck1, w_ck2, w_cv1, w_cv2,
           w_proj_pool, w_proj_nsa, w_out, norm_mlp, w_ff1, w_ff2, norm_final):
    b, seq, d = x.shape
    depth = w_in.shape[0]
    n = b * seq
    n_chunks = seq // CMP_STRIDE
    n_slc = seq // SEL_BLOCK
    assert n_slc <= HEAD_DIM, "the selection one-hot shares the 64 spare key lanes"
    assert seq >= WIN_KEYS and seq % TM_IN == 0

    w_in_p = _permute_w_in(w_in)
    w_pool_b = w_pool.astype(BF16)
    pe = jnp.stack([pe_k, pe_v], axis=1).reshape(depth, 2, 1, CMP_LEN * HEAD_DIM)
    w_c1 = jnp.stack([w_ck1, w_cv1], axis=1).astype(BF16)
    pad = jnp.zeros_like(w_ck2)
    w_c2 = jnp.stack([jnp.concatenate([w, pad], axis=-1) for w in (w_ck2, w_ck2, w_cv2, w_cv2)],
                     axis=1).astype(BF16)
    wpp, wpn, wo = w_proj_pool.astype(BF16), w_proj_nsa.astype(BF16), w_out.astype(BF16)
    w1, w2 = w_ff1.astype(BF16), w_ff2.astype(BF16)

    cos, slo, shi = _rope_tables(jnp.arange(seq))
    ccos, cslo, cshi = _rope_tables(jnp.arange(n_chunks) * CMP_STRIDE + CMP_LEN - 1)
    ident = (jnp.ones_like(ccos), jnp.zeros_like(cslo), jnp.zeros_like(cshi))
    cmp_tabs = [jnp.stack([t, i_], axis=0) for t, i_ in zip((ccos, cslo, cshi), ident)]
    cmp_start = jnp.arange(n_chunks) * CMP_STRIDE
    slc_start = jnp.arange(n_slc) * SEL_BLOCK
    ovt = ((cmp_start[None, :] <= slc_start[:, None] + SEL_BLOCK - 1)
           & (cmp_start[None, :] + CMP_LEN - 1 >= slc_start[:, None])).astype(BF16)
    k_loc = jnp.arange(NK_SEL)[:, None]
    t_loc = jnp.arange(TQ)[None, :]
    keep = jnp.stack([jnp.ones((NK_SEL, TQ), bool), k_loc <= t_loc, k_loc > t_loc,
                      jnp.zeros((NK_SEL, TQ), bool)])
    pat = jnp.where(keep, 0.0, NEG_INF).astype(F32)

    x2 = x.reshape(n, d)
    for l in range(depth):
        u, q, c4, ks, vst, kw, vwt, gm, gn = _inproj(
            x2, norm_mix[l][None, :], w_in_p, l, cos, slo, shi, b, seq)
        cmp_n, cmp_t = _compress(c4, pe[l], w_c1[l], w_c2[l], *cmp_tabs)
        n_i = seq // TQ // 2
        y_lo, y_hi = [
            _attention(q, gn, cmp_n, cmp_t, ks, vst, kw, vwt, ovt, pat, i0, n_i).reshape(
                n // 2, NSA_WIDTH) for i0 in (0, n_i)]
        x2 = _merge_mlp(
            x2, u, y_lo, y_hi, gm, w_pool_b, pool_scale[l][None, :],
            wpp, wpn, wo, norm_mlp[l][None, :], w1, w2, norm_final[None, :],
            layer=l, final=(l == depth - 1), seq=seq)
    return x2.reshape(b, seq, d)
```

```python
import functools

import jax
import jax.numpy as jnp
import numpy as np
from jax import lax
from jax.experimental import pallas as pl
from jax.experimental.pallas import tpu as pltpu

F32 = jnp.float32
BF16 = jnp.bfloat16

D_MODEL = 1024
POOL_WINDOWS = (2, 4, 8, 16)
POOL_WIDTH = 512
POOL_GW = 128
N_HEADS = 16
HEAD_DIM = 64
N_KV_GROUPS = 2
HPG = 8
NSA_WIDTH = 1024
KV_WIDTH = 128
CMP_LEN = 32
CMP_STRIDE = 16
CMP_HIDDEN = 256
SEL_BLOCK = 64
N_SEL = 16
WINDOW = 512
SEL_BONUS = 1e4
NEG_INF = -1e30
ROPE_THETA = 10000.0
D_FF = 4096
RMS_EPS = 1e-6
N_GATE = 3 * N_HEADS
Q_SCALE = HEAD_DIM ** -0.5 * float(np.log2(np.e))

LANES = 128
VMEM_LIMIT = 56 * 1024 * 1024

C_U = 0
C_Q = C_U + POOL_WIDTH
C_KV = C_Q + NSA_WIDTH
C_GM = C_KV + 6 * KV_WIDTH
C_GN = C_GM + 2 * D_MODEL
N_INP = C_GN + LANES

TM_IN = 512
POOL_HALO = 16
TM_MLP = 512
FF_CHUNK = 1024
BATCH_STEP = 2
TQ = 256
NK_SEL = 256
WIN_KEYS = WINDOW + TQ
VT_CHUNK = 128
V_ROWS = HEAD_DIM + 16
PAT_ALL, PAT_CAUSAL, PAT_ABOVE, PAT_NONE = 0, 1, 2, 3
assert NK_SEL == TQ and WINDOW == 2 * NK_SEL and WIN_KEYS % NK_SEL == 0


def _dot(a, b):
    return jnp.dot(a, b, preferred_element_type=F32)


def _rms(x, g):
    return x * lax.rsqrt(jnp.mean(x * x, axis=-1, keepdims=True) + RMS_EPS) * g


def _rope(t, cos, sin_lo, sin_hi):
    return t * cos + pltpu.roll(t, LANES - 32, 1) * sin_lo + pltpu.roll(t, 32, 1) * sin_hi


def _resident(shape, index_map):
    return pl.BlockSpec(shape, index_map, pipeline_mode=pl.Buffered(1))


def _layer_resident(stacked, layer):
    nd = stacked.ndim - 1
    return _resident((None,) + stacked.shape[1:], lambda i: (layer,) + (0,) * nd)


def _value_rows(v_t):
    tail_row = lax.broadcasted_iota(jnp.int32, (V_ROWS - HEAD_DIM, v_t.shape[1]), 0)
    return jnp.concatenate([v_t, jnp.where(tail_row == 0, 1.0, 0.0)], axis=0)


def _inproj_kernel(x_ref, g_ref, w_ref, cos_ref, slo_ref, shi_ref,
                   u_ref, q_ref, cmp_ref, ks_ref, vst_ref, kw_ref, vwt_ref, gm_ref, gn_ref,
                   kv_scr, *, tiles_per_seq):
    h = _rms(x_ref[...], g_ref[...]).astype(BF16)
    cos, slo, shi = cos_ref[...], slo_ref[...], shi_ref[...]
    lane = lax.broadcasted_iota(jnp.int32, (TM_IN, LANES), 1)
    low = lane < HEAD_DIM
    pos = (pl.program_id(0) % tiles_per_seq) * TM_IN + lax.broadcasted_iota(
        jnp.int32, (TM_IN, LANES), 0)
    block_onehot = jnp.where(lane - HEAD_DIM == pos // SEL_BLOCK, 1.0, 0.0)

    u_ref[...] = _dot(h, w_ref[:, C_U:C_Q])
    q = _dot(h, w_ref[:, C_Q:C_KV])
    for k in range(NSA_WIDTH // LANES):
        qt = (_rope(q[:, k * LANES:(k + 1) * LANES], cos, slo, shi) * Q_SCALE).T
        spare = jnp.zeros((LANES - HEAD_DIM, TM_IN), F32)
        for par in range(2):
            q_ref[0, 2 * k + par] = jnp.concatenate(
                [qt[par * HEAD_DIM:(par + 1) * HEAD_DIM, :], spare], axis=0).astype(BF16)
    kv = _dot(h, w_ref[:, C_KV:C_GM])
    n_rows = TM_IN // CMP_STRIDE
    low_c = lax.broadcasted_iota(jnp.int32, (n_rows, LANES), 1) < HEAD_DIM
    for t in range(2):
        kv_scr[t] = kv[:, t * LANES:(t + 1) * LANES]
        for pp in range(CMP_STRIDE // 2):
            a = kv_scr[t, pl.ds(2 * pp, n_rows, stride=CMP_STRIDE), :]
            b = kv_scr[t, pl.ds(2 * pp + 1, n_rows, stride=CMP_STRIDE), :]
            sl = slice(pp * LANES, (pp + 1) * LANES)
            cmp_ref[0, 2 * t, :, sl] = jnp.where(low_c, a, pltpu.roll(b, HEAD_DIM, 1))
            cmp_ref[0, 2 * t + 1, :, sl] = jnp.where(low_c, pltpu.roll(a, HEAD_DIM, 1), b)
    ks = _rope(kv[:, 2 * KV_WIDTH:3 * KV_WIDTH], cos, slo, shi)
    kw = _rope(kv[:, 4 * KV_WIDTH:5 * KV_WIDTH], cos, slo, shi)
    vs_t = kv[:, 3 * KV_WIDTH:4 * KV_WIDTH].T
    vw_t = kv[:, 5 * KV_WIDTH:6 * KV_WIDTH].T
    for g in range(N_KV_GROUPS):
        ks_g = ks if g == 0 else pltpu.roll(ks, HEAD_DIM, 1)
        kw_g = kw if g == 0 else pltpu.roll(kw, HEAD_DIM, 1)
        ks_ref[0, g] = jnp.where(low, ks_g, block_onehot).astype(BF16)
        kw_ref[0, g] = jnp.where(low, kw_g, 0.0).astype(BF16)
        vs_g = _value_rows(vs_t[g * HEAD_DIM:(g + 1) * HEAD_DIM, :]).astype(BF16)
        vw_g = _value_rows(vw_t[g * HEAD_DIM:(g + 1) * HEAD_DIM, :]).astype(BF16)
        for c in range(TM_IN // VT_CHUNK):
            sl = slice(c * VT_CHUNK, (c + 1) * VT_CHUNK)
            vst_ref[0, g, c] = vs_g[:, sl]
            vwt_ref[0, g, c] = vw_g[:, sl]
    gm_ref[...] = _dot(h, w_ref[:, C_GM:C_GN])
    gn_ref[0] = jax.nn.sigmoid(_dot(h, w_ref[:, C_GN:N_INP])).T


def _inproj(x2, g, w, layer, cos, slo, shi, b, seq):
    n = x2.shape[0]
    tiles_per_seq = seq // TM_IN
    n_chunks = TM_IN // VT_CHUNK
    row = lambda w_: pl.BlockSpec((TM_IN, w_), lambda i: (i, 0))
    tab = pl.BlockSpec((TM_IN, LANES), lambda i: (i % tiles_per_seq, 0))
    kg = pl.BlockSpec((1, N_KV_GROUPS, TM_IN, LANES),
                      lambda i: (i // tiles_per_seq, 0, i % tiles_per_seq, 0))
    vt = pl.BlockSpec((1, N_KV_GROUPS, n_chunks, V_ROWS, VT_CHUNK),
                      lambda i: (i // tiles_per_seq, 0, i % tiles_per_seq, 0, 0))
    cmp_spec = pl.BlockSpec(
        (1, 2 * N_KV_GROUPS, TM_IN // CMP_STRIDE, CMP_STRIDE * HEAD_DIM),
        lambda i: (i // tiles_per_seq, 0, i % tiles_per_seq, 0))
    qt_spec = pl.BlockSpec((1, N_HEADS, LANES, TM_IN),
                           lambda i: (i // tiles_per_seq, 0, 0, i % tiles_per_seq))
    gt_spec = pl.BlockSpec((1, LANES, TM_IN), lambda i: (i // tiles_per_seq, 0, i % tiles_per_seq))
    k_shape = jax.ShapeDtypeStruct((b, N_KV_GROUPS, seq, LANES), BF16)
    vt_shape = jax.ShapeDtypeStruct((b, N_KV_GROUPS, seq // VT_CHUNK, V_ROWS, VT_CHUNK), BF16)
    return pl.pallas_call(
        functools.partial(_inproj_kernel, tiles_per_seq=tiles_per_seq),
        grid=(n // TM_IN,),
        in_specs=[row(D_MODEL), _resident((1, D_MODEL), lambda i: (0, 0)),
                  _layer_resident(w, layer), tab, tab, tab],
        out_specs=[row(POOL_WIDTH), qt_spec, cmp_spec, kg, vt,
                   kg, vt, row(2 * D_MODEL), gt_spec],
        out_shape=[
            jax.ShapeDtypeStruct((n, POOL_WIDTH), F32),
            jax.ShapeDtypeStruct((b, N_HEADS, LANES, seq), BF16),
            jax.ShapeDtypeStruct(
                (b, 2 * N_KV_GROUPS, seq // CMP_STRIDE, CMP_STRIDE * HEAD_DIM), F32),
            k_shape, vt_shape, k_shape, vt_shape,
            jax.ShapeDtypeStruct((n, 2 * D_MODEL), F32),
            jax.ShapeDtypeStruct((b, LANES, seq), F32),
        ],
        scratch_shapes=[pltpu.VMEM((2, TM_IN, LANES), F32)],
        compiler_params=pltpu.CompilerParams(
            dimension_semantics=("arbitrary",), vmem_limit_bytes=VMEM_LIMIT),
        name="in_proj",
    )(x2, g, w, cos, slo, shi)


def _pool_mix(cur, prev, t0, wp_ref, sc_ref):
    t = t0 + lax.broadcasted_iota(jnp.int32, (cur.shape[0], POOL_GW), 0)
    out = []
    for g, w in enumerate(POOL_WINDOWS):
        sl = slice(g * POOL_GW, (g + 1) * POOL_GW)
        cg = cur[:, sl]
        s = jnp.concatenate([prev[:, sl], cg], axis=0)
        sh = 1
        while sh < w:
            s = s + pltpu.roll(s, sh, 0)
            sh *= 2
        cnt = jnp.minimum(t + 1, w).astype(F32)
        d = s[POOL_HALO:] / cnt - cg
        out.append((_dot(d.astype(BF16), wp_ref[g]) * sc_ref[:, sl]).astype(BF16))
    return jnp.concatenate(out, axis=1)


def _compress_kernel(c_ref, pe_ref, w1_ref, w2_ref, cos_ref, slo_ref, shi_ref, o_ref, ot_ref):
    half = CMP_STRIDE * HEAD_DIM
    c = c_ref[...]
    a = _dot((c + pe_ref[:, 0:half]).astype(BF16), w1_ref[0:half, :])
    b = _dot((c + pe_ref[:, half:2 * half]).astype(BF16), w1_ref[half:2 * half, :])
    n_rows = c.shape[0]
    hid = a + pltpu.roll(b, n_rows - 1, 0)
    act = jax.nn.gelu(hid, approximate=True)
    out = _dot(act.astype(BF16), w2_ref[...])
    out = _rope(out, cos_ref[...], slo_ref[...], shi_ref[...])
    row = lax.broadcasted_iota(jnp.int32, out.shape, 0)
    out = jnp.where(row < n_rows - 1, out, 0.0)
    o_ref[...] = out.astype(BF16)
    ot_ref[...] = _value_rows(out.T[0:HEAD_DIM, :]).astype(BF16)


def _compress(c4, pe, w1, w2p, cos, slo, shi):
    b, n_kv, n_chunks, half = c4.shape
    return pl.pallas_call(
        _compress_kernel,
        grid=(b, n_kv),
        in_specs=[
            pl.BlockSpec((None, None, n_chunks, half), lambda bi, j: (bi, j, 0, 0)),
            pl.BlockSpec((None, 1, 2 * half), lambda bi, j: (j // N_KV_GROUPS, 0, 0)),
            pl.BlockSpec((None, 2 * half, CMP_HIDDEN), lambda bi, j: (j // N_KV_GROUPS, 0, 0)),
            pl.BlockSpec((None, CMP_HIDDEN, LANES), lambda bi, j: (j, 0, 0)),
            pl.BlockSpec((None, n_chunks, LANES), lambda bi, j: (j // N_KV_GROUPS, 0, 0)),
            pl.BlockSpec((None, n_chunks, LANES), lambda bi, j: (j // N_KV_GROUPS, 0, 0)),
            pl.BlockSpec((None, n_chunks, LANES), lambda bi, j: (j // N_KV_GROUPS, 0, 0)),
        ],
        out_specs=[
            pl.BlockSpec((None, None, n_chunks, LANES), lambda bi, j: (bi, j, 0, 0)),
            pl.BlockSpec((None, None, V_ROWS, n_chunks), lambda bi, j: (bi, j, 0, 0)),
        ],
        out_shape=[
            jax.ShapeDtypeStruct((b, n_kv, n_chunks, LANES), BF16),
            jax.ShapeDtypeStruct((b, n_kv, V_ROWS, n_chunks), BF16),
        ],
        compiler_params=pltpu.CompilerParams(dimension_semantics=("arbitrary", "arbitrary")),
        name="compress",
    )(c4, pe, w1, w2p, cos, slo, shi)


def _block_rank(score):
    n_slc = score.shape[0]
    sub = 8
    ranks = []
    for v in range(n_slc // sub):
        blk = score[v * sub:(v + 1) * sub, :]
        jb_v = v * sub + lax.broadcasted_iota(jnp.int32, blk.shape, 0)
        r = jnp.zeros(blk.shape, F32)
        for jp in range(n_slc):
            row = score[jp:jp + 1, :]
            ge = jnp.where(row >= blk, 1.0, 0.0)
            gt = jnp.where(row > blk, 1.0, 0.0)
            if jp < v * sub:
                r = r + ge
            elif jp >= (v + 1) * sub:
                r = r + gt
            else:
                r = r + jnp.where(jb_v > jp, ge, gt)
        ranks.append(r)
    return jnp.concatenate(ranks, axis=0)


def _attn_kernel(q_ref, gn_ref, kc_ref, vct_ref, ks_ref, vst_ref, kw_ref, vwt_ref, ovt_ref,
                 pat_ref, o_ref, qsel_ref, s_ref, p_ref, al_ref, mc_ref, mw_ref, ms_ref,
                 accc_ref, accw_ref, accs_ref, *, i0):
    i = pl.program_id(1) + i0
    s0 = i * TQ
    n_cmp = kc_ref.shape[2]
    n_slc = ovt_ref.shape[0]
    n_ch = BATCH_STEP * N_HEADS
    groups = range(BATCH_STEP * N_KV_GROUPS)
    group_of = lambda ch: ch // HPG
    seq_of = lambda ch: ch // N_HEADS

    q_plain = lambda ch: q_ref[seq_of(ch), ch % N_HEADS]
    q_selected = lambda ch: qsel_ref[ch]

    def gate_rows(c, ch):
        row = (group_of(ch) % N_KV_GROUPS) * HEAD_DIM + c * HPG + ch % HPG
        return gn_ref[seq_of(ch), row:row + 1, :]

    def key_tile(ref, k0, nk):
        k0 = pl.multiple_of(k0, VT_CHUNK)
        return [ref[g // N_KV_GROUPS, g % N_KV_GROUPS, pl.ds(k0, nk), :] for g in groups]

    def value_tile(ref, k0, nk):
        c0 = k0 // VT_CHUNK
        return [jnp.concatenate([ref[g // N_KV_GROUPS, g % N_KV_GROUPS, c0 + c]
                                 for c in range(nk // VT_CHUNK)], axis=1) for g in groups]

    imp_raw = []

    def run_step(soft=None, score=None, value=None, importance=False):
        for ch in range(n_ch):
            if soft is not None:
                nk, m_ref, first = soft
                s = s_ref[ch, 0:nk, :]
                mx = jnp.max(s, axis=0, keepdims=True)
                if first:
                    m_new, al_new = mx, None
                else:
                    m_prev = m_ref[ch:ch + 1, :]
                    m_new = jnp.maximum(m_prev, mx)
                    al_new = jnp.exp2(m_prev - m_new)
                p_new = jnp.exp2(s - m_new).astype(BF16)
            if score is not None:
                kts, q_chunk, bias = score
                kt = kts[group_of(ch)]
                sc = _dot(kt, q_chunk(ch))
                s_ref[ch, 0:kt.shape[0], :] = sc if bias is None else sc + bias()
            if value is not None:
                vts, acc_ref, vfirst = value
                vt = vts[group_of(ch)]
                pv = _dot(vt, p_ref[0:vt.shape[1], ch * TQ:(ch + 1) * TQ])
                acc_ref[ch] = pv if vfirst else al_ref[ch:ch + 1, :] * acc_ref[ch] + pv
            if importance:
                imp_raw.append(_dot(ovt_ref[...], p_ref[0:n_cmp, ch * TQ:(ch + 1) * TQ]))
            if soft is not None:
                p_ref[0:nk, ch * TQ:(ch + 1) * TQ] = p_new
                m_ref[ch:ch + 1, :] = m_new
                if al_new is not None:
                    al_ref[ch:ch + 1, :] = al_new

    n_idx = lax.broadcasted_iota(jnp.int32, (n_cmp, TQ), 0)
    t_cmp = s0 + lax.broadcasted_iota(jnp.int32, (n_cmp, TQ), 1)
    cmp_mask = jnp.where(n_idx * CMP_STRIDE + CMP_LEN - 1 <= t_cmp, 0.0, NEG_INF)
    cmp_bias = lambda: cmp_mask

    w0 = jnp.maximum(s0 - WINDOW, 0)
    win_tiles = [(off, min(NK_SEL, WIN_KEYS - off)) for off in range(0, WIN_KEYS, NK_SEL)]
    n_win = len(win_tiles)

    def win_bias(off, nk):
        if i0 * TQ >= WINDOW:
            static_pat = {0: PAT_ABOVE, WINDOW: PAT_CAUSAL}.get(off)
            return None if static_pat is None else (lambda: pat_ref[static_pat])
        d = w0 + off - s0
        pat = jnp.where(d == -WINDOW, PAT_ABOVE,
                        jnp.where(d == 0, PAT_CAUSAL, jnp.where(d < 0, PAT_ALL, PAT_NONE)))
        return lambda: pat_ref[pat]

    def win_step(k):
        args = {}
        if k < n_win:
            off, nk = win_tiles[k]
            args["score"] = (key_tile(kw_ref, w0 + off, nk), q_plain, win_bias(off, nk))
        if 1 <= k <= n_win:
            args["soft"] = (win_tiles[k - 1][1], mw_ref, k == 1)
        if 2 <= k <= n_win + 1:
            off, nk = win_tiles[k - 2]
            args["value"] = (value_tile(vwt_ref, w0 + off, nk), accw_ref, k == 2)
        return args

    run_step(score=([kc_ref[g // N_KV_GROUPS, g % N_KV_GROUPS] for g in groups], q_plain, cmp_bias))
    run_step(soft=(n_cmp, mc_ref, True), **win_step(0))
    run_step(value=([vct_ref[g // N_KV_GROUPS, N_KV_GROUPS + g % N_KV_GROUPS] for g in groups],
                    accc_ref, True), importance=True,
             **win_step(1))
    for k in range(2, n_win):
        run_step(**win_step(k))

    cmp_scale = [jnp.where(mc_ref[ch:ch + 1, :] > 0.5 * NEG_INF,
                           1.0 / accc_ref[ch, HEAD_DIM:HEAD_DIM + 1, :], 0.0)
                 for ch in range(n_ch)]

    jb = lax.broadcasted_iota(jnp.int32, (n_slc, TQ), 0)
    tq = s0 + lax.broadcasted_iota(jnp.int32, (n_slc, TQ), 1)
    causal = jb * SEL_BLOCK <= tq
    near = jnp.logical_or(jb == 0, jb >= tq // SEL_BLOCK - 1)
    for g in groups:
        heads_g = range(g * HPG, (g + 1) * HPG)
        imp = functools.reduce(lambda a, b: a + b, [imp_raw[ch] * cmp_scale[ch] for ch in heads_g])
        score = jnp.where(causal, jnp.where(near, SEL_BONUS, imp), NEG_INF)
        rank = jnp.concatenate(
            [_block_rank(score[:, c0:c0 + LANES]) for c0 in range(0, TQ, LANES)], axis=1)
        sel_bias = jnp.where(causal, jnp.where(rank < float(N_SEL), 0.0, NEG_INF), NEG_INF)
        parts = [jnp.zeros((HEAD_DIM, TQ), F32), sel_bias]
        if n_slc < HEAD_DIM:
            parts.append(jnp.zeros((HEAD_DIM - n_slc, TQ), F32))
        sel_rows = jnp.concatenate(parts, axis=0).astype(BF16)
        for hh in heads_g:
            qsel_ref[hh] = q_plain(hh) + sel_rows

    ms_ref[...] = jnp.full(ms_ref.shape, NEG_INF, F32)
    accs_ref[...] = jnp.zeros(accs_ref.shape, F32)
    n_full = s0 // NK_SEL

    def sel_tile(m):
        return jnp.where(m == 0, n_full, m - 1)

    def sel_scores(m, bias=None):
        return (key_tile(ks_ref, sel_tile(m) * NK_SEL, NK_SEL), q_selected, bias)

    def sel_values(m, live):
        vt = value_tile(vst_ref, sel_tile(jnp.maximum(m, 0)) * NK_SEL, NK_SEL)
        return [jnp.where(live, v, jnp.zeros_like(v)) for v in vt]

    run_step(score=sel_scores(0, lambda: pat_ref[PAT_CAUSAL]), **win_step(n_win))
    run_step(**win_step(n_win + 1))

    def sel_body(m, carry):
        run_step(soft=(NK_SEL, ms_ref, False), score=sel_scores(m),
                 value=(sel_values(m - 2, m >= 2), accs_ref, False))
        return carry

    lax.fori_loop(1, n_full + 1, sel_body, 0)
    run_step(soft=(NK_SEL, ms_ref, False),
             value=(sel_values(n_full - 1, n_full >= 1), accs_ref, False))
    run_step(value=(sel_values(n_full, True), accs_ref, False))

    heads = []
    for ch in range(n_ch):
        coef_c = gate_rows(0, ch) * cmp_scale[ch]
        coef_w = gate_rows(2, ch) / accw_ref[ch, HEAD_DIM:HEAD_DIM + 1, :]
        coef_s = gate_rows(1, ch) / accs_ref[ch, HEAD_DIM:HEAD_DIM + 1, :]
        heads.append(accc_ref[ch, 0:HEAD_DIM, :] * coef_c + accw_ref[ch, 0:HEAD_DIM, :] * coef_w
                     + accs_ref[ch, 0:HEAD_DIM, :] * coef_s)
    for pair in range(n_ch // 2):
        both = jnp.concatenate(heads[2 * pair:2 * pair + 2], axis=0)
        col = (pair % (N_HEADS // 2)) * LANES
        o_ref[pair // (N_HEADS // 2), :, col:col + LANES] = both.T.astype(BF16)


def _attention(q3, gn3, kcmp, vcmp_t, ks4, vst, kw4, vwt, ovt, pat, i0, n_i):
    b, _, _, seq = q3.shape
    last = (i0 + n_i) * TQ - 1
    n_cmp = min(kcmp.shape[2], -(-((last - CMP_LEN + 1) // CMP_STRIDE + 1) // LANES) * LANES)
    n_slc = min(seq // SEL_BLOCK, -(-(last // SEL_BLOCK + 1) // 16) * 16)
    chunks_per_seq = seq // VT_CHUNK
    whole = lambda bi, i: (bi, 0, 0, 0)
    once = lambda shape, imap: pl.BlockSpec(shape, imap, pipeline_mode=pl.Buffered(1))
    return pl.pallas_call(
        functools.partial(_attn_kernel, i0=i0),
        grid=(b // BATCH_STEP, n_i),
        in_specs=[
            pl.BlockSpec((BATCH_STEP, N_HEADS, LANES, TQ), lambda bi, i: (bi, 0, 0, i + i0)),
            pl.BlockSpec((BATCH_STEP, LANES, TQ), lambda bi, i: (bi, 0, i + i0)),
            once((BATCH_STEP, 2 * N_KV_GROUPS, n_cmp, LANES), whole),
            once((BATCH_STEP, 2 * N_KV_GROUPS, V_ROWS, n_cmp), whole),
            once((BATCH_STEP, N_KV_GROUPS, seq, LANES), whole),
            once((BATCH_STEP, N_KV_GROUPS, chunks_per_seq, V_ROWS, VT_CHUNK),
                 lambda bi, i: (bi, 0, 0, 0, 0)),
            once((BATCH_STEP, N_KV_GROUPS, seq, LANES), whole),
            once((BATCH_STEP, N_KV_GROUPS, chunks_per_seq, V_ROWS, VT_CHUNK),
                 lambda bi, i: (bi, 0, 0, 0, 0)),
            pl.BlockSpec((n_slc, n_cmp), lambda bi, i: (0, 0)),
            _resident(pat.shape, lambda bi, i: (0, 0, 0)),
        ],
        out_specs=pl.BlockSpec((BATCH_STEP, TQ, NSA_WIDTH), lambda bi, i: (bi, i, 0)),
        out_shape=jax.ShapeDtypeStruct((b, n_i * TQ, NSA_WIDTH), BF16),
        scratch_shapes=[
            pltpu.VMEM((BATCH_STEP * N_HEADS,LANES, TQ), BF16),
            pltpu.VMEM((BATCH_STEP * N_HEADS,NK_SEL, TQ), F32),
            pltpu.VMEM((NK_SEL, BATCH_STEP * N_HEADS * TQ), BF16),
            pltpu.VMEM((BATCH_STEP * N_HEADS,TQ), F32),
            pltpu.VMEM((BATCH_STEP * N_HEADS,TQ), F32),
            pltpu.VMEM((BATCH_STEP * N_HEADS,TQ), F32),
            pltpu.VMEM((BATCH_STEP * N_HEADS,TQ), F32),
            pltpu.VMEM((BATCH_STEP * N_HEADS,V_ROWS, TQ), F32),
            pltpu.VMEM((BATCH_STEP * N_HEADS,V_ROWS, TQ), F32),
            pltpu.VMEM((BATCH_STEP * N_HEADS,V_ROWS, TQ), F32),
        ],
        compiler_params=pltpu.CompilerParams(
            dimension_semantics=("arbitrary", "arbitrary"),
            vmem_limit_bytes=VMEM_LIMIT),
        name="nsa_attention",
    )(q3, gn3, kcmp, vcmp_t, ks4, vst, kw4, vwt, ovt, pat)


def _mlp_kernel(x_ref, u_ref, up_ref, ylo_ref, yhi_ref, gm_ref, wp_ref, sc_ref, wpp_ref, wpn_ref, wo_ref,
                nm_ref, w1_ref, w2_ref, nf_ref, o_ref, *, final, tiles_per_seq):
    st = pl.program_id(0) % tiles_per_seq
    prev = jnp.where(st > 0, up_ref[...], 0.0)
    y_pool = _pool_mix(u_ref[...], prev, st * TM_MLP, wp_ref, sc_ref)
    p1 = _dot(y_pool, wpp_ref[...])
    y_nsa = jnp.where(2 * st < tiles_per_seq, ylo_ref[...], yhi_ref[...])
    p2 = _dot(y_nsa, wpn_ref[...])
    ga = jax.nn.sigmoid(gm_ref[:, 0:D_MODEL])
    gb = jax.nn.sigmoid(gm_ref[:, D_MODEL:2 * D_MODEL])
    merged = ga * p1 + gb * p2
    x = x_ref[...] + _dot(merged.astype(BF16), wo_ref[...])
    h = _rms(x, nm_ref[...]).astype(BF16)
    acc = jnp.zeros((TM_MLP, D_MODEL), F32)
    for c in range(D_FF // FF_CHUNK):
        sl = slice(c * FF_CHUNK, (c + 1) * FF_CHUNK)
        a = jnp.square(jnp.maximum(_dot(h, w1_ref[:, sl]), 0.0)).astype(BF16)
        acc = acc + _dot(a, w2_ref[sl, :])
    x = x + acc
    if final:
        x = _rms(x, nf_ref[...])
    o_ref[...] = x


def _merge_mlp(x2, u, y_lo, y_hi, gm, wp, sc, wpp, wpn, wo, nm, w1, w2, nf, layer, final, seq):
    n = x2.shape[0]
    halo_per_tile = TM_MLP // POOL_HALO
    tps = seq // TM_MLP
    half = tps // 2
    row = lambda w_: pl.BlockSpec((TM_MLP, w_), lambda i: (i, 0))
    lo = pl.BlockSpec((TM_MLP, NSA_WIDTH),
                      lambda i: ((i // tps) * half + jnp.minimum(i % tps, half - 1), 0))
    hi = pl.BlockSpec((TM_MLP, NSA_WIDTH),
                      lambda i: ((i // tps) * half + jnp.maximum(i % tps - half, 0), 0))
    halo = pl.BlockSpec((POOL_HALO, POOL_WIDTH),
                        lambda i: (jnp.maximum(i * halo_per_tile - 1, 0), 0))
    res = lambda a: _resident(a.shape, lambda i: (0,) * a.ndim)
    lay = lambda a: _layer_resident(a, layer)
    return pl.pallas_call(
        functools.partial(_mlp_kernel, final=final, tiles_per_seq=seq // TM_MLP),
        grid=(n // TM_MLP,),
        in_specs=[row(D_MODEL), row(POOL_WIDTH), halo, lo, hi, row(2 * D_MODEL),
                  lay(wp), res(sc), lay(wpp), lay(wpn), lay(wo), res(nm), lay(w1), lay(w2),
                  res(nf)],
        out_specs=row(D_MODEL),
        out_shape=jax.ShapeDtypeStruct((n, D_MODEL), F32),
        compiler_params=pltpu.CompilerParams(
            dimension_semantics=("arbitrary",), vmem_limit_bytes=VMEM_LIMIT),
        name="merge_mlp",
    )(x2, u, u, y_lo, y_hi, gm, wp, sc, wpp, wpn, wo, nm, w1, w2, nf)


def _rope_tables(pos):
    inv = ROPE_THETA ** (-jnp.arange(0, HEAD_DIM, 2, dtype=F32) / HEAD_DIM)
    ang = pos.astype(F32)[:, None] * inv[None, :]
    ang = jnp.concatenate([ang, ang, ang, ang], axis=-1)
    first_half = (jnp.arange(LANES) % HEAD_DIM) < HEAD_DIM // 2
    cos, sin = jnp.cos(ang), jnp.sin(ang)
    return cos, jnp.where(first_half, -sin, 0.0), jnp.where(first_half, 0.0, sin)


def _permute_w_in(w_in):
    o_q = POOL_WIDTH
    o_kv = o_q + NSA_WIDTH
    o_gn = o_kv + 6 * KV_WIDTH
    o_gm = o_gn + N_GATE
    depth = w_in.shape[0]
    gn = w_in[:, :, o_gn:o_gm].reshape(depth, D_MODEL, N_KV_GROUPS, HPG, 3)
    gn = gn.transpose(0, 1, 2, 4, 3).reshape(depth, D_MODEL, N_KV_GROUPS, 3 * HPG)
    gn = jnp.pad(gn, ((0, 0), (0, 0), (0, 0), (0, LANES // N_KV_GROUPS - 3 * HPG)))
    gn = gn.reshape(depth, D_MODEL, LANES)
    return jnp.concatenate(
        [w_in[:, :, 0:o_gn], w_in[:, :, o_gm:], gn], axis=-1).astype(BF16)


def kernel(x, norm_mix, w_in, w_pool, pool_scale, pe_k, pe_v, w_ck1, w_ck2, w_cv1, w_cv2,
           w_proj_pool, w_proj_nsa, w_out, norm_mlp, w_ff1, w_ff2, norm_final):
    b, seq, d = x.shape
    depth = w_in.shape[0]
    n = b * seq
    n_chunks = seq // CMP_STRIDE
    n_slc = seq // SEL_BLOCK
    assert n_slc <= HEAD_DIM, "the selection one-hot shares the 64 spare key lanes"
    assert seq >= WIN_KEYS and seq % TM_IN == 0 and b % BATCH_STEP == 0

    w_in_p = _permute_w_in(w_in)
    w_pool_b = w_pool.astype(BF16)
    pe = jnp.stack([pe_k, pe_v], axis=1).reshape(depth, 2, 1, CMP_LEN * HEAD_DIM)
    w_c1 = jnp.stack([w_ck1, w_cv1], axis=1).astype(BF16)
    pad = jnp.zeros_like(w_ck2)
    w_c2 = jnp.stack([jnp.concatenate([w, pad], axis=-1) for w in (w_ck2, w_ck2, w_cv2, w_cv2)],
                     axis=1).astype(BF16)
    wpp, wpn, wo = w_proj_pool.astype(BF16), w_proj_nsa.astype(BF16), w_out.astype(BF16)
    w1, w2 = w_ff1.astype(BF16), w_ff2.astype(BF16)

    cos, slo, shi = _rope_tables(jnp.arange(seq))
    ccos, cslo, cshi = _rope_tables(jnp.arange(n_chunks) * CMP_STRIDE + CMP_LEN - 1)
    ident = (jnp.ones_like(ccos), jnp.zeros_like(cslo), jnp.zeros_like(cshi))
    cmp_tabs = [jnp.stack([t, i_], axis=0) for t, i_ in zip((ccos, cslo, cshi), ident)]
    cmp_start = jnp.arange(n_chunks) * CMP_STRIDE
    slc_start = jnp.arange(n_slc) * SEL_BLOCK
    ovt = ((cmp_start[None, :] <= slc_start[:, None] + SEL_BLOCK - 1)
           & (cmp_start[None, :] + CMP_LEN - 1 >= slc_start[:, None])).astype(BF16)
    k_loc = jnp.arange(NK_SEL)[:, None]
    t_loc = jnp.arange(TQ)[None, :]
    keep = jnp.stack([jnp.ones((NK_SEL, TQ), bool), k_loc <= t_loc, k_loc > t_loc,
                      jnp.zeros((NK_SEL, TQ), bool)])
    pat = jnp.where(keep, 0.0, NEG_INF).astype(F32)

    x2 = x.reshape(n, d)
    for l in range(depth):
        u, q, c4, ks, vst, kw, vwt, gm, gn = _inproj(
            x2, norm_mix[l][None, :], w_in_p, l, cos, slo, shi, b, seq)
        cmp_n, cmp_t = _compress(c4, pe[l], w_c1[l], w_c2[l], *cmp_tabs)
        n_i = seq // TQ // 2
        y_lo, y_hi = [
            _attention(q, gn, cmp_n, cmp_t, ks, vst, kw, vwt, ovt, pat, i0, n_i).reshape(
                n // 2, NSA_WIDTH) for i0 in (0, n_i)]
        x2 = _merge_mlp(
            x2, u, y_lo, y_hi, gm, w_pool_b, pool_scale[l][None, :],
            wpp, wpn, wo, norm_mlp[l][None, :], w1, w2, norm_final[None, :],
            layer=l, final=(l == depth - 1), seq=seq)
    return x2.reshape(b, seq, d)
```
